```python
import math
import jax
import jax.numpy as jnp
from jax import lax
import numpy as np

D_MODEL = 1024
BATCH = 2
SEQ = 16384
DEPTH = 4
DEC_BATCH = 32
DEC_SEQ = 16
PAST_LEN = 1024

CHUNK = 64
EPS = 1e-6
D_FF = 2816
A_HEADS = 4
A_DK = 128
A_DV = 128
B_HEADS = 4
B_DK = 128
B_DV = 128
C_HEADS = 4
C_DK = 64
C_DV = 128
C_RANK = 16
GLA_GATE_TEMP = 16.0
N_BRANCH = 3
BRANCH_W = 512

IN_SIZES = (
    A_HEADS * A_DK, A_HEADS * A_DK, A_HEADS * A_DV, A_HEADS * A_DV,
    B_HEADS * B_DK, B_HEADS * B_DK, B_HEADS * B_DV, B_HEADS * B_DV, 2 * B_HEADS,
    C_HEADS * C_DK, C_HEADS * C_DK, C_HEADS * C_DV, C_HEADS * C_DV, C_RANK,
    N_BRANCH * D_MODEL,
)
D_IN = 4 * 512 + (4 * 512 + 8) + (256 + 256 + 512 + 512 + 16) + 3 * 1024

kernel_name = "hybrid_streaming_hgrn2_mlstm_gla_step"


def split_points():
    return tuple(int(v) for v in np.cumsum(np.array(IN_SIZES))[:-1])


def rms_norm(x, w):
    xf = x.astype(jnp.float32)
    y = xf * lax.rsqrt(jnp.mean(xf * xf, axis=-1, keepdims=True) + EPS)
    return (y * w.astype(jnp.float32)).astype(x.dtype)


def head_rms_norm(o, w):
    n_heads = o.shape[2]
    y = rms_norm(o, w.reshape(n_heads, -1))
    return y.reshape(o.shape[0], o.shape[1], -1)


def half_swiglu(x, norm_w, w_up, w_down):
    h = rms_norm(x, norm_w)
    g, u = jnp.split(h @ w_up, 2, axis=-1)
    return x + 0.5 * ((jax.nn.silu(g) * u) @ w_down)


def to_blocks(a, blk):
    b, t, h, d = a.shape
    return a.reshape(b, t // blk, blk, h, d).transpose(1, 0, 3, 2, 4)


def from_blocks(a):
    nb, b, h, blk, d = a.shape
    return a.transpose(1, 0, 3, 2, 4).reshape(b, nb * blk, h, d)


def gated_linear_recurrence(q, k, v, log_f, s0):
    blk = math.gcd(q.shape[1], CHUNK)
    causal = jnp.tril(jnp.ones((blk, blk), dtype=bool))

    def step(state, blocks):
        qb, kb, vb, gb = blocks
        b = jnp.cumsum(gb, axis=2)
        diff = b[:, :, :, None, :] - b[:, :, None, :, :]
        decay = jnp.exp(jnp.where(causal[:, :, None], diff, -jnp.inf))
        scores = jnp.einsum('bhtd,bhsd,bhtsd->bhts', qb, kb, decay)
        o = (jnp.einsum('bhts,bhsv->bhtv', scores, vb)
             + jnp.einsum('bhtd,bhdv->bhtv', qb * jnp.exp(b), state))
        b_last = b[:, :, -1:, :]
        new_state = (jnp.exp(b_last[:, :, 0, :])[..., None] * state
                     + jnp.einsum('bhsd,bhsv->bhdv', kb * jnp.exp(b_last - b), vb))
        return new_state, o

    blocks = tuple(to_blocks(a.astype(jnp.float32), blk) for a in (q, k, v, log_f))
    s_final, o = lax.scan(step, s0.astype(jnp.float32), blocks)
    return from_blocks(o), s_final


def mlstm_recurrence(q, k, v, log_i, log_f, c0, n0, m0):
    blk = math.gcd(q.shape[1], CHUNK)
    causal = jnp.tril(jnp.ones((blk, blk), dtype=bool))

    def step(carry, blocks):
        c, n, m = carry
        qb, kb, vb, ib, fb = blocks
        ib = ib[..., 0]
        b = jnp.cumsum(fb[..., 0], axis=-1)
        d_intra = jnp.where(causal, b[..., :, None] - b[..., None, :] + ib[..., None, :], -jnp.inf)
        d_inter = b + m[..., None]
        m_t = jnp.maximum(d_inter, jnp.max(d_intra, axis=-1))
        w_intra = jnp.exp(d_intra - m_t[..., None]) * jnp.einsum('bhtd,bhsd->bhts', qb, kb)
        w_inter = jnp.exp(d_inter - m_t)
        num = (jnp.einsum('bhts,bhsv->bhtv', w_intra, vb)
               + w_inter[..., None] * jnp.einsum('bhtd,bhdv->bhtv', qb, c))
        den = jnp.sum(w_intra, axis=-1) + w_inter * jnp.einsum('bhtd,bhd->bht', qb, n)
        h = num / jnp.maximum(jnp.abs(den), jnp.exp(-m_t))[..., None]
        e_intra = b[..., -1:] - b + ib
        e_inter = b[..., -1] + m
        m_new = jnp.maximum(e_inter, jnp.max(e_intra, axis=-1))
        w_s = jnp.exp(e_intra - m_new[..., None])
        w_c = jnp.exp(e_inter - m_new)
        c_new = w_c[..., None, None] * c + jnp.einsum('bhs,bhsd,bhsv->bhdv', w_s, kb, vb)
        n_new = w_c[..., None] * n + jnp.einsum('bhs,bhsd->bhd', w_s, kb)
        return (c_new, n_new, m_new), h

    blocks = tuple(to_blocks(a.astype(jnp.float32), blk)
                   for a in (q, k, v, log_i[..., None], log_f[..., None]))
    init = (c0.astype(jnp.float32), n0.astype(jnp.float32), m0.astype(jnp.float32))
    (c_f, n_f, m_f), h = lax.scan(step, init, blocks)
    return from_blocks(h), c_f, n_f, m_f


def token_mixer(x, norm_w, w_in, lb, hgrn_norm_w, mlstm_gate_b, mlstm_norm_w,
                gla_w_a2, gla_b_a, gla_norm_w, w_branch, w_out,
                s_hgrn, c_ml, n_ml, m_ml, s_gla):
    bsz, t_len, _ = x.shape
    f32 = jnp.float32
    h = rms_norm(x, norm_w)
    p = h @ w_in
    (a_q, a_f, a_i, a_g, b_q, b_k, b_v, b_o, b_if,
     c_q, c_k, c_v, c_g, c_lr, merge_g) = jnp.split(p, split_points(), axis=-1)

    def heads(a, n_heads):
        return a.reshape(bsz, t_len, n_heads, -1)

    lbh = lb.reshape(A_HEADS, A_DK)
    a_x = heads(a_f, A_HEADS).astype(f32)
    forget = lbh + (1.0 - lbh) * jax.nn.sigmoid(a_x)
    key_a = (1.0 - lbh) * jax.nn.sigmoid(-a_x)
    o_a, s_hgrn_new = gated_linear_recurrence(
        jax.nn.silu(heads(a_q, A_HEADS)), key_a, heads(a_i, A_HEADS), jnp.log(forget), s_hgrn)
    o_a = head_rms_norm(o_a * jax.nn.sigmoid(heads(a_g, A_HEADS)), hgrn_norm_w)

    gates_b = b_if.astype(f32) + mlstm_gate_b.astype(f32)
    log_i = gates_b[..., :B_HEADS]
    log_f = jax.nn.log_sigmoid(gates_b[..., B_HEADS:])
    h_b, c_new, n_new, m_new = mlstm_recurrence(
        heads(b_q, B_HEADS), heads(b_k, B_HEADS) * (B_DK ** -0.5), heads(b_v, B_HEADS),
        log_i, log_f, c_ml, n_ml, m_ml)
    o_b = jax.nn.sigmoid(b_o) * head_rms_norm(h_b, mlstm_norm_w)

    log_alpha = jax.nn.log_sigmoid((c_lr @ gla_w_a2 + gla_b_a).astype(f32)) / GLA_GATE_TEMP
    o_c, s_gla_new = gated_linear_recurrence(
        heads(c_q, C_HEADS) * (C_DK ** -0.5), heads(c_k, C_HEADS), heads(c_v, C_HEADS),
        heads(log_alpha, C_HEADS), s_gla)
    o_c = head_rms_norm(o_c, gla_norm_w) * jax.nn.silu(c_g)

    branches = jnp.stack([o_a, o_b, o_c], axis=2).astype(x.dtype)
    proj_b = jnp.einsum('btnc,ncd->btnd', branches, w_branch)
    gate = jax.nn.sigmoid(merge_g).reshape(bsz, t_len, N_BRANCH, D_MODEL)
    merged = jnp.sum(gate * proj_b, axis=2)
    return x + merged @ w_out, (s_hgrn_new, c_new, n_new, m_new, s_gla_new)


def run_trunk(x, states, weights):
    (ffn1_norm, ffn1_w_up, ffn1_w_down, mix_norm, w_in, hgrn_lb_logits, hgrn_norm,
     mlstm_gate_bias, mlstm_norm, gla_w_a2, gla_b_a, gla_norm, w_branch, w_out,
     ffn2_norm, ffn2_w_up, ffn2_w_down, final_norm) = weights
    cum = jnp.cumsum(jax.nn.softmax(hgrn_lb_logits.astype(jnp.float32), axis=0), axis=0)
    lower_bounds = cum - cum[0:1]
    new_states = ([], [], [], [], [])
    for l in range(DEPTH):
        x = half_swiglu(x, ffn1_norm[l], ffn1_w_up[l], ffn1_w_down[l])
        x, st = token_mixer(x, mix_norm[l], w_in[l], lower_bounds[l], hgrn_norm[l],
                            mlstm_gate_bias[l], mlstm_norm[l], gla_w_a2[l], gla_b_a[l],
                            gla_norm[l], w_branch[l], w_out[l],
                            states[0][l], states[1][l], states[2][l], states[3][l], states[4][l])
        x = half_swiglu(x, ffn2_norm[l], ffn2_w_up[l], ffn2_w_down[l])
        for acc, s in zip(new_states, st):
            acc.append(s)
    y = rms_norm(x, final_norm)
    return y, tuple(jnp.stack(acc) for acc in new_states)


def setup_inputs(seed: int = 0) -> dict:
    key = jax.random.key(seed)
    ks = jax.random.split(key, 32)
    f32 = jnp.float32

    def nrm(k, shape, scale):
        return scale * jax.random.normal(k, shape, f32)

    def gain(k, shape):
        return 1.0 + 0.01 * jax.random.normal(k, shape, f32)

    f_bias = 3.0 + jnp.linspace(0.0, 3.0, B_HEADS, dtype=f32) + nrm(ks[20], (DEPTH, B_HEADS), 0.1)
    i_bias = nrm(ks[21], (DEPTH, B_HEADS), 0.1)
    return {
        "x_prompt": nrm(ks[0], (BATCH, SEQ, D_MODEL), 1.0),
        "x_sample": nrm(ks[1], (DEC_BATCH, DEC_SEQ, D_MODEL), 1.0),
        "state_hgrn": nrm(ks[2], (DEPTH, DEC_BATCH, A_HEADS, A_DK, A_DV), 0.5),
        "state_mlstm_c": nrm(ks[3], (DEPTH, DEC_BATCH, B_HEADS, B_DK, B_DV), 0.5),
        "state_mlstm_n": nrm(ks[4], (DEPTH, DEC_BATCH, B_HEADS, B_DK), 0.5),
        "state_mlstm_m": nrm(ks[5], (DEPTH, DEC_BATCH, B_HEADS), 1.0),
        "state_gla": nrm(ks[6], (DEPTH, DEC_BATCH, C_HEADS, C_DK, C_DV), 1.0),
        "ffn1_norm": gain(ks[7], (DEPTH, D_MODEL)),
        "ffn1_w_up": nrm(ks[8], (DEPTH, D_MODEL, 2 * D_FF), D_MODEL ** -0.5),
        "ffn1_w_down": nrm(ks[9], (DEPTH, D_FF, D_MODEL), D_FF ** -0.5),
        "mix_norm": gain(ks[10], (DEPTH, D_MODEL)),
        "w_in": nrm(ks[11], (DEPTH, D_MODEL, D_IN), D_MODEL ** -0.5),
        "hgrn_lb_logits": nrm(ks[12], (DEPTH, A_HEADS * A_DK), 0.5),
        "hgrn_norm": gain(ks[13], (DEPTH, A_HEADS * A_DV)),
        "mlstm_gate_bias": jnp.concatenate([i_bias, f_bias], axis=-1),
        "mlstm_norm": gain(ks[14], (DEPTH, B_HEADS * B_DV)),
        "gla_w_a2": nrm(ks[15], (DEPTH, C_RANK, C_HEADS * C_DK), C_RANK ** -0.5),
        "gla_b_a": nrm(ks[16], (DEPTH, C_HEADS * C_DK), 0.1),
        "gla_norm": gain(ks[17], (DEPTH, C_HEADS * C_DV)),
        "w_branch": nrm(ks[18], (DEPTH, N_BRANCH, BRANCH_W, D_MODEL), BRANCH_W ** -0.5),
        "w_out": nrm(ks[19], (DEPTH, D_MODEL, D_MODEL), D_MODEL ** -0.5),
        "ffn2_norm": gain(ks[22], (DEPTH, D_MODEL)),
        "ffn2_w_up": nrm(ks[23], (DEPTH, D_MODEL, 2 * D_FF), D_MODEL ** -0.5),
        "ffn2_w_down": nrm(ks[24], (DEPTH, D_FF, D_MODEL), D_FF ** -0.5),
        "final_norm": gain(ks[25], (D_MODEL,)),
    }


def reference(x_prompt, x_sample, state_hgrn, state_mlstm_c, state_mlstm_n, state_mlstm_m,
              state_gla, ffn1_norm, ffn1_w_up, ffn1_w_down, mix_norm, w_in, hgrn_lb_logits,
              hgrn_norm, mlstm_gate_bias, mlstm_norm, gla_w_a2, gla_b_a, gla_norm, w_branch,
              w_out, ffn2_norm, ffn2_w_up, ffn2_w_down, final_norm):
    weights = (ffn1_norm, ffn1_w_up, ffn1_w_down, mix_norm, w_in, hgrn_lb_logits, hgrn_norm,
               mlstm_gate_bias, mlstm_norm, gla_w_a2, gla_b_a, gla_norm, w_branch, w_out,
               ffn2_norm, ffn2_w_up, ffn2_w_down, final_norm)
    f32 = jnp.float32
    bp = x_prompt.shape[0]
    zero_states = (
        jnp.zeros((DEPTH, bp, A_HEADS, A_DK, A_DV), f32),
        jnp.zeros((DEPTH, bp, B_HEADS, B_DK, B_DV), f32),
        jnp.zeros((DEPTH, bp, B_HEADS, B_DK), f32),
        jnp.zeros((DEPTH, bp, B_HEADS), f32),
        jnp.zeros((DEPTH, bp, C_HEADS, C_DK, C_DV), f32),
    )
    y_prompt, (hgrn_p, mc_p, mn_p, mm_p, gla_p) = run_trunk(x_prompt, zero_states, weights)
    sample_states = (state_hgrn, state_mlstm_c, state_mlstm_n, state_mlstm_m, state_gla)
    y_sample, (hgrn_s, mc_s, mn_s, mm_s, gla_s) = run_trunk(x_sample, sample_states, weights)
    return (y_prompt, y_sample, hgrn_p, mc_p, mn_p, mm_p, gla_p, hgrn_s, mc_s, mn_s, mm_s, gla_s)
```

```python
import functools

import jax
import jax.numpy as jnp
from jax import lax
from jax.experimental import pallas as pl
from jax.experimental.pallas import tpu as pltpu

F32 = jnp.float32
BF16 = jnp.bfloat16

EPS = 1e-6
CHUNK = 64
GLA_GATE_TEMP = 16.0
N_BRANCH = 3
LANE = 128
V7X_VMEM_BYTES = 64 * 1024 * 1024
VMEM_LIMIT = V7X_VMEM_BYTES - 8 * 1024 * 1024
SAFE_DECAY_SPAN = 60.0

_NT = (((1,), (1,)), ((), ()))
_TN = (((0,), (0,)), ((), ()))


def _dot(a, b):
    return jnp.dot(a, b, preferred_element_type=F32)


def _dot_nt(a, b):
    return lax.dot_general(a, b, _NT, preferred_element_type=F32)


def _dot_tn(a, b):
    return lax.dot_general(a, b, _TN, preferred_element_type=F32)


def _rms(x, w):
    return x * lax.rsqrt(jnp.mean(x * x, axis=-1, keepdims=True) + EPS) * w


def _sigmoid(x):
    return 1.0 / (1.0 + jnp.exp(-x))


def _log_sigmoid(x):
    return jnp.minimum(x, 0.0) - jnp.log(1.0 + jnp.exp(-jnp.abs(x)))


def _split3(x):
    hi = x.astype(BF16)
    r = x - hi.astype(F32)
    mid = r.astype(BF16)
    lo = (r - mid.astype(F32)).astype(BF16)
    return hi, mid, lo


def _cumsum_rows(tri, g):
    hi, mid, lo = _split3(g)
    return (_dot(tri, hi) + _dot(tri, mid)) + _dot(tri, lo)


def _rows_to_lanes(sel, x):
    hi, mid, lo = _split3(x)
    return (_dot_nt(sel, hi) + _dot_nt(sel, mid)) + _dot_nt(sel, lo)


def _pick_tile(n, cap):
    t = min(n, cap)
    while n % t or (t % 8 and t != n):
        t -= 1
    return t


def _params(sem):
    return pltpu.CompilerParams(dimension_semantics=sem, vmem_limit_bytes=VMEM_LIMIT)


def _ffn_kernel(x_ref, nw_ref, wg_ref, wu_ref, wd_ref, *rest, n_ff, final):
    if final:
        fw_ref, o_ref, h_scr, acc_scr = rest
    else:
        o_ref, h_scr, acc_scr = rest
    j = pl.program_id(1)

    @pl.when(j == 0)
    def _():
        h_scr[...] = _rms(x_ref[...], nw_ref[...]).astype(BF16)
        acc_scr[...] = jnp.zeros_like(acc_scr)

    h = h_scr[...]
    g = _dot(h, wg_ref[...])
    u = _dot(h, wu_ref[...])
    act = (g * _sigmoid(g) * u).astype(BF16)
    acc_scr[...] += _dot(act, wd_ref[...])

    @pl.when(j == n_ff - 1)
    def _():
        y = x_ref[...] + 0.5 * acc_scr[...]
        if final:
            y = _rms(y, fw_ref[...])
        o_ref[...] = y


def _ffn(x, norm_w, w_up, w_down, layer, final_w=None):
    n, d = x.shape
    d_ff = w_down.shape[1]
    tm = _pick_tile(n, 1024)
    tf = 256
    n_ff = d_ff // tf
    assert d_ff % tf == 0
    final = final_w is not None
    in_specs = [
        pl.BlockSpec((tm, d), lambda i, j: (i, 0)),
        pl.BlockSpec((None, 1, d), lambda i, j: (layer, 0, 0)),
        pl.BlockSpec((None, d, tf), lambda i, j: (layer, 0, j)),
        pl.BlockSpec((None, d, tf), lambda i, j: (layer, 0, j + n_ff)),
        pl.BlockSpec((None, tf, d), lambda i, j: (layer, j, 0)),
    ]
    args = [x, norm_w, w_up, w_up, w_down]
    if final:
        in_specs.append(pl.BlockSpec((1, d), lambda i, j: (0, 0)))
        args.append(final_w)
    return pl.pallas_call(
        functools.partial(_ffn_kernel, n_ff=n_ff, final=final),
        grid=(n // tm, n_ff),
        in_specs=in_specs,
        out_specs=pl.BlockSpec((tm, d), lambda i, j: (i, 0)),
        out_shape=jax.ShapeDtypeStruct((n, d), F32),
        scratch_shapes=[pltpu.VMEM((tm, d), BF16), pltpu.VMEM((tm, d), F32)],
        compiler_params=_params(("parallel", "arbitrary")),
        name="ffn",
    )(*args)


def _inproj_kernel(x_ref, nw_ref, w_ref, oa_ref, of_ref, ob_ref, oc_ref, os_ref, *, widths):
    h = _rms(x_ref[...], nw_ref[...]).astype(BF16)
    c0 = 0
    for ref, width in zip((oa_ref, of_ref, ob_ref, oc_ref, os_ref), widths):
        step = min(width, 512)
        for c in range(0, width, step):
            ref[:, c:c + step] = _dot(h, w_ref[:, c0 + c:c0 + c + step]).astype(ref.dtype)
        c0 += width


def _inproj(x, norm_w, w_cat, layer, widths):
    n, d = x.shape
    tm = _pick_tile(n, 512)
    dtypes = (BF16, F32, BF16, BF16, F32)
    return pl.pallas_call(
        functools.partial(_inproj_kernel, widths=widths),
        grid=(n // tm,),
        in_specs=[
            pl.BlockSpec((tm, d), lambda i: (i, 0)),
            pl.BlockSpec((None, 1, d), lambda i: (layer, 0, 0)),
            pl.BlockSpec((None, d, sum(widths)), lambda i: (layer, 0, 0)),
        ],
        out_specs=[pl.BlockSpec((tm, w), lambda i: (i, 0)) for w in widths],
        out_shape=[jax.ShapeDtypeStruct((n, w), dt) for w, dt in zip(widths, dtypes)],
        compiler_params=_params(("parallel",)),
        name="inproj",
    )(x, norm_w, w_cat)


def _merge_kernel(x_ref, nw_ref, oa_ref, ob_ref, oc_ref, wm_ref, wb_ref, wo_ref, o_ref):
    x = x_ref[...]
    d = x.shape[1]
    h = _rms(x, nw_ref[...]).astype(BF16)
    merged = None
    for nb, br_ref in enumerate((oa_ref, ob_ref, oc_ref)):
        gate = _sigmoid(_dot(h, wm_ref[:, nb * d:(nb + 1) * d]))
        term = gate * _dot(br_ref[...], wb_ref[nb])
        merged = term if merged is None else merged + term
    o_ref[...] = x + _dot(merged.astype(BF16), wo_ref[...])


def _merge(x, norm_w, o_a, o_b, o_c, w_mg, w_branch, w_out, layer):
    n, d = x.shape
    bw = o_a.shape[1]
    tm = _pick_tile(n, 512)
    br_spec = pl.BlockSpec((tm, bw), lambda i: (i, 0))
    return pl.pallas_call(
        _merge_kernel,
        grid=(n // tm,),
        in_specs=[
            pl.BlockSpec((tm, d), lambda i: (i, 0)),
            pl.BlockSpec((None, 1, d), lambda i: (layer, 0, 0)),
            br_spec, br_spec, br_spec,
            pl.BlockSpec((None, d, N_BRANCH * d), lambda i: (layer, 0, 0)),
            pl.BlockSpec((None, N_BRANCH, bw, d), lambda i: (layer, 0, 0, 0)),
            pl.BlockSpec((None, d, d), lambda i: (layer, 0, 0)),
        ],
        out_specs=pl.BlockSpec((tm, d), lambda i: (i, 0)),
        out_shape=jax.ShapeDtypeStruct((n, d), F32),
        compiler_params=_params(("parallel",)),
        name="merge",
    )(x, norm_w, o_a, o_b, o_c, w_mg, w_branch, w_out)


def _chunk_consts(length):
    r = lax.broadcasted_iota(jnp.int32, (length, length), 0)
    c = lax.broadcasted_iota(jnp.int32, (length, length), 1)
    causal = r >= c
    return causal, jnp.where(causal, 1.0, 0.0).astype(BF16)


def _head_norm(o, w):
    return o * lax.rsqrt(jnp.mean(o * o, axis=-1, keepdims=True) + EPS) * w


def _glr_chunk(q, k, g, vs, masks, st, causal, tri):
    length = q.shape[0]
    b = _cumsum_rows(tri, g)
    b_mid = b[length // 2:length // 2 + 1, :]
    b_last = b[length - 1:length, :]
    span = jnp.max(jnp.maximum(b[0:1, :] - b_mid, b_mid - b_last))
    qt = q * jnp.exp(b - b_mid)
    kt = (k * jnp.exp(b_mid - b)).astype(BF16)
    q_in = q * jnp.exp(b)
    k_st = (k * jnp.exp(b_last - b)).astype(BF16)
    st_b = st.astype(BF16)
    outs = []
    upd = None
    for v, m in zip(vs, masks):
        vb = v.astype(BF16)
        qh = qt if m is None else qt * m
        sc = jnp.where(causal, _dot_nt(qh.astype(BF16), kt), 0.0)
        o_intra = _dot(sc.astype(BF16), vb)
        qi = q_in if m is None else q_in * m
        o_inter = _dot_nt(qi.astype(BF16), st_b)
        outs.append((o_intra, o_inter))
        u = _dot_tn(vb, k_st)
        if m is not None:
            u = u * m
        upd = u if upd is None else upd + u
    return outs, st * jnp.exp(b_last) + upd, span


def _glr_intra_exact(q, k, g, vs, masks, tmp_ref, tri):
    length = q.shape[0]
    b = _cumsum_rows(tri, g)
    tmp_ref[0, 0:length, :] = k
    tmp_ref[1, 0:length, :] = b
    for h, v in enumerate(vs):
        tmp_ref[2 + h, 0:length, :] = v
    t_idx = lax.broadcasted_iota(jnp.int32, (length, 1), 0)

    def body(s, accs):
        k_s = tmp_ref[0, pl.ds(s, 1), :]
        b_s = tmp_ref[1, pl.ds(s, 1), :]
        a = q * k_s * jnp.exp(jnp.minimum(b - b_s, 0.0))
        new = []
        for h, m in enumerate(masks):
            ah = a if m is None else a * m
            col = jnp.where(t_idx >= s, jnp.sum(ah, axis=1, keepdims=True), 0.0)
            new.append(accs[h] + col * tmp_ref[2 + h, pl.ds(s, 1), :])
        return tuple(new)

    init = tuple(jnp.zeros((length, v.shape[1]), F32) for v in vs)
    return lax.fori_loop(0, length, body, init)


def _lane_masks(n_sub):
    if n_sub == 1:
        return [None]
    lane = lax.broadcasted_iota(jnp.int32, (1, LANE), 1)
    w = LANE // n_sub
    return [jnp.where((lane >= h * w) & (lane < (h + 1) * w), 1.0, 0.0) for h in range(n_sub)]


def _glr_kernel(*refs, kind, layer, length, n_chunks, n_groups, n_sub, has_state):
    if kind == "hgrn":
        q_ref, v_ref, og_ref, f_ref, lbl_ref, nw_ref = refs[:6]
        refs = refs[6:]
    else:
        q_ref, k_ref, v_ref, og_ref, s_ref, w2_ref, ba_ref, nw_ref = refs[:8]
        refs = refs[8:]
    if has_state:
        s0_ref, refs = refs[0], refs[1:]
    o_ref, sout_ref, st_scr, inter_scr, tmp_scr = refs
    t = pl.program_id(1)
    n_heads = n_groups * n_sub
    dv = LANE

    @pl.when(t == 0)
    def _():
        for gi in range(n_groups):
            if has_state:
                st_scr[gi] = s0_ref[gi].T
            else:
                st_scr[gi] = jnp.zeros((dv, LANE), F32)

    causal, tri = _chunk_consts(length)
    masks = _lane_masks(n_sub)

    if kind == "hgrn":
        logits = lbl_ref[...]
        e = jnp.exp(logits - jnp.max(logits, axis=0, keepdims=True))
        p = e / jnp.sum(e, axis=0, keepdims=True)
        lb = jnp.zeros((1, p.shape[1]), F32)
        for j in range(1, layer + 1):
            lb = lb + p[j:j + 1, :]

    def load_group(rows, gi):
        lanes = slice(gi * LANE, (gi + 1) * LANE)
        if kind == "hgrn":
            aq = q_ref[rows, lanes].astype(F32)
            q = aq * _sigmoid(aq)
            x = f_ref[rows, lanes]
            lbh = lb[:, lanes]
            g = jnp.log(lbh + (1.0 - lbh) * _sigmoid(x))
            k = (1.0 - lbh) * _sigmoid(-x)
        else:
            q = q_ref[rows, lanes].astype(F32) * (float(LANE // n_sub) ** -0.5)
            k = k_ref[rows, lanes].astype(F32)
            pre = _dot(s_ref[rows, :].astype(BF16), w2_ref[:, lanes]) + ba_ref[:, lanes]
            g = _log_sigmoid(pre) / GLA_GATE_TEMP
        vs = [v_ref[rows, (gi * n_sub + h) * dv:(gi * n_sub + h + 1) * dv].astype(F32)
              for h in range(n_sub)]
        return q, k, g, vs

    def finish(rows, head, o):
        lanes = slice(head * dv, (head + 1) * dv)
        og = og_ref[rows, lanes].astype(F32)
        if kind == "hgrn":
            y = _head_norm(o * _sigmoid(og), nw_ref[:, lanes])
        else:
            y = _head_norm(o, nw_ref[:, lanes]) * (og * _sigmoid(og))
        o_ref[rows, lanes] = y.astype(o_ref.dtype)

    def chunk(c, carry):
        rows = pl.ds(pl.multiple_of(c * length, length), length)
        span = None
        for gi in range(n_groups):
            q, k, g, vs = load_group(rows, gi)
            outs, st_new, sp = _glr_chunk(q, k, g, vs, masks, st_scr[gi], causal, tri)
            st_scr[gi] = st_new
            span = sp if span is None else jnp.maximum(span, sp)
            for h, (o_intra, o_inter) in enumerate(outs):
                inter_scr[gi * n_sub + h, 0:length, :] = o_inter
                finish(rows, gi * n_sub + h, o_intra + o_inter)

        @pl.when(span > SAFE_DECAY_SPAN)
        def _():
            for gi in range(n_groups):
                q, k, g, vs = load_group(rows, gi)
                intra = _glr_intra_exact(q, k, g, vs, masks, tmp_scr, tri)
                for h in range(n_sub):
                    head = gi * n_sub + h
                    finish(rows, head, intra[h] + inter_scr[head, 0:length, :])

        return carry

    lax.fori_loop(0, n_chunks, chunk, 0)

    @pl.when(t == pl.num_programs(1) - 1)
    def _():
        for gi in range(n_groups):
            sout_ref[gi] = st_scr[gi].T


def _time_block(t_len):
    length = min(CHUNK, t_len)
    tb = _pick_tile(t_len, 256)
    assert tb % length == 0
    return length, tb


def _glr(kind, layer, bsz, t_len, ins, s0):
    length, tb = _time_block(t_len)
    n_groups, n_sub = (4, 1) if kind == "hgrn" else (2, 2)
    n_heads = n_groups * n_sub
    bw = n_heads * LANE
    has_state = s0 is not None

    def tok(width, blk):
        return pl.BlockSpec((None, tb, width), lambda b, t: (b, t, blk))

    def full(shape):
        return pl.BlockSpec(shape, lambda b, t: (0,) * len(shape))

    if kind == "hgrn":
        a, f, lbl, nw = ins
        a = a.reshape(bsz, t_len, -1)
        f = f.reshape(bsz, t_len, -1)
        args = [a, a, a, f, lbl, nw]
        in_specs = [tok(bw, 0), tok(bw, 1), tok(bw, 2), tok(bw, 0), full(lbl.shape),
                    pl.BlockSpec((None, 1, bw), lambda b, t: (layer, 0, 0))]
    else:
        c, s, w2, ba, nw = ins
        c = c.reshape(bsz, t_len, -1)
        s = s.reshape(bsz, t_len, -1)
        kw = n_groups * LANE
        args = [c, c, c, c, s, w2, ba, nw]
        in_specs = [tok(kw, 0), tok(kw, 1), tok(bw, 1), tok(bw, 2), tok(LANE, 0),
                    pl.BlockSpec((None, LANE, kw), lambda b, t: (layer, 0, 0)),
                    pl.BlockSpec((None, 1, kw), lambda b, t: (layer, 0, 0)),
                    pl.BlockSpec((None, 1, bw), lambda b, t: (layer, 0, 0))]
    st_spec = pl.BlockSpec((None, n_groups, LANE, LANE), lambda b, t: (b, 0, 0, 0))
    if has_state:
        args.append(s0)
        in_specs.append(st_spec)
    o, s_new = pl.pallas_call(
        functools.partial(_glr_kernel, kind=kind, layer=layer, length=length,
                          n_chunks=tb // length, n_groups=n_groups, n_sub=n_sub,
                          has_state=has_state),
        grid=(bsz, t_len // tb),
        in_specs=in_specs,
        out_specs=[pl.BlockSpec((None, tb, bw), lambda b, t: (b, t, 0)), st_spec],
        out_shape=[jax.ShapeDtypeStruct((bsz, t_len, bw), BF16),
                   jax.ShapeDtypeStruct((bsz, n_groups, LANE, LANE), F32)],
        scratch_shapes=[pltpu.VMEM((n_groups, LANE, LANE), F32),
                        pltpu.VMEM((n_heads, length, LANE), F32),
                        pltpu.VMEM((2 + n_sub, length, LANE), F32)],
        compiler_params=_params(("parallel", "arbitrary")),
        name=kind,
    )(*args)
    return o.reshape(bsz * t_len, bw), s_new


def _mlstm_kernel(*refs, length, n_chunks, n_heads, has_state):
    q_ref, k_ref, v_ref, og_ref, s_ref, bias_ref, nw_ref = refs[:7]
    refs = refs[7:]
    if has_state:
        c0_ref, n0_ref, m0_ref = refs[:3]
        refs = refs[3:]
    o_ref, cout_ref, nout_ref, mout_ref, c_scr, n_scr, m_scr = refs
    t = pl.program_id(1)
    dk = LANE
    scale = float(dk) ** -0.5

    @pl.when(t == 0)
    def _():
        if has_state:
            c_scr[...] = c0_ref[...]
            n_scr[...] = n0_ref[...]
            m_scr[...] = m0_ref[...]
        else:
            c_scr[...] = jnp.zeros_like(c_scr)
            n_scr[...] = jnp.zeros_like(n_scr)
            m_scr[...] = jnp.zeros_like(m_scr)

    causal, tri = _chunk_consts(length)
    sel = jnp.where(lax.broadcasted_iota(jnp.int32, (8, LANE), 0)
                    == lax.broadcasted_iota(jnp.int32, (8, LANE), 1), 1.0, 0.0).astype(BF16)

    def chunk(c, carry):
        rows = pl.ds(pl.multiple_of(c * length, length), length)
        x = s_ref[rows, :] + bias_ref[...]
        bc_all = _cumsum_rows(tri, _log_sigmoid(x))
        bt_all = _rows_to_lanes(sel, bc_all)
        it_all = _rows_to_lanes(sel, x)
        m_all = m_scr[...]
        m_new_row = jnp.zeros((1, LANE), F32)
        lane = lax.broadcasted_iota(jnp.int32, (1, LANE), 1)
        for h in range(n_heads):
            lanes = slice(h * dk, (h + 1) * dk)
            q = q_ref[rows, lanes]
            kf = k_ref[rows, lanes].astype(F32) * scale
            kb = kf.astype(BF16)
            vb = v_ref[rows, lanes]
            b_c = bc_all[:, n_heads + h:n_heads + h + 1]
            i_c = x[:, h:h + 1]
            b_r = bt_all[n_heads + h:n_heads + h + 1, :]
            i_r = it_all[h:h + 1, :]
            b_last = b_c[length - 1:length, :]
            m_old = m_all[:, h:h + 1]
            c_old = c_scr[h]
            n_old = n_scr[h:h + 1, :]

            d_intra = jnp.where(causal, b_c - b_r + i_r, -jnp.inf)
            d_inter = b_c + m_old
            m_t = jnp.maximum(d_inter, jnp.max(d_intra, axis=1, keepdims=True))
            w_intra = jnp.exp(d_intra - m_t) * _dot_nt(q, kb)
            w_inter = jnp.exp(d_inter - m_t)
            num = _dot(w_intra.astype(BF16), vb) + w_inter * _dot(q, c_old.astype(BF16))
            qn = jnp.sum(q.astype(F32) * n_old, axis=1, keepdims=True)
            den = jnp.sum(w_intra, axis=1, keepdims=True) + w_inter * qn
            hid = num / jnp.maximum(jnp.abs(den), jnp.exp(-m_t))

            e_intra = b_last - b_c + i_c
            e_inter = b_last + m_old
            m_new = jnp.maximum(e_inter, jnp.max(e_intra, axis=0, keepdims=True))
            w_s = jnp.exp(e_intra - m_new)
            w_c = jnp.exp(e_inter - m_new)
            kw = w_s * kf
            c_scr[h] = w_c * c_old + _dot_tn(kw.astype(BF16), vb)
            n_scr[h:h + 1, :] = w_c * n_old + jnp.sum(kw, axis=0, keepdims=True)
            m_new_row = jnp.where(lane == h, m_new, m_new_row)

            og = og_ref[rows, lanes].astype(F32)
            y = _sigmoid(og) * _head_norm(hid, nw_ref[:, lanes])
            o_ref[rows, lanes] = y.astype(o_ref.dtype)
        m_scr[...] = m_new_row
        return carry

    lax.fori_loop(0, n_chunks, chunk, 0)

    @pl.when(t == pl.num_programs(1) - 1)
    def _():
        cout_ref[...] = c_scr[...]
        nout_ref[...] = n_scr[...]
        mout_ref[...] = m_scr[...]


def _mlstm(layer, bsz, t_len, p_b, p_s, bias_row, norm_w, states):
    length, tb = _time_block(t_len)
    n_heads = p_b.shape[1] // (4 * LANE)
    bw = n_heads * LANE
    has_state = states is not None
    p_b = p_b.reshape(bsz, t_len, -1)
    p_s = p_s.reshape(bsz, t_len, -1)

    def tok(width, blk):
        return pl.BlockSpec((None, tb, width), lambda b, t: (b, t, blk))

    c_spec = pl.BlockSpec((None, n_heads, LANE, LANE), lambda b, t: (b, 0, 0, 0))
    n_spec = pl.BlockSpec((None, n_heads, LANE), lambda b, t: (b, 0, 0))
    m_spec = pl.BlockSpec((None, 1, LANE), lambda b, t: (b, 0, 0))
    args = [p_b, p_b, p_b, p_b, p_s, bias_row, norm_w]
    in_specs = [tok(bw, 0), tok(bw, 1), tok(bw, 2), tok(bw, 3), tok(LANE, 0),
                pl.BlockSpec((None, 1, LANE), lambda b, t: (layer, 0, 0)),
                pl.BlockSpec((None, 1, bw), lambda b, t: (layer, 0, 0))]
    if has_state:
        args += list(states)
        in_specs += [c_spec, n_spec, m_spec]
    o, c_new, n_new, m_new = pl.pallas_call(
        functools.partial(_mlstm_kernel, length=length, n_chunks=tb // length,
                          n_heads=n_heads, has_state=has_state),
        grid=(bsz, t_len // tb),
        in_specs=in_specs,
        out_specs=[pl.BlockSpec((None, tb, bw), lambda b, t: (b, t, 0)), c_spec, n_spec, m_spec],
        out_shape=[jax.ShapeDtypeStruct((bsz, t_len, bw), BF16),
                   jax.ShapeDtypeStruct((bsz, n_heads, LANE, LANE), F32),
                   jax.ShapeDtypeStruct((bsz, n_heads, LANE), F32),
                   jax.ShapeDtypeStruct((bsz, 1, LANE), F32)],
        scratch_shapes=[pltpu.VMEM((n_heads, LANE, LANE), F32),
                        pltpu.VMEM((n_heads, LANE), F32),
                        pltpu.VMEM((1, LANE), F32)],
        compiler_params=_params(("parallel", "arbitrary")),
        name="mlstm",
    )(*args)
    return o.reshape(bsz * t_len, bw), c_new, n_new, m_new


def _prep_weights(w_in, gla_w_a2, mlstm_gate_bias, heads):
    (ah, adk, adv), (bh, bdk, bdv), (ch, cdk, cdv), rank = heads
    sizes = (ah * adk, ah * adk, ah * adv, ah * adv,
             bh * bdk, bh * bdk, bh * bdv, bh * bdv, 2 * bh,
             ch * cdk, ch * cdk, ch * cdv, ch * cdv, rank)
    offs = [0]
    for s in sizes:
        offs.append(offs[-1] + s)
    col = lambda i: w_in[:, :, offs[i]:offs[i + 1]]
    (a_q, a_f, a_i, a_g, b_q, b_k, b_v, b_o, b_if, c_q, c_k, c_v, c_g, c_lr) = [col(i) for i in range(14)]
    pad = jnp.zeros(w_in.shape[:2] + (LANE - 2 * bh - rank,), w_in.dtype)
    groups = ([a_q, a_i, a_g], [a_f], [b_q, b_k, b_v, b_o], [c_q, c_k, c_v, c_g], [b_if, c_lr, pad])
    widths = tuple(sum(g.shape[-1] for g in grp) for grp in groups)
    w_cat = jnp.concatenate([g for grp in groups for g in grp], axis=-1).astype(BF16)
    w_mg = w_in[:, :, offs[14]:].astype(BF16)
    depth = w_in.shape[0]
    w2_pad = jnp.zeros((depth, LANE, gla_w_a2.shape[-1]), F32)
    w2_pad = w2_pad.at[:, 2 * bh:2 * bh + rank, :].set(gla_w_a2).astype(BF16)
    bias_row = jnp.zeros((depth, 1, LANE), F32).at[:, 0, :2 * bh].set(mlstm_gate_bias)
    return w_cat, widths, w_mg, w2_pad, bias_row


def _run_trunk(x, states, wts, heads):
    (ffn1_norm, ffn1_up, ffn1_down, mix_norm, w_cat, widths, w_mg, hgrn_lb_logits, hgrn_norm,
     bias_row, mlstm_norm, w2_pad, gla_b_a, gla_norm, w_branch, w_out,
     ffn2_norm, ffn2_up, ffn2_down, final_norm) = wts
    (ah, adk, adv), (bh, bdk, bdv), (ch, cdk, cdv), _ = heads
    bsz, t_len, d = x.shape
    depth = ffn1_norm.shape[0]
    x = x.reshape(bsz * t_len, d)
    new_states = ([], [], [], [], [])
    for l in range(depth):
        x = _ffn(x, ffn1_norm, ffn1_up, ffn1_down, l)
        p_a, p_f, p_b, p_c, p_s = _inproj(x, mix_norm, w_cat, l, widths)
        if states is None:
            st_a = st_b = st_c = None
        else:
            s_hgrn, c_ml, n_ml, m_ml, s_gla = states
            st_a = s_hgrn[l]
            m_pad = jnp.zeros((bsz, 1, LANE), F32).at[:, 0, :bh].set(m_ml[l])
            st_b = (c_ml[l], n_ml[l], m_pad)
            st_c = s_gla[l].reshape(bsz, ch // 2, 2 * cdk, cdv)
        o_a, s_a = _glr("hgrn", l, bsz, t_len, (p_a, p_f, hgrn_lb_logits, hgrn_norm), st_a)
        o_b, c_new, n_new, m_new = _mlstm(l, bsz, t_len, p_b, p_s, bias_row, mlstm_norm, st_b)
        o_c, s_c = _glr("gla", l, bsz, t_len, (p_c, p_s, w2_pad, gla_b_a, gla_norm), st_c)
        x = _merge(x, mix_norm, o_a, o_b, o_c, w_mg, w_branch, w_out, l)
        x = _ffn(x, ffn2_norm, ffn2_up, ffn2_down, l,
                 final_w=final_norm if l == depth - 1 else None)
        for acc, s in zip(new_states, (s_a, c_new, n_new, m_new[:, 0, :bh],
                                       s_c.reshape(bsz, ch, cdk, cdv))):
            acc.append(s)
    return x.reshape(bsz, t_len, d), tuple(jnp.stack(acc) for acc in new_states)


def kernel(x_prompt, x_sample, state_hgrn, state_mlstm_c, state_mlstm_n, state_mlstm_m, state_gla, ffn1_norm, ffn1_w_up, ffn1_w_down, mix_norm, w_in, hgrn_lb_logits, hgrn_norm, mlstm_gate_bias, mlstm_norm, gla_w_a2, gla_b_a, gla_norm, w_branch, w_out, ffn2_norm, ffn2_w_up, ffn2_w_down, final_norm):
    heads = (state_hgrn.shape[2:], state_mlstm_c.shape[2:], state_gla.shape[2:], gla_w_a2.shape[1])
    assert heads[0][1:] == (LANE, LANE) and heads[1][1:] == (LANE, LANE)
    assert heads[2][1:] == (LANE // 2, LANE) and heads[2][0] % 2 == 0
    w_cat, widths, w_mg, w2_pad, bias_row = _prep_weights(w_in, gla_w_a2, mlstm_gate_bias, heads)
    row = lambda a: a[:, None, :]
    wts = (row(ffn1_norm), ffn1_w_up.astype(BF16), ffn1_w_down.astype(BF16), row(mix_norm),
           w_cat, widths, w_mg, hgrn_lb_logits, row(hgrn_norm), bias_row, row(mlstm_norm),
           w2_pad, row(gla_b_a), row(gla_norm), w_branch.astype(BF16), w_out.astype(BF16),
           row(ffn2_norm), ffn2_w_up.astype(BF16), ffn2_w_down.astype(BF16), final_norm[None, :])
    y_p, st_p = _run_trunk(x_prompt, None, wts, heads)
    sample_states = (state_hgrn, state_mlstm_c, state_mlstm_n, state_mlstm_m, state_gla)
    y_s, st_s = _run_trunk(x_sample, sample_states, wts, heads)
    return (y_p, y_s) + st_p + st_s
```

```python
import functools

import jax
import jax.numpy as jnp
from jax import lax
from jax.experimental import pallas as pl
from jax.experimental.pallas import tpu as pltpu

F32 = jnp.float32
BF16 = jnp.bfloat16

EPS = 1e-6
CHUNK = 64
GLA_GATE_TEMP = 16.0
N_BRANCH = 3
LANE = 128
V7X_VMEM_BYTES = 64 * 1024 * 1024
VMEM_LIMIT = V7X_VMEM_BYTES - 8 * 1024 * 1024
SAFE_DECAY_SPAN = 60.0

_NT = (((1,), (1,)), ((), ()))
_TN = (((0,), (0,)), ((), ()))


def _dot(a, b):
    return jnp.dot(a, b, preferred_element_type=F32)


def _dot_nt(a, b):
    return lax.dot_general(a, b, _NT, preferred_element_type=F32)


def _dot_tn(a, b):
    return lax.dot_general(a, b, _TN, preferred_element_type=F32)


def _rms(x, w):
    return x * lax.rsqrt(jnp.mean(x * x, axis=-1, keepdims=True) + EPS) * w


def _sigmoid(x):
    return 1.0 / (1.0 + jnp.exp(-x))


def _log_sigmoid(x):
    return jnp.minimum(x, 0.0) - jnp.log(1.0 + jnp.exp(-jnp.abs(x)))


def _split3(x):
    hi = x.astype(BF16)
    r = x - hi.astype(F32)
    mid = r.astype(BF16)
    lo = (r - mid.astype(F32)).astype(BF16)
    return hi, mid, lo


def _cumsum_rows(tri, g):
    hi, mid, lo = _split3(g)
    return (_dot(tri, hi) + _dot(tri, mid)) + _dot(tri, lo)


def _rows_to_lanes(sel, x):
    hi, mid, lo = _split3(x)
    return (_dot_nt(sel, hi) + _dot_nt(sel, mid)) + _dot_nt(sel, lo)


def _pick_tile(n, cap):
    t = min(n, cap)
    while n % t or (t % 8 and t != n):
        t -= 1
    return t


def _params(sem):
    return pltpu.CompilerParams(dimension_semantics=sem, vmem_limit_bytes=VMEM_LIMIT)


def _ffn_kernel(x_ref, nw_ref, wg_ref, wu_ref, wd_ref, *rest, n_ff, final):
    if final:
        fw_ref, o_ref, h_scr, acc_scr = rest
    else:
        o_ref, h_scr, acc_scr = rest
    j = pl.program_id(1)

    @pl.when(j == 0)
    def _():
        h_scr[...] = _rms(x_ref[...], nw_ref[...]).astype(BF16)
        acc_scr[...] = jnp.zeros_like(acc_scr)

    h = h_scr[...]
    g = _dot(h, wg_ref[...])
    u = _dot(h, wu_ref[...])
    act = (g * _sigmoid(g) * u).astype(BF16)
    acc_scr[...] += _dot(act, wd_ref[...])

    @pl.when(j == n_ff - 1)
    def _():
        y = x_ref[...] + 0.5 * acc_scr[...]
        if final:
            y = _rms(y, fw_ref[...])
        o_ref[...] = y


def _ffn(x, norm_w, w_up, w_down, layer, final_w=None):
    n, d = x.shape
    d_ff = w_down.shape[1]
    tm = _pick_tile(n, 1024)
    tf = 256
    n_ff = d_ff // tf
    assert d_ff % tf == 0
    final = final_w is not None
    in_specs = [
        pl.BlockSpec((tm, d), lambda i, j: (i, 0)),
        pl.BlockSpec((None, 1, d), lambda i, j: (layer, 0, 0)),
        pl.BlockSpec((None, d, tf), lambda i, j: (layer, 0, j)),
        pl.BlockSpec((None, d, tf), lambda i, j: (layer, 0, j + n_ff)),
        pl.BlockSpec((None, tf, d), lambda i, j: (layer, j, 0)),
    ]
    args = [x, norm_w, w_up, w_up, w_down]
    if final:
        in_specs.append(pl.BlockSpec((1, d), lambda i, j: (0, 0)))
        args.append(final_w)
    return pl.pallas_call(
        functools.partial(_ffn_kernel, n_ff=n_ff, final=final),
        grid=(n // tm, n_ff),
        in_specs=in_specs,
        out_specs=pl.BlockSpec((tm, d), lambda i, j: (i, 0)),
        out_shape=jax.ShapeDtypeStruct((n, d), F32),
        scratch_shapes=[pltpu.VMEM((tm, d), BF16), pltpu.VMEM((tm, d), F32)],
        compiler_params=_params(("parallel", "arbitrary")),
        name="ffn",
    )(*args)


def _inproj_kernel(x_ref, nw_ref, w_ref, oa_ref, of_ref, ob_ref, oc_ref, os_ref, *, widths):
    h = _rms(x_ref[...], nw_ref[...]).astype(BF16)
    c0 = 0
    for ref, width in zip((oa_ref, of_ref, ob_ref, oc_ref, os_ref), widths):
        step = min(width, 512)
        for c in range(0, width, step):
            ref[:, c:c + step] = _dot(h, w_ref[:, c0 + c:c0 + c + step]).astype(ref.dtype)
        c0 += width


def _inproj(x, norm_w, w_cat, layer, widths):
    n, d = x.shape
    tm = _pick_tile(n, 512)
    dtypes = (BF16, F32, BF16, BF16, F32)
    return pl.pallas_call(
        functools.partial(_inproj_kernel, widths=widths),
        grid=(n // tm,),
        in_specs=[
            pl.BlockSpec((tm, d), lambda i: (i, 0)),
            pl.BlockSpec((None, 1, d), lambda i: (layer, 0, 0)),
            pl.BlockSpec((None, d, sum(widths)), lambda i: (layer, 0, 0)),
        ],
        out_specs=[pl.BlockSpec((tm, w), lambda i: (i, 0)) for w in widths],
        out_shape=[jax.ShapeDtypeStruct((n, w), dt) for w, dt in zip(widths, dtypes)],
        compiler_params=_params(("parallel",)),
        name="inproj",
    )(x, norm_w, w_cat)


def _merge_kernel(x_ref, nw_ref, oa_ref, ob_ref, oc_ref, wm_ref, wb_ref, wo_ref, o_ref):
    x = x_ref[...]
    d = x.shape[1]
    h = _rms(x, nw_ref[...]).astype(BF16)
    merged = None
    for nb, br_ref in enumerate((oa_ref, ob_ref, oc_ref)):
        gate = _sigmoid(_dot(h, wm_ref[:, nb * d:(nb + 1) * d]))
        term = gate * _dot(br_ref[...], wb_ref[nb])
        merged = term if merged is None else merged + term
    o_ref[...] = x + _dot(merged.astype(BF16), wo_ref[...])


def _merge(x, norm_w, o_a, o_b, o_c, w_mg, w_branch, w_out, layer):
    n, d = x.shape
    bw = o_a.shape[1]
    tm = _pick_tile(n, 512)
    br_spec = pl.BlockSpec((tm, bw), lambda i: (i, 0))
    return pl.pallas_call(
        _merge_kernel,
        grid=(n // tm,),
        in_specs=[
            pl.BlockSpec((tm, d), lambda i: (i, 0)),
            pl.BlockSpec((None, 1, d), lambda i: (layer, 0, 0)),
            br_spec, br_spec, br_spec,
            pl.BlockSpec((None, d, N_BRANCH * d), lambda i: (layer, 0, 0)),
            pl.BlockSpec((None, N_BRANCH, bw, d), lambda i: (layer, 0, 0, 0)),
            pl.BlockSpec((None, d, d), lambda i: (layer, 0, 0)),
        ],
        out_specs=pl.BlockSpec((tm, d), lambda i: (i, 0)),
        out_shape=jax.ShapeDtypeStruct((n, d), F32),
        compiler_params=_params(("parallel",)),
        name="merge",
    )(x, norm_w, o_a, o_b, o_c, w_mg, w_branch, w_out)


def _chunk_consts(length, n_chunks=1):
    r = lax.broadcasted_iota(jnp.int32, (length, length), 0)
    c = lax.broadcasted_iota(jnp.int32, (length, length), 1)
    causal = r >= c
    tb = length * n_chunks
    rb = lax.broadcasted_iota(jnp.int32, (tb, tb), 0)
    cb = lax.broadcasted_iota(jnp.int32, (tb, tb), 1)
    same = None
    for i in range(n_chunks):
        blk = (rb >= i * length) & (rb < (i + 1) * length) & (cb >= i * length)
        same = blk if same is None else same | blk
    tri = jnp.where(same & (rb >= cb), 1.0, 0.0).astype(BF16)
    return causal, tri


def _head_norm(o, w):
    return o * lax.rsqrt(jnp.mean(o * o, axis=-1, keepdims=True) + EPS) * w


def _glr_intra_exact(q, k, g, vs, masks, tmp_ref, tri):
    length = q.shape[0]
    b = _cumsum_rows(tri, g)
    tmp_ref[0, 0:length, :] = k
    tmp_ref[1, 0:length, :] = b
    for h, v in enumerate(vs):
        tmp_ref[2 + h, 0:length, :] = v
    t_idx = lax.broadcasted_iota(jnp.int32, (length, 1), 0)

    def body(s, accs):
        k_s = tmp_ref[0, pl.ds(s, 1), :]
        b_s = tmp_ref[1, pl.ds(s, 1), :]
        a = q * k_s * jnp.exp(jnp.minimum(b - b_s, 0.0))
        new = []
        for h, m in enumerate(masks):
            ah = a if m is None else a * m
            col = jnp.where(t_idx >= s, jnp.sum(ah, axis=1, keepdims=True), 0.0)
            new.append(accs[h] + col * tmp_ref[2 + h, pl.ds(s, 1), :])
        return tuple(new)

    init = tuple(jnp.zeros((length, v.shape[1]), F32) for v in vs)
    return lax.fori_loop(0, length, body, init)


def _lane_masks(n_sub):
    if n_sub == 1:
        return [None]
    lane = lax.broadcasted_iota(jnp.int32, (1, LANE), 1)
    w = LANE // n_sub
    return [jnp.where((lane >= h * w) & (lane < (h + 1) * w), 1.0, 0.0) for h in range(n_sub)]


def _glr_kernel(*refs, kind, layer, length, n_chunks, n_groups, n_sub, has_state):
    if kind == "hgrn":
        q_ref, v_ref, og_ref, f_ref, lbl_ref, nw_ref = refs[:6]
        refs = refs[6:]
    else:
        q_ref, k_ref, v_ref, og_ref, s_ref, w2_ref, ba_ref, nw_ref = refs[:8]
        refs = refs[8:]
    if has_state:
        s0_ref, refs = refs[0], refs[1:]
    (o_ref, sout_ref, st_scr, qt_scr, qin_scr, kt_scr, kst_scr, dec_scr, o_scr, inter_scr,
     tmp_scr) = refs
    t = pl.program_id(1)
    n_heads = n_groups * n_sub
    dv = LANE
    tb = length * n_chunks

    @pl.when(t == 0)
    def _():
        for gi in range(n_groups):
            if has_state:
                st_scr[gi] = s0_ref[gi].T
            else:
                st_scr[gi] = jnp.zeros((dv, LANE), F32)

    causal, tri_blk = _chunk_consts(length, n_chunks)
    masks = _lane_masks(n_sub)

    if kind == "hgrn":
        logits = lbl_ref[...]
        e = jnp.exp(logits - jnp.max(logits, axis=0, keepdims=True))
        p = e / jnp.sum(e, axis=0, keepdims=True)
        lb = jnp.zeros((1, p.shape[1]), F32)
        for j in range(1, layer + 1):
            lb = lb + p[j:j + 1, :]

    def load_group(rows, gi, with_v=True):
        lanes = slice(gi * LANE, (gi + 1) * LANE)
        if kind == "hgrn":
            aq = q_ref[rows, lanes].astype(F32)
            q = aq * _sigmoid(aq)
            x = f_ref[rows, lanes]
            lbh = lb[:, lanes]
            g = jnp.log(lbh + (1.0 - lbh) * _sigmoid(x))
            k = (1.0 - lbh) * _sigmoid(-x)
        else:
            q = q_ref[rows, lanes].astype(F32) * (float(LANE // n_sub) ** -0.5)
            k = k_ref[rows, lanes].astype(F32)
            pre = _dot(s_ref[rows, :].astype(BF16), w2_ref[:, lanes]) + ba_ref[:, lanes]
            g = _log_sigmoid(pre) / GLA_GATE_TEMP
        vs = [v_ref[rows, (gi * n_sub + h) * dv:(gi * n_sub + h + 1) * dv].astype(F32)
              for h in range(n_sub)] if with_v else None
        return q, k, g, vs

    def finish(rows, head, o):
        lanes = slice(head * dv, (head + 1) * dv)
        og = og_ref[rows, lanes].astype(F32)
        if kind == "hgrn":
            y = _head_norm(o * _sigmoid(og), nw_ref[:, lanes])
        else:
            y = _head_norm(o, nw_ref[:, lanes]) * (og * _sigmoid(og))
        o_ref[rows, lanes] = y.astype(o_ref.dtype)

    span_vec = jnp.zeros((1, LANE), F32)
    for gi in range(n_groups):
        glanes = slice(gi * LANE, (gi + 1) * LANE)
        q, k, g, _ = load_group(slice(0, tb), gi, with_v=False)
        b = _cumsum_rows(tri_blk, g)
        for c in range(n_chunks):
            sl = slice(c * length, (c + 1) * length)
            bc, qc, kc = b[sl], q[sl], k[sl]
            b_mid = bc[length // 2:length // 2 + 1, :]
            b_last = bc[length - 1:length, :]
            span_vec = jnp.maximum(span_vec, jnp.maximum(bc[0:1, :] - b_mid, b_mid - b_last))
            qt = qc * jnp.exp(bc - b_mid)
            q_in = qc * jnp.exp(bc)
            kt_scr[sl, glanes] = (kc * jnp.exp(b_mid - bc)).astype(BF16)
            kst_scr[sl, glanes] = (kc * jnp.exp(b_last - bc)).astype(BF16)
            dec_scr[c:c + 1, glanes] = jnp.exp(b_last)
            for h, m in enumerate(masks):
                hl = slice((gi * n_sub + h) * LANE, (gi * n_sub + h + 1) * LANE)
                qt_scr[sl, hl] = (qt if m is None else qt * m).astype(BF16)
                qin_scr[sl, hl] = (q_in if m is None else q_in * m).astype(BF16)

    for gi in range(n_groups):
        glanes = slice(gi * LANE, (gi + 1) * LANE)
        st = st_scr[gi]
        for c in range(n_chunks):
            sl = slice(c * length, (c + 1) * length)
            kt = kt_scr[sl, glanes]
            kst = kst_scr[sl, glanes]
            st_b = st.astype(BF16)
            upd = None
            for h, m in enumerate(masks):
                hl = slice((gi * n_sub + h) * LANE, (gi * n_sub + h + 1) * LANE)
                vb = v_ref[sl, hl]
                sc = jnp.where(causal, _dot_nt(qt_scr[sl, hl], kt), 0.0).astype(BF16)
                o_inter = _dot_nt(qin_scr[sl, hl], st_b)
                inter_scr[sl, hl] = o_inter
                o_scr[sl, hl] = _dot(sc, vb) + o_inter
                u = _dot_tn(vb, kst)
                if m is not None:
                    u = u * m
                upd = u if upd is None else upd + u
            st = st * dec_scr[c:c + 1, glanes] + upd
        st_scr[gi] = st

    for head in range(n_heads):
        finish(slice(0, tb), head, o_scr[:, head * dv:(head + 1) * dv])

    @pl.when(jnp.max(span_vec) > SAFE_DECAY_SPAN)
    def _():
        tri = jnp.where(causal, 1.0, 0.0).astype(BF16)

        def chunk(c, carry):
            rows = pl.ds(pl.multiple_of(c * length, length), length)
            for gi in range(n_groups):
                q, k, g, vs = load_group(rows, gi)
                intra = _glr_intra_exact(q, k, g, vs, masks, tmp_scr, tri)
                for h in range(n_sub):
                    head = gi * n_sub + h
                    finish(rows, head, intra[h] + inter_scr[rows, head * dv:(head + 1) * dv])
            return carry

        lax.fori_loop(0, n_chunks, chunk, 0)

    @pl.when(t == pl.num_programs(1) - 1)
    def _():
        for gi in range(n_groups):
            sout_ref[gi] = st_scr[gi].T


def _time_block(t_len):
    length = min(CHUNK, t_len)
    tb = _pick_tile(t_len, 256)
    assert tb % length == 0
    return length, tb


def _glr(kind, layer, bsz, t_len, ins, s0):
    length, tb = _time_block(t_len)
    n_groups, n_sub = (4, 1) if kind == "hgrn" else (2, 2)
    n_heads = n_groups * n_sub
    bw = n_heads * LANE
    has_state = s0 is not None

    def tok(width, blk):
        return pl.BlockSpec((None, tb, width), lambda b, t: (b, t, blk))

    def full(shape):
        return pl.BlockSpec(shape, lambda b, t: (0,) * len(shape))

    if kind == "hgrn":
        a, f, lbl, nw = ins
        a = a.reshape(bsz, t_len, -1)
        f = f.reshape(bsz, t_len, -1)
        args = [a, a, a, f, lbl, nw]
        in_specs = [tok(bw, 0), tok(bw, 1), tok(bw, 2), tok(bw, 0), full(lbl.shape),
                    pl.BlockSpec((None, 1, bw), lambda b, t: (layer, 0, 0))]
    else:
        c, s, w2, ba, nw = ins
        c = c.reshape(bsz, t_len, -1)
        s = s.reshape(bsz, t_len, -1)
        kw = n_groups * LANE
        args = [c, c, c, c, s, w2, ba, nw]
        in_specs = [tok(kw, 0), tok(kw, 1), tok(bw, 1), tok(bw, 2), tok(LANE, 0),
                    pl.BlockSpec((None, LANE, kw), lambda b, t: (layer, 0, 0)),
                    pl.BlockSpec((None, 1, kw), lambda b, t: (layer, 0, 0)),
                    pl.BlockSpec((None, 1, bw), lambda b, t: (layer, 0, 0))]
    st_spec = pl.BlockSpec((None, n_groups, LANE, LANE), lambda b, t: (b, 0, 0, 0))
    if has_state:
        args.append(s0)
        in_specs.append(st_spec)
    o, s_new = pl.pallas_call(
        functools.partial(_glr_kernel, kind=kind, layer=layer, length=length,
                          n_chunks=tb // length, n_groups=n_groups, n_sub=n_sub,
                          has_state=has_state),
        grid=(bsz, t_len // tb),
        in_specs=in_specs,
        out_specs=[pl.BlockSpec((None, tb, bw), lambda b, t: (b, t, 0)), st_spec],
        out_shape=[jax.ShapeDtypeStruct((bsz, t_len, bw), BF16),
                   jax.ShapeDtypeStruct((bsz, n_groups, LANE, LANE), F32)],
        scratch_shapes=[pltpu.VMEM((n_groups, LANE, LANE), F32),
                        pltpu.VMEM((tb, bw), BF16),
                        pltpu.VMEM((tb, bw), BF16),
                        pltpu.VMEM((tb, n_groups * LANE), BF16),
                        pltpu.VMEM((tb, n_groups * LANE), BF16),
                        pltpu.VMEM((max(tb // length, 8), n_groups * LANE), F32),
                        pltpu.VMEM((tb, bw), F32),
                        pltpu.VMEM((tb, bw), F32),
                        pltpu.VMEM((2 + n_sub, length, LANE), F32)],
        compiler_params=_params(("parallel", "arbitrary")),
        name=kind,
    )(*args)
    return o.reshape(bsz * t_len, bw), s_new


def _mlstm_kernel(*refs, length, n_chunks, n_heads, has_state):
    q_ref, k_ref, v_ref, og_ref, s_ref, bias_ref, nw_ref = refs[:7]
    refs = refs[7:]
    if has_state:
        c0_ref, n0_ref, m0_ref = refs[:3]
        refs = refs[3:]
    o_ref, cout_ref, nout_ref, mout_ref, c_scr, n_scr, m_scr, o_scr = refs
    t = pl.program_id(1)
    dk = LANE
    scale = float(dk) ** -0.5

    @pl.when(t == 0)
    def _():
        if has_state:
            c_scr[...] = c0_ref[...]
            n_scr[...] = n0_ref[...]
            m_scr[...] = m0_ref[...]
        else:
            c_scr[...] = jnp.zeros_like(c_scr)
            n_scr[...] = jnp.zeros_like(n_scr)
            m_scr[...] = jnp.zeros_like(m_scr)

    causal, tri_blk = _chunk_consts(length, n_chunks)
    sel = jnp.where(lax.broadcasted_iota(jnp.int32, (8, LANE), 0)
                    == lax.broadcasted_iota(jnp.int32, (8, LANE), 1), 1.0, 0.0).astype(BF16)

    x = s_ref[...] + bias_ref[...]
    bc_all = _cumsum_rows(tri_blk, _log_sigmoid(x))
    m_all = m_scr[...]
    c_st = [c_scr[h] for h in range(n_heads)]
    n_st = [n_scr[h:h + 1, :] for h in range(n_heads)]
    m_st = [m_all[:, h:h + 1] for h in range(n_heads)]

    for c in range(n_chunks):
        sl = slice(c * length, (c + 1) * length)
        bt_all = _rows_to_lanes(sel, bc_all[sl])
        it_all = _rows_to_lanes(sel, x[sl])
        for h in range(n_heads):
            lanes = slice(h * dk, (h + 1) * dk)
            q = q_ref[sl, lanes]
            kf = k_ref[sl, lanes].astype(F32) * scale
            vb = v_ref[sl, lanes]
            b_c = bc_all[sl, n_heads + h:n_heads + h + 1]
            i_c = x[sl, h:h + 1]
            b_r = bt_all[n_heads + h:n_heads + h + 1, :]
            i_r = it_all[h:h + 1, :]
            b_last = b_c[length - 1:length, :]

            d_intra = jnp.where(causal, b_c - b_r + i_r, -jnp.inf)
            a_t = jnp.max(d_intra, axis=1, keepdims=True)
            p = jnp.exp(d_intra - a_t) * _dot_nt(q, kf.astype(BF16))
            num_pre = _dot(p.astype(BF16), vb)
            sum_pre = jnp.sum(p, axis=1, keepdims=True)
            e_intra = b_last - b_c + i_c
            e_max = jnp.max(e_intra, axis=0, keepdims=True)
            kw = jnp.exp(e_intra - e_max) * kf
            u_pre = _dot_tn(kw.astype(BF16), vb)
            nk_pre = jnp.sum(kw, axis=0, keepdims=True)

            m_old, c_old, n_old = m_st[h], c_st[h], n_st[h]
            d_inter = b_c + m_old
            m_t = jnp.maximum(d_inter, a_t)
            f_intra = jnp.exp(a_t - m_t)
            w_inter = jnp.exp(d_inter - m_t)
            num = f_intra * num_pre + w_inter * _dot(q, c_old.astype(BF16))
            qn = jnp.sum(q.astype(F32) * n_old, axis=1, keepdims=True)
            den = f_intra * sum_pre + w_inter * qn
            o_scr[sl, lanes] = num / jnp.maximum(jnp.abs(den), jnp.exp(-m_t))

            e_inter = b_last + m_old
            m_new = jnp.maximum(e_inter, e_max)
            w_c = jnp.exp(e_inter - m_new)
            f_s = jnp.exp(e_max - m_new)
            c_st[h] = w_c * c_old + f_s * u_pre
            n_st[h] = w_c * n_old + f_s * nk_pre
            m_st[h] = m_new

    lane = lax.broadcasted_iota(jnp.int32, (1, LANE), 1)
    m_row = jnp.zeros((1, LANE), F32)
    for h in range(n_heads):
        lanes = slice(h * dk, (h + 1) * dk)
        c_scr[h] = c_st[h]
        n_scr[h:h + 1, :] = n_st[h]
        m_row = jnp.where(lane == h, m_st[h], m_row)
        og = og_ref[:, lanes].astype(F32)
        y = _sigmoid(og) * _head_norm(o_scr[:, lanes], nw_ref[:, lanes])
        o_ref[:, lanes] = y.astype(o_ref.dtype)
    m_scr[...] = m_row

    @pl.when(t == pl.num_programs(1) - 1)
    def _():
        cout_ref[...] = c_scr[...]
        nout_ref[...] = n_scr[...]
        mout_ref[...] = m_scr[...]


def _mlstm(layer, bsz, t_len, p_b, p_s, bias_row, norm_w, states):
    length, tb = _time_block(t_len)
    n_heads = p_b.shape[1] // (4 * LANE)
    bw = n_heads * LANE
    has_state = states is not None
    p_b = p_b.reshape(bsz, t_len, -1)
    p_s = p_s.reshape(bsz, t_len, -1)

    def tok(width, blk):
        return pl.BlockSpec((None, tb, width), lambda b, t: (b, t, blk))

    c_spec = pl.BlockSpec((None, n_heads, LANE, LANE), lambda b, t: (b, 0, 0, 0))
    n_spec = pl.BlockSpec((None, n_heads, LANE), lambda b, t: (b, 0, 0))
    m_spec = pl.BlockSpec((None, 1, LANE), lambda b, t: (b, 0, 0))
    args = [p_b, p_b, p_b, p_b, p_s, bias_row, norm_w]
    in_specs = [tok(bw, 0), tok(bw, 1), tok(bw, 2), tok(bw, 3), tok(LANE, 0),
                pl.BlockSpec((None, 1, LANE), lambda b, t: (layer, 0, 0)),
                pl.BlockSpec((None, 1, bw), lambda b, t: (layer, 0, 0))]
    if has_state:
        args += list(states)
        in_specs += [c_spec, n_spec, m_spec]
    o, c_new, n_new, m_new = pl.pallas_call(
        functools.partial(_mlstm_kernel, length=length, n_chunks=tb // length,
                          n_heads=n_heads, has_state=has_state),
        grid=(bsz, t_len // tb),
        in_specs=in_specs,
        out_specs=[pl.BlockSpec((None, tb, bw), lambda b, t: (b, t, 0)), c_spec, n_spec, m_spec],
        out_shape=[jax.ShapeDtypeStruct((bsz, t_len, bw), BF16),
                   jax.ShapeDtypeStruct((bsz, n_heads, LANE, LANE), F32),
                   jax.ShapeDtypeStruct((bsz, n_heads, LANE), F32),
                   jax.ShapeDtypeStruct((bsz, 1, LANE), F32)],
        scratch_shapes=[pltpu.VMEM((n_heads, LANE, LANE), F32),
                        pltpu.VMEM((n_heads, LANE), F32),
                        pltpu.VMEM((1, LANE), F32),
                        pltpu.VMEM((tb, bw), F32)],
        compiler_params=_params(("parallel", "arbitrary")),
        name="mlstm",
    )(*args)
    return o.reshape(bsz * t_len, bw), c_new, n_new, m_new


def _prep_weights(w_in, gla_w_a2, mlstm_gate_bias, heads):
    (ah, adk, adv), (bh, bdk, bdv), (ch, cdk, cdv), rank = heads
    sizes = (ah * adk, ah * adk, ah * adv, ah * adv,
             bh * bdk, bh * bdk, bh * bdv, bh * bdv, 2 * bh,
             ch * cdk, ch * cdk, ch * cdv, ch * cdv, rank)
    offs = [0]
    for s in sizes:
        offs.append(offs[-1] + s)
    col = lambda i: w_in[:, :, offs[i]:offs[i + 1]]
    (a_q, a_f, a_i, a_g, b_q, b_k, b_v, b_o, b_if, c_q, c_k, c_v, c_g, c_lr) = [col(i) for i in range(14)]
    pad = jnp.zeros(w_in.shape[:2] + (LANE - 2 * bh - rank,), w_in.dtype)
    groups = ([a_q, a_i, a_g], [a_f], [b_q, b_k, b_v, b_o], [c_q, c_k, c_v, c_g], [b_if, c_lr, pad])
    widths = tuple(sum(g.shape[-1] for g in grp) for grp in groups)
    w_cat = jnp.concatenate([g for grp in groups for g in grp], axis=-1).astype(BF16)
    w_mg = w_in[:, :, offs[14]:].astype(BF16)
    depth = w_in.shape[0]
    w2_pad = jnp.zeros((depth, LANE, gla_w_a2.shape[-1]), F32)
    w2_pad = w2_pad.at[:, 2 * bh:2 * bh + rank, :].set(gla_w_a2).astype(BF16)
    bias_row = jnp.zeros((depth, 1, LANE), F32).at[:, 0, :2 * bh].set(mlstm_gate_bias)
    return w_cat, widths, w_mg, w2_pad, bias_row


def _run_trunk(x, states, wts, heads):
    (ffn1_norm, ffn1_up, ffn1_down, mix_norm, w_cat, widths, w_mg, hgrn_lb_logits, hgrn_norm,
     bias_row, mlstm_norm, w2_pad, gla_b_a, gla_norm, w_branch, w_out,
     ffn2_norm, ffn2_up, ffn2_down, final_norm) = wts
    (ah, adk, adv), (bh, bdk, bdv), (ch, cdk, cdv), _ = heads
    bsz, t_len, d = x.shape
    depth = ffn1_norm.shape[0]
    x = x.reshape(bsz * t_len, d)
    new_states = ([], [], [], [], [])
    for l in range(depth):
        x = _ffn(x, ffn1_norm, ffn1_up, ffn1_down, l)
        p_a, p_f, p_b, p_c, p_s = _inproj(x, mix_norm, w_cat, l, widths)
        if states is None:
            st_a = st_b = st_c = None
        else:
            s_hgrn, c_ml, n_ml, m_ml, s_gla = states
            st_a = s_hgrn[l]
            m_pad = jnp.zeros((bsz, 1, LANE), F32).at[:, 0, :bh].set(m_ml[l])
            st_b = (c_ml[l], n_ml[l], m_pad)
            st_c = s_gla[l].reshape(bsz, ch // 2, 2 * cdk, cdv)
        o_a, s_a = _glr("hgrn", l, bsz, t_len, (p_a, p_f, hgrn_lb_logits, hgrn_norm), st_a)
        o_b, c_new, n_new, m_new = _mlstm(l, bsz, t_len, p_b, p_s, bias_row, mlstm_norm, st_b)
        o_c, s_c = _glr("gla", l, bsz, t_len, (p_c, p_s, w2_pad, gla_b_a, gla_norm), st_c)
        x = _merge(x, mix_norm, o_a, o_b, o_c, w_mg, w_branch, w_out, l)
        x = _ffn(x, ffn2_norm, ffn2_up, ffn2_down, l,
                 final_w=final_norm if l == depth - 1 else None)
        for acc, s in zip(new_states, (s_a, c_new, n_new, m_new[:, 0, :bh],
                                       s_c.reshape(bsz, ch, cdk, cdv))):
            acc.append(s)
    return x.reshape(bsz, t_len, d), tuple(jnp.stack(acc) for acc in new_states)


def kernel(x_prompt, x_sample, state_hgrn, state_mlstm_c, state_mlstm_n, state_mlstm_m, state_gla, ffn1_norm, ffn1_w_up, ffn1_w_down, mix_norm, w_in, hgrn_lb_logits, hgrn_norm, mlstm_gate_bias, mlstm_norm, gla_w_a2, gla_b_a, gla_norm, w_branch, w_out, ffn2_norm, ffn2_w_up, ffn2_w_down, final_norm):
    heads = (state_hgrn.shape[2:], state_mlstm_c.shape[2:], state_gla.shape[2:], gla_w_a2.shape[1])
    assert heads[0][1:] == (LANE, LANE) and heads[1][1:] == (LANE, LANE)
    assert heads[2][1:] == (LANE // 2, LANE) and heads[2][0] % 2 == 0
    w_cat, widths, w_mg, w2_pad, bias_row = _prep_weights(w_in, gla_w_a2, mlstm_gate_bias, heads)
    row = lambda a: a[:, None, :]
    wts = (row(ffn1_norm), ffn1_w_up.astype(BF16), ffn1_w_down.astype(BF16), row(mix_norm),
           w_cat, widths, w_mg, hgrn_lb_logits, row(hgrn_norm), bias_row, row(mlstm_norm),
           w2_pad, row(gla_b_a), row(gla_norm), w_branch.astype(BF16), w_out.astype(BF16),
           row(ffn2_norm), ffn2_w_up.astype(BF16), ffn2_w_down.astype(BF16), final_norm[None, :])
    y_p, st_p = _run_trunk(x_prompt, None, wts, heads)
    sample_states = (state_hgrn, state_mlstm_c, state_mlstm_n, state_mlstm_m, state_gla)
    y_s, st_s = _run_trunk(x_sample, sample_states, wts, heads)
    return (y_p, y_s) + st_p + st_s
```

```python
import functools

import jax
import jax.numpy as jnp
from jax import lax
from jax.experimental import pallas as pl
from jax.experimental.pallas import tpu as pltpu

F32 = jnp.float32
BF16 = jnp.bfloat16

EPS = 1e-6
CHUNK = 64
MLSTM_CHUNK = 256
GLA_GATE_TEMP = 16.0
N_BRANCH = 3
LANE = 128
V7X_VMEM_BYTES = 64 * 1024 * 1024
VMEM_LIMIT = V7X_VMEM_BYTES - 8 * 1024 * 1024
SAFE_DECAY_SPAN = 60.0

_NT = (((1,), (1,)), ((), ()))
_TN = (((0,), (0,)), ((), ()))


def _dot(a, b):
    return jnp.dot(a, b, preferred_element_type=F32)


def _dot_nt(a, b):
    return lax.dot_general(a, b, _NT, preferred_element_type=F32)


def _dot_tn(a, b):
    return lax.dot_general(a, b, _TN, preferred_element_type=F32)


def _rms(x, w):
    return x * lax.rsqrt(jnp.mean(x * x, axis=-1, keepdims=True) + EPS) * w


def _sigmoid(x):
    return 1.0 / (1.0 + jnp.exp(-x))


def _log_sigmoid(x):
    return jnp.minimum(x, 0.0) - jnp.log(1.0 + jnp.exp(-jnp.abs(x)))


def _split3(x):
    hi = x.astype(BF16)
    r = x - hi.astype(F32)
    mid = r.astype(BF16)
    lo = (r - mid.astype(F32)).astype(BF16)
    return hi, mid, lo


def _cumsum_rows(tri, g):
    hi, mid, lo = _split3(g)
    return (_dot(tri, hi) + _dot(tri, mid)) + _dot(tri, lo)


def _rows_to_lanes(sel, x):
    hi, mid, lo = _split3(x)
    return (_dot_nt(sel, hi) + _dot_nt(sel, mid)) + _dot_nt(sel, lo)


def _pick_tile(n, cap):
    t = min(n, cap)
    while n % t or (t % 8 and t != n):
        t -= 1
    return t


def _params(sem):
    return pltpu.CompilerParams(dimension_semantics=sem, vmem_limit_bytes=VMEM_LIMIT)


def _ffn_kernel(x_ref, nw_ref, wg_ref, wu_ref, wd_ref, *rest, n_ff, final):
    if final:
        fw_ref, o_ref, h_scr, acc_scr = rest
    else:
        o_ref, h_scr, acc_scr = rest
    j = pl.program_id(1)

    @pl.when(j == 0)
    def _():
        h_scr[...] = _rms(x_ref[...], nw_ref[...]).astype(BF16)
        acc_scr[...] = jnp.zeros_like(acc_scr)

    h = h_scr[...]
    g = _dot(h, wg_ref[...])
    u = _dot(h, wu_ref[...])
    act = (g * _sigmoid(g) * u).astype(BF16)
    acc_scr[...] += _dot(act, wd_ref[...])

    @pl.when(j == n_ff - 1)
    def _():
        y = x_ref[...] + 0.5 * acc_scr[...]
        if final:
            y = _rms(y, fw_ref[...])
        o_ref[...] = y


def _ffn(x, norm_w, w_up, w_down, layer, final_w=None):
    n, d = x.shape
    d_ff = w_down.shape[1]
    tm = _pick_tile(n, 1024)
    tf = 256
    n_ff = d_ff // tf
    assert d_ff % tf == 0
    final = final_w is not None
    in_specs = [
        pl.BlockSpec((tm, d), lambda i, j: (i, 0)),
        pl.BlockSpec((None, 1, d), lambda i, j: (layer, 0, 0)),
        pl.BlockSpec((None, d, tf), lambda i, j: (layer, 0, j)),
        pl.BlockSpec((None, d, tf), lambda i, j: (layer, 0, j + n_ff)),
        pl.BlockSpec((None, tf, d), lambda i, j: (layer, j, 0)),
    ]
    args = [x, norm_w, w_up, w_up, w_down]
    if final:
        in_specs.append(pl.BlockSpec((1, d), lambda i, j: (0, 0)))
        args.append(final_w)
    return pl.pallas_call(
        functools.partial(_ffn_kernel, n_ff=n_ff, final=final),
        grid=(n // tm, n_ff),
        in_specs=in_specs,
        out_specs=pl.BlockSpec((tm, d), lambda i, j: (i, 0)),
        out_shape=jax.ShapeDtypeStruct((n, d), F32),
        scratch_shapes=[pltpu.VMEM((tm, d), BF16), pltpu.VMEM((tm, d), F32)],
        compiler_params=_params(("parallel", "arbitrary")),
        name="ffn",
    )(*args)


def _inproj_kernel(x_ref, nw_ref, w_ref, oa_ref, of_ref, ob_ref, oc_ref, os_ref, *, widths):
    h = _rms(x_ref[...], nw_ref[...]).astype(BF16)
    c0 = 0
    for ref, width in zip((oa_ref, of_ref, ob_ref, oc_ref, os_ref), widths):
        step = min(width, 512)
        for c in range(0, width, step):
            ref[:, c:c + step] = _dot(h, w_ref[:, c0 + c:c0 + c + step]).astype(ref.dtype)
        c0 += width


def _inproj(x, norm_w, w_cat, layer, widths):
    n, d = x.shape
    tm = _pick_tile(n, 512)
    dtypes = (BF16, F32, BF16, BF16, F32)
    return pl.pallas_call(
        functools.partial(_inproj_kernel, widths=widths),
        grid=(n // tm,),
        in_specs=[
            pl.BlockSpec((tm, d), lambda i: (i, 0)),
            pl.BlockSpec((None, 1, d), lambda i: (layer, 0, 0)),
            pl.BlockSpec((None, d, sum(widths)), lambda i: (layer, 0, 0)),
        ],
        out_specs=[pl.BlockSpec((tm, w), lambda i: (i, 0)) for w in widths],
        out_shape=[jax.ShapeDtypeStruct((n, w), dt) for w, dt in zip(widths, dtypes)],
        compiler_params=_params(("parallel",)),
        name="inproj",
    )(x, norm_w, w_cat)


def _merge_kernel(x_ref, nw_ref, oa_ref, ob_ref, oc_ref, wm_ref, wb_ref, wo_ref, o_ref):
    x = x_ref[...]
    d = x.shape[1]
    h = _rms(x, nw_ref[...]).astype(BF16)
    merged = None
    for nb, br_ref in enumerate((oa_ref, ob_ref, oc_ref)):
        gate = _sigmoid(_dot(h, wm_ref[:, nb * d:(nb + 1) * d]))
        term = gate * _dot(br_ref[...], wb_ref[nb])
        merged = term if merged is None else merged + term
    o_ref[...] = x + _dot(merged.astype(BF16), wo_ref[...])


def _merge(x, norm_w, o_a, o_b, o_c, w_mg, w_branch, w_out, layer):
    n, d = x.shape
    bw = o_a.shape[1]
    tm = _pick_tile(n, 512)
    br_spec = pl.BlockSpec((tm, bw), lambda i: (i, 0))
    return pl.pallas_call(
        _merge_kernel,
        grid=(n // tm,),
        in_specs=[
            pl.BlockSpec((tm, d), lambda i: (i, 0)),
            pl.BlockSpec((None, 1, d), lambda i: (layer, 0, 0)),
            br_spec, br_spec, br_spec,
            pl.BlockSpec((None, d, N_BRANCH * d), lambda i: (layer, 0, 0)),
            pl.BlockSpec((None, N_BRANCH, bw, d), lambda i: (layer, 0, 0, 0)),
            pl.BlockSpec((None, d, d), lambda i: (layer, 0, 0)),
        ],
        out_specs=pl.BlockSpec((tm, d), lambda i: (i, 0)),
        out_shape=jax.ShapeDtypeStruct((n, d), F32),
        compiler_params=_params(("parallel",)),
        name="merge",
    )(x, norm_w, o_a, o_b, o_c, w_mg, w_branch, w_out)


def _chunk_consts(length, n_chunks=1):
    r = lax.broadcasted_iota(jnp.int32, (length, length), 0)
    c = lax.broadcasted_iota(jnp.int32, (length, length), 1)
    causal = r >= c
    tb = length * n_chunks
    rb = lax.broadcasted_iota(jnp.int32, (tb, tb), 0)
    cb = lax.broadcasted_iota(jnp.int32, (tb, tb), 1)
    same = None
    for i in range(n_chunks):
        blk = (rb >= i * length) & (rb < (i + 1) * length) & (cb >= i * length)
        same = blk if same is None else same | blk
    tri = jnp.where(same & (rb >= cb), 1.0, 0.0).astype(BF16)
    return causal, tri


def _head_norm(o, w):
    return o * lax.rsqrt(jnp.mean(o * o, axis=-1, keepdims=True) + EPS) * w


def _glr_intra_exact(q, k, g, vs, masks, tmp_ref, tri):
    length = q.shape[0]
    b = _cumsum_rows(tri, g)
    tmp_ref[0, 0:length, :] = k
    tmp_ref[1, 0:length, :] = b
    for h, v in enumerate(vs):
        tmp_ref[2 + h, 0:length, :] = v
    t_idx = lax.broadcasted_iota(jnp.int32, (length, 1), 0)

    def body(s, accs):
        k_s = tmp_ref[0, pl.ds(s, 1), :]
        b_s = tmp_ref[1, pl.ds(s, 1), :]
        a = q * k_s * jnp.exp(jnp.minimum(b - b_s, 0.0))
        new = []
        for h, m in enumerate(masks):
            ah = a if m is None else a * m
            col = jnp.where(t_idx >= s, jnp.sum(ah, axis=1, keepdims=True), 0.0)
            new.append(accs[h] + col * tmp_ref[2 + h, pl.ds(s, 1), :])
        return tuple(new)

    init = tuple(jnp.zeros((length, v.shape[1]), F32) for v in vs)
    return lax.fori_loop(0, length, body, init)


def _lane_masks(n_sub):
    if n_sub == 1:
        return [None]
    lane = lax.broadcasted_iota(jnp.int32, (1, LANE), 1)
    w = LANE // n_sub
    return [jnp.where((lane >= h * w) & (lane < (h + 1) * w), 1.0, 0.0) for h in range(n_sub)]


def _glr_kernel(*refs, kind, layer, length, n_chunks, n_groups, n_sub, has_state):
    if kind == "hgrn":
        q_ref, v_ref, og_ref, f_ref, lbl_ref, nw_ref = refs[:6]
        refs = refs[6:]
    else:
        q_ref, k_ref, v_ref, og_ref, s_ref, w2_ref, ba_ref, nw_ref = refs[:8]
        refs = refs[8:]
    if has_state:
        s0_ref, refs = refs[0], refs[1:]
    (o_ref, sout_ref, st_scr, qt_scr, qin_scr, kt_scr, kst_scr, dec_scr, o_scr, inter_scr,
     tmp_scr) = refs
    t = pl.program_id(1)
    n_heads = n_groups * n_sub
    dv = LANE
    tb = length * n_chunks

    @pl.when(t == 0)
    def _():
        for gi in range(n_groups):
            if has_state:
                st_scr[gi] = s0_ref[gi].T
            else:
                st_scr[gi] = jnp.zeros((dv, LANE), F32)

    causal, tri_blk = _chunk_consts(length, n_chunks)
    masks = _lane_masks(n_sub)

    if kind == "hgrn":
        logits = lbl_ref[...]
        e = jnp.exp(logits - jnp.max(logits, axis=0, keepdims=True))
        p = e / jnp.sum(e, axis=0, keepdims=True)
        lb = jnp.zeros((1, p.shape[1]), F32)
        for j in range(1, layer + 1):
            lb = lb + p[j:j + 1, :]

    def load_group(rows, gi, with_v=True):
        lanes = slice(gi * LANE, (gi + 1) * LANE)
        if kind == "hgrn":
            aq = q_ref[rows, lanes].astype(F32)
            q = aq * _sigmoid(aq)
            x = f_ref[rows, lanes]
            lbh = lb[:, lanes]
            g = jnp.log(lbh + (1.0 - lbh) * _sigmoid(x))
            k = (1.0 - lbh) * _sigmoid(-x)
        else:
            q = q_ref[rows, lanes].astype(F32) * (float(LANE // n_sub) ** -0.5)
            k = k_ref[rows, lanes].astype(F32)
            pre = _dot(s_ref[rows, :].astype(BF16), w2_ref[:, lanes]) + ba_ref[:, lanes]
            g = _log_sigmoid(pre) / GLA_GATE_TEMP
        vs = [v_ref[rows, (gi * n_sub + h) * dv:(gi * n_sub + h + 1) * dv].astype(F32)
              for h in range(n_sub)] if with_v else None
        return q, k, g, vs

    def finish(rows, head, o):
        lanes = slice(head * dv, (head + 1) * dv)
        og = og_ref[rows, lanes].astype(F32)
        if kind == "hgrn":
            y = _head_norm(o * _sigmoid(og), nw_ref[:, lanes])
        else:
            y = _head_norm(o, nw_ref[:, lanes]) * (og * _sigmoid(og))
        o_ref[rows, lanes] = y.astype(o_ref.dtype)

    span_vec = jnp.zeros((1, LANE), F32)
    for gi in range(n_groups):
        glanes = slice(gi * LANE, (gi + 1) * LANE)
        q, k, g, _ = load_group(slice(0, tb), gi, with_v=False)
        b = _cumsum_rows(tri_blk, g)
        for c in range(n_chunks):
            sl = slice(c * length, (c + 1) * length)
            bc, qc, kc = b[sl], q[sl], k[sl]
            b_mid = bc[length // 2:length // 2 + 1, :]
            b_last = bc[length - 1:length, :]
            span_vec = jnp.maximum(span_vec, jnp.maximum(bc[0:1, :] - b_mid, b_mid - b_last))
            qt = qc * jnp.exp(bc - b_mid)
            q_in = qc * jnp.exp(bc)
            kt_scr[sl, glanes] = (kc * jnp.exp(b_mid - bc)).astype(BF16)
            kst_scr[sl, glanes] = (kc * jnp.exp(b_last - bc)).astype(BF16)
            dec_scr[c:c + 1, glanes] = jnp.exp(b_last)
            for h, m in enumerate(masks):
                hl = slice((gi * n_sub + h) * LANE, (gi * n_sub + h + 1) * LANE)
                qt_scr[sl, hl] = (qt if m is None else qt * m).astype(BF16)
                qin_scr[sl, hl] = (q_in if m is None else q_in * m).astype(BF16)

    for gi in range(n_groups):
        glanes = slice(gi * LANE, (gi + 1) * LANE)
        st = st_scr[gi]
        for c in range(n_chunks):
            sl = slice(c * length, (c + 1) * length)
            kt = kt_scr[sl, glanes]
            kst = kst_scr[sl, glanes]
            st_b = st.astype(BF16)
            upd = None
            for h, m in enumerate(masks):
                hl = slice((gi * n_sub + h) * LANE, (gi * n_sub + h + 1) * LANE)
                vb = v_ref[sl, hl]
                sc = jnp.where(causal, _dot_nt(qt_scr[sl, hl], kt), 0.0).astype(BF16)
                o_inter = _dot_nt(qin_scr[sl, hl], st_b)
                inter_scr[sl, hl] = o_inter
                o_scr[sl, hl] = _dot(sc, vb) + o_inter
                u = _dot_tn(vb, kst)
                if m is not None:
                    u = u * m
                upd = u if upd is None else upd + u
            st = st * dec_scr[c:c + 1, glanes] + upd
        st_scr[gi] = st

    for head in range(n_heads):
        finish(slice(0, tb), head, o_scr[:, head * dv:(head + 1) * dv])

    @pl.when(jnp.max(span_vec) > SAFE_DECAY_SPAN)
    def _():
        tri = jnp.where(causal, 1.0, 0.0).astype(BF16)

        def chunk(c, carry):
            rows = pl.ds(pl.multiple_of(c * length, length), length)
            for gi in range(n_groups):
                q, k, g, vs = load_group(rows, gi)
                intra = _glr_intra_exact(q, k, g, vs, masks, tmp_scr, tri)
                for h in range(n_sub):
                    head = gi * n_sub + h
                    finish(rows, head, intra[h] + inter_scr[rows, head * dv:(head + 1) * dv])
            return carry

        lax.fori_loop(0, n_chunks, chunk, 0)

    @pl.when(t == pl.num_programs(1) - 1)
    def _():
        for gi in range(n_groups):
            sout_ref[gi] = st_scr[gi].T


def _time_block(kind, t_len):
    tb = _pick_tile(t_len, 256)
    length = min(CHUNK, t_len) if kind == "hgrn" else tb
    assert tb % length == 0
    return length, tb


def _glr(kind, layer, bsz, t_len, ins, s0):
    length, tb = _time_block(kind, t_len)
    n_groups, n_sub = (4, 1) if kind == "hgrn" else (2, 2)
    n_heads = n_groups * n_sub
    bw = n_heads * LANE
    has_state = s0 is not None

    def tok(width, blk):
        return pl.BlockSpec((None, tb, width), lambda b, t: (b, t, blk))

    def full(shape):
        return pl.BlockSpec(shape, lambda b, t: (0,) * len(shape))

    if kind == "hgrn":
        a, f, lbl, nw = ins
        a = a.reshape(bsz, t_len, -1)
        f = f.reshape(bsz, t_len, -1)
        args = [a, a, a, f, lbl, nw]
        in_specs = [tok(bw, 0), tok(bw, 1), tok(bw, 2), tok(bw, 0), full(lbl.shape),
                    pl.BlockSpec((None, 1, bw), lambda b, t: (layer, 0, 0))]
    else:
        c, s, w2, ba, nw = ins
        c = c.reshape(bsz, t_len, -1)
        s = s.reshape(bsz, t_len, -1)
        kw = n_groups * LANE
        args = [c, c, c, c, s, w2, ba, nw]
        in_specs = [tok(kw, 0), tok(kw, 1), tok(bw, 1), tok(bw, 2), tok(LANE, 0),
                    pl.BlockSpec((None, LANE, kw), lambda b, t: (layer, 0, 0)),
                    pl.BlockSpec((None, 1, kw), lambda b, t: (layer, 0, 0)),
                    pl.BlockSpec((None, 1, bw), lambda b, t: (layer, 0, 0))]
    st_spec = pl.BlockSpec((None, n_groups, LANE, LANE), lambda b, t: (b, 0, 0, 0))
    if has_state:
        args.append(s0)
        in_specs.append(st_spec)
    o, s_new = pl.pallas_call(
        functools.partial(_glr_kernel, kind=kind, layer=layer, length=length,
                          n_chunks=tb // length, n_groups=n_groups, n_sub=n_sub,
                          has_state=has_state),
        grid=(bsz, t_len // tb),
        in_specs=in_specs,
        out_specs=[pl.BlockSpec((None, tb, bw), lambda b, t: (b, t, 0)), st_spec],
        out_shape=[jax.ShapeDtypeStruct((bsz, t_len, bw), BF16),
                   jax.ShapeDtypeStruct((bsz, n_groups, LANE, LANE), F32)],
        scratch_shapes=[pltpu.VMEM((n_groups, LANE, LANE), F32),
                        pltpu.VMEM((tb, bw), BF16),
                        pltpu.VMEM((tb, bw), BF16),
                        pltpu.VMEM((tb, n_groups * LANE), BF16),
                        pltpu.VMEM((tb, n_groups * LANE), BF16),
                        pltpu.VMEM((max(tb // length, 8), n_groups * LANE), F32),
                        pltpu.VMEM((tb, bw), F32),
                        pltpu.VMEM((tb, bw), F32),
                        pltpu.VMEM((2 + n_sub, length, LANE), F32)],
        compiler_params=_params(("parallel", "arbitrary")),
        name=kind,
    )(*args)
    return o.reshape(bsz * t_len, bw), s_new


def _mlstm_kernel(*refs, length, n_heads, has_state):
    q_ref, k_ref, v_ref, og_ref, s_ref, bias_ref, nw_ref = refs[:7]
    refs = refs[7:]
    if has_state:
        c0_ref, n0_ref, m0_ref = refs[:3]
        refs = refs[3:]
    o_ref, cout_ref, nout_ref, mout_ref, cn_scr, m_scr = refs
    t = pl.program_id(1)
    dk = dv = LANE
    scale = float(dk) ** -0.5
    lane = lax.broadcasted_iota(jnp.int32, (1, LANE), 1)

    @pl.when(t == 0)
    def _():
        for h in range(n_heads):
            if has_state:
                n_cols = jnp.broadcast_to(n0_ref[h:h + 1, :], (dv, dk)).T
                cn_scr[h] = jnp.concatenate([c0_ref[h], n_cols], axis=1)
                m_scr[h:h + 1, :] = jnp.broadcast_to(m0_ref[:, h:h + 1], (1, LANE))
            else:
                cn_scr[h] = jnp.zeros((dk, 2 * dv), F32)
                m_scr[h:h + 1, :] = jnp.zeros((1, LANE), F32)

    causal, tri = _chunk_consts(length)
    sel = jnp.where(lax.broadcasted_iota(jnp.int32, (8, LANE), 0)
                    == lax.broadcasted_iota(jnp.int32, (8, LANE), 1), 1.0, 0.0).astype(BF16)
    reps = -(-length // LANE)

    x = s_ref[...] + bias_ref[...]
    y = jnp.where(lane < n_heads, x, _cumsum_rows(tri, _log_sigmoid(x)))
    y_t = _rows_to_lanes(sel, y)
    ones = jnp.ones((length, dv), BF16)
    m_row = jnp.zeros((1, LANE), F32)

    for h in range(n_heads):
        lanes = slice(h * dk, (h + 1) * dk)
        q = q_ref[:, lanes]
        kf = k_ref[:, lanes].astype(F32) * scale
        v_ext = jnp.concatenate([v_ref[:, lanes], ones], axis=1)
        i_c = jnp.broadcast_to(y[:, h:h + 1], (length, LANE))
        b_c = jnp.broadcast_to(y[:, n_heads + h:n_heads + h + 1], (length, LANE))
        z_r = y_t[h:h + 1, :] - y_t[n_heads + h:n_heads + h + 1, :]
        b_last = b_c[length - 1:length, :]
        b_cw = jnp.concatenate([b_c] * reps, axis=1)[:, :length]

        d_intra = jnp.where(causal, b_cw + z_r, -jnp.inf)
        a_t = jnp.max(d_intra, axis=1, keepdims=True)
        p = jnp.exp(d_intra - a_t) * _dot_nt(q, kf.astype(BF16))
        num_ext = _dot(p.astype(BF16), v_ext)
        e_intra = b_last - b_c + i_c
        e_max = jnp.max(e_intra, axis=0, keepdims=True)
        kw = jnp.exp(e_intra - e_max) * kf
        u_ext = _dot_tn(kw.astype(BF16), v_ext)

        m_old = m_scr[h:h + 1, :]
        cn_old = cn_scr[h]
        d_inter = b_c + m_old
        m_t = jnp.maximum(d_inter, a_t)
        f_intra = jnp.exp(a_t - m_t)
        w_inter = jnp.exp(d_inter - m_t)
        qc_ext = _dot(q, cn_old.astype(BF16))
        tot = (jnp.concatenate([f_intra, f_intra], axis=1) * num_ext
               + jnp.concatenate([w_inter, w_inter], axis=1) * qc_ext)
        hid = tot[:, :dv] / jnp.maximum(jnp.abs(tot[:, dv:]), jnp.exp(-m_t))

        e_inter = b_last + m_old
        m_new = jnp.maximum(e_inter, e_max)
        w_c = jnp.exp(e_inter - m_new)
        f_s = jnp.exp(e_max - m_new)
        cn_scr[h] = (jnp.concatenate([w_c, w_c], axis=1) * cn_old
                     + jnp.concatenate([f_s, f_s], axis=1) * u_ext)
        m_scr[h:h + 1, :] = m_new
        m_row = jnp.where(lane == h, m_new, m_row)

        og = og_ref[:, lanes].astype(F32)
        y_out = _sigmoid(og) * _head_norm(hid, nw_ref[:, lanes])
        o_ref[:, lanes] = y_out.astype(o_ref.dtype)

    @pl.when(t == pl.num_programs(1) - 1)
    def _():
        for h in range(n_heads):
            cn = cn_scr[h]
            cout_ref[h] = cn[:, :dv]
            nout_ref[h:h + 1, :] = cn[:, dv:].T[0:1, :]
        mout_ref[...] = m_row


def _mlstm_kernel_old(*refs, length, n_chunks, n_heads, has_state):
    q_ref, k_ref, v_ref, og_ref, s_ref, bias_ref, nw_ref = refs[:7]
    refs = refs[7:]
    if has_state:
        c0_ref, n0_ref, m0_ref = refs[:3]
        refs = refs[3:]
    o_ref, cout_ref, nout_ref, mout_ref, c_scr, n_scr, m_scr, o_scr = refs
    t = pl.program_id(1)
    dk = LANE
    scale = float(dk) ** -0.5

    @pl.when(t == 0)
    def _():
        if has_state:
            c_scr[...] = c0_ref[...]
            n_scr[...] = n0_ref[...]
            m_scr[...] = m0_ref[...]
        else:
            c_scr[...] = jnp.zeros_like(c_scr)
            n_scr[...] = jnp.zeros_like(n_scr)
            m_scr[...] = jnp.zeros_like(m_scr)

    causal, tri_blk = _chunk_consts(length, n_chunks)
    sel = jnp.where(lax.broadcasted_iota(jnp.int32, (8, LANE), 0)
                    == lax.broadcasted_iota(jnp.int32, (8, LANE), 1), 1.0, 0.0).astype(BF16)

    x = s_ref[...] + bias_ref[...]
    bc_all = _cumsum_rows(tri_blk, _log_sigmoid(x))
    m_all = m_scr[...]
    c_st = [c_scr[h] for h in range(n_heads)]
    n_st = [n_scr[h:h + 1, :] for h in range(n_heads)]
    m_st = [m_all[:, h:h + 1] for h in range(n_heads)]

    for c in range(n_chunks):
        sl = slice(c * length, (c + 1) * length)
        bt_all = _rows_to_lanes(sel, bc_all[sl])
        it_all = _rows_to_lanes(sel, x[sl])
        for h in range(n_heads):
            lanes = slice(h * dk, (h + 1) * dk)
            q = q_ref[sl, lanes]
            kf = k_ref[sl, lanes].astype(F32) * scale
            vb = v_ref[sl, lanes]
            b_c = bc_all[sl, n_heads + h:n_heads + h + 1]
            i_c = x[sl, h:h + 1]
            b_r = bt_all[n_heads + h:n_heads + h + 1, :]
            i_r = it_all[h:h + 1, :]
            b_last = b_c[length - 1:length, :]

            d_intra = jnp.where(causal, b_c - b_r + i_r, -jnp.inf)
            a_t = jnp.max(d_intra, axis=1, keepdims=True)
            p = jnp.exp(d_intra - a_t) * _dot_nt(q, kf.astype(BF16))
            num_pre = _dot(p.astype(BF16), vb)
            sum_pre = jnp.sum(p, axis=1, keepdims=True)
            e_intra = b_last - b_c + i_c
            e_max = jnp.max(e_intra, axis=0, keepdims=True)
            kw = jnp.exp(e_intra - e_max) * kf
            u_pre = _dot_tn(kw.astype(BF16), vb)
            nk_pre = jnp.sum(kw, axis=0, keepdims=True)

            m_old, c_old, n_old = m_st[h], c_st[h], n_st[h]
            d_inter = b_c + m_old
            m_t = jnp.maximum(d_inter, a_t)
            f_intra = jnp.exp(a_t - m_t)
            w_inter = jnp.exp(d_inter - m_t)
            num = f_intra * num_pre + w_inter * _dot(q, c_old.astype(BF16))
            qn = jnp.sum(q.astype(F32) * n_old, axis=1, keepdims=True)
            den = f_intra * sum_pre + w_inter * qn
            o_scr[sl, lanes] = num / jnp.maximum(jnp.abs(den), jnp.exp(-m_t))

            e_inter = b_last + m_old
            m_new = jnp.maximum(e_inter, e_max)
            w_c = jnp.exp(e_inter - m_new)
            f_s = jnp.exp(e_max - m_new)
            c_st[h] = w_c * c_old + f_s * u_pre
            n_st[h] = w_c * n_old + f_s * nk_pre
            m_st[h] = m_new

    lane = lax.broadcasted_iota(jnp.int32, (1, LANE), 1)
    m_row = jnp.zeros((1, LANE), F32)
    for h in range(n_heads):
        lanes = slice(h * dk, (h + 1) * dk)
        c_scr[h] = c_st[h]
        n_scr[h:h + 1, :] = n_st[h]
        m_row = jnp.where(lane == h, m_st[h], m_row)
        og = og_ref[:, lanes].astype(F32)
        y = _sigmoid(og) * _head_norm(o_scr[:, lanes], nw_ref[:, lanes])
        o_ref[:, lanes] = y.astype(o_ref.dtype)
    m_scr[...] = m_row

    @pl.when(t == pl.num_programs(1) - 1)
    def _():
        cout_ref[...] = c_scr[...]
        nout_ref[...] = n_scr[...]
        mout_ref[...] = m_scr[...]


def _mlstm(layer, bsz, t_len, p_b, p_s, bias_row, norm_w, states):
    length = tb = _pick_tile(t_len, MLSTM_CHUNK)
    n_heads = p_b.shape[1] // (4 * LANE)
    bw = n_heads * LANE
    has_state = states is not None
    p_b = p_b.reshape(bsz, t_len, -1)
    p_s = p_s.reshape(bsz, t_len, -1)

    def tok(width, blk):
        return pl.BlockSpec((None, tb, width), lambda b, t: (b, t, blk))

    c_spec = pl.BlockSpec((None, n_heads, LANE, LANE), lambda b, t: (b, 0, 0, 0))
    n_spec = pl.BlockSpec((None, n_heads, LANE), lambda b, t: (b, 0, 0))
    m_spec = pl.BlockSpec((None, 1, LANE), lambda b, t: (b, 0, 0))
    args = [p_b, p_b, p_b, p_b, p_s, bias_row, norm_w]
    in_specs = [tok(bw, 0), tok(bw, 1), tok(bw, 2), tok(bw, 3), tok(LANE, 0),
                pl.BlockSpec((None, 1, LANE), lambda b, t: (layer, 0, 0)),
                pl.BlockSpec((None, 1, bw), lambda b, t: (layer, 0, 0))]
    if has_state:
        args += list(states)
        in_specs += [c_spec, n_spec, m_spec]
    o, c_new, n_new, m_new = pl.pallas_call(
        functools.partial(_mlstm_kernel, length=length, n_heads=n_heads, has_state=has_state),
        grid=(bsz, t_len // tb),
        in_specs=in_specs,
        out_specs=[pl.BlockSpec((None, tb, bw), lambda b, t: (b, t, 0)), c_spec, n_spec, m_spec],
        out_shape=[jax.ShapeDtypeStruct((bsz, t_len, bw), BF16),
                   jax.ShapeDtypeStruct((bsz, n_heads, LANE, LANE), F32),
                   jax.ShapeDtypeStruct((bsz, n_heads, LANE), F32),
                   jax.ShapeDtypeStruct((bsz, 1, LANE), F32)],
        scratch_shapes=[pltpu.VMEM((n_heads, LANE, 2 * LANE), F32),
                        pltpu.VMEM((n_heads, LANE), F32)],
        compiler_params=_params(("parallel", "arbitrary")),
        name="mlstm",
    )(*args)
    return o.reshape(bsz * t_len, bw), c_new, n_new, m_new


def _prep_weights(w_in, gla_w_a2, mlstm_gate_bias, heads):
    (ah, adk, adv), (bh, bdk, bdv), (ch, cdk, cdv), rank = heads
    sizes = (ah * adk, ah * adk, ah * adv, ah * adv,
             bh * bdk, bh * bdk, bh * bdv, bh * bdv, 2 * bh,
             ch * cdk, ch * cdk, ch * cdv, ch * cdv, rank)
    offs = [0]
    for s in sizes:
        offs.append(offs[-1] + s)
    col = lambda i: w_in[:, :, offs[i]:offs[i + 1]]
    (a_q, a_f, a_i, a_g, b_q, b_k, b_v, b_o, b_if, c_q, c_k, c_v, c_g, c_lr) = [col(i) for i in range(14)]
    pad = jnp.zeros(w_in.shape[:2] + (LANE - 2 * bh - rank,), w_in.dtype)
    groups = ([a_q, a_i, a_g], [a_f], [b_q, b_k, b_v, b_o], [c_q, c_k, c_v, c_g], [b_if, c_lr, pad])
    widths = tuple(sum(g.shape[-1] for g in grp) for grp in groups)
    w_cat = jnp.concatenate([g for grp in groups for g in grp], axis=-1).astype(BF16)
    w_mg = w_in[:, :, offs[14]:].astype(BF16)
    depth = w_in.shape[0]
    w2_pad = jnp.zeros((depth, LANE, gla_w_a2.shape[-1]), F32)
    w2_pad = w2_pad.at[:, 2 * bh:2 * bh + rank, :].set(gla_w_a2).astype(BF16)
    bias_row = jnp.zeros((depth, 1, LANE), F32).at[:, 0, :2 * bh].set(mlstm_gate_bias)
    return w_cat, widths, w_mg, w2_pad, bias_row


def _run_trunk(x, states, wts, heads):
    (ffn1_norm, ffn1_up, ffn1_down, mix_norm, w_cat, widths, w_mg, hgrn_lb_logits, hgrn_norm,
     bias_row, mlstm_norm, w2_pad, gla_b_a, gla_norm, w_branch, w_out,
     ffn2_norm, ffn2_up, ffn2_down, final_norm) = wts
    (ah, adk, adv), (bh, bdk, bdv), (ch, cdk, cdv), _ = heads
    bsz, t_len, d = x.shape
    depth = ffn1_norm.shape[0]
    x = x.reshape(bsz * t_len, d)
    new_states = ([], [], [], [], [])
    for l in range(depth):
        x = _ffn(x, ffn1_norm, ffn1_up, ffn1_down, l)
        p_a, p_f, p_b, p_c, p_s = _inproj(x, mix_norm, w_cat, l, widths)
        if states is None:
            st_a = st_b = st_c = None
        else:
            s_hgrn, c_ml, n_ml, m_ml, s_gla = states
            st_a = s_hgrn[l]
            m_pad = jnp.zeros((bsz, 1, LANE), F32).at[:, 0, :bh].set(m_ml[l])
            st_b = (c_ml[l], n_ml[l], m_pad)
            st_c = s_gla[l].reshape(bsz, ch // 2, 2 * cdk, cdv)
        o_a, s_a = _glr("hgrn", l, bsz, t_len, (p_a, p_f, hgrn_lb_logits, hgrn_norm), st_a)
        o_b, c_new, n_new, m_new = _mlstm(l, bsz, t_len, p_b, p_s, bias_row, mlstm_norm, st_b)
        o_c, s_c = _glr("gla", l, bsz, t_len, (p_c, p_s, w2_pad, gla_b_a, gla_norm), st_c)
        x = _merge(x, mix_norm, o_a, o_b, o_c, w_mg, w_branch, w_out, l)
        x = _ffn(x, ffn2_norm, ffn2_up, ffn2_down, l,
                 final_w=final_norm if l == depth - 1 else None)
        for acc, s in zip(new_states, (s_a, c_new, n_new, m_new[:, 0, :bh],
                                       s_c.reshape(bsz, ch, cdk, cdv))):
            acc.append(s)
    return x.reshape(bsz, t_len, d), tuple(jnp.stack(acc) for acc in new_states)


def kernel(x_prompt, x_sample, state_hgrn, state_mlstm_c, state_mlstm_n, state_mlstm_m, state_gla, ffn1_norm, ffn1_w_up, ffn1_w_down, mix_norm, w_in, hgrn_lb_logits, hgrn_norm, mlstm_gate_bias, mlstm_norm, gla_w_a2, gla_b_a, gla_norm, w_branch, w_out, ffn2_norm, ffn2_w_up, ffn2_w_down, final_norm):
    heads = (state_hgrn.shape[2:], state_mlstm_c.shape[2:], state_gla.shape[2:], gla_w_a2.shape[1])
    assert heads[0][1:] == (LANE, LANE) and heads[1][1:] == (LANE, LANE)
    assert heads[2][1:] == (LANE // 2, LANE) and heads[2][0] % 2 == 0
    w_cat, widths, w_mg, w2_pad, bias_row = _prep_weights(w_in, gla_w_a2, mlstm_gate_bias, heads)
    row = lambda a: a[:, None, :]
    wts = (row(ffn1_norm), ffn1_w_up.astype(BF16), ffn1_w_down.astype(BF16), row(mix_norm),
           w_cat, widths, w_mg, hgrn_lb_logits, row(hgrn_norm), bias_row, row(mlstm_norm),
           w2_pad, row(gla_b_a), row(gla_norm), w_branch.astype(BF16), w_out.astype(BF16),
           row(ffn2_norm), ffn2_w_up.astype(BF16), ffn2_w_down.astype(BF16), final_norm[None, :])
    y_p, st_p = _run_trunk(x_prompt, None, wts, heads)
    sample_states = (state_hgrn, state_mlstm_c, state_mlstm_n, state_mlstm_m, state_gla)
    y_s, st_s = _run_trunk(x_sample, sample_states, wts, heads)
    return (y_p, y_s) + st_p + st_s
```

```python
import functools

import jax
import jax.numpy as jnp
from jax import lax
from jax.experimental import pallas as pl
from jax.experimental.pallas import tpu as pltpu

F32 = jnp.float32
BF16 = jnp.bfloat16

EPS = 1e-6
CHUNK = 64
MLSTM_CHUNK = 256
GLA_GATE_TEMP = 16.0
N_BRANCH = 3
LANE = 128
V7X_VMEM_BYTES = 64 * 1024 * 1024
VMEM_LIMIT = V7X_VMEM_BYTES - 8 * 1024 * 1024
SAFE_DECAY_SPAN = 60.0

_NT = (((1,), (1,)), ((), ()))
_TN = (((0,), (0,)), ((), ()))


def _dot(a, b):
    return jnp.dot(a, b, preferred_element_type=F32)


def _dot_nt(a, b):
    return lax.dot_general(a, b, _NT, preferred_element_type=F32)


def _dot_tn(a, b):
    return lax.dot_general(a, b, _TN, preferred_element_type=F32)


def _rms(x, w):
    return x * lax.rsqrt(jnp.mean(x * x, axis=-1, keepdims=True) + EPS) * w


def _sigmoid(x):
    return 1.0 / (1.0 + jnp.exp(-x))


def _log_sigmoid(x):
    return jnp.minimum(x, 0.0) - jnp.log(1.0 + jnp.exp(-jnp.abs(x)))


def _split3(x):
    hi = x.astype(BF16)
    r = x - hi.astype(F32)
    mid = r.astype(BF16)
    lo = (r - mid.astype(F32)).astype(BF16)
    return hi, mid, lo


def _cumsum_rows(tri, g):
    hi, mid, lo = _split3(g)
    return (_dot(tri, hi) + _dot(tri, mid)) + _dot(tri, lo)


def _rows_to_lanes(sel, x):
    hi, mid, lo = _split3(x)
    return (_dot_nt(sel, hi) + _dot_nt(sel, mid)) + _dot_nt(sel, lo)


def _pick_tile(n, cap):
    t = min(n, cap)
    while n % t or (t % 8 and t != n):
        t -= 1
    return t


def _params(sem):
    return pltpu.CompilerParams(dimension_semantics=sem, vmem_limit_bytes=VMEM_LIMIT)


def _ffn_kernel(x_ref, nw_ref, wup_ref, wd_ref, *rest, d_ff, tf, tn, final):
    if final:
        fw_ref, o_ref, act_scr = rest
    else:
        o_ref, act_scr = rest
    d = x_ref.shape[1]
    h = _rms(x_ref[...], nw_ref[...]).astype(BF16)
    for c in range(0, d_ff, tf):
        g = _dot(h, wup_ref[:, c:c + tf])
        u = _dot(h, wup_ref[:, d_ff + c:d_ff + c + tf])
        act_scr[:, c:c + tf] = (g * _sigmoid(g) * u).astype(BF16)
    for c in range(0, d, tn):
        o_ref[:, c:c + tn] = x_ref[:, c:c + tn] + 0.5 * _dot(act_scr[...], wd_ref[:, c:c + tn])
    if final:
        o_ref[...] = _rms(o_ref[...], fw_ref[...])


def _ffn(x, norm_w, w_up, w_down, layer, final_w=None):
    n, d = x.shape
    d_ff = w_down.shape[1]
    tm = _pick_tile(n, 1024)
    tf = tn = 256
    assert d_ff % tf == 0 and d % tn == 0
    final = final_w is not None
    resident = pl.Buffered(1)
    in_specs = [
        pl.BlockSpec((tm, d), lambda i: (i, 0)),
        pl.BlockSpec((None, 1, d), lambda i: (layer, 0, 0)),
        pl.BlockSpec((None, d, 2 * d_ff), lambda i: (layer, 0, 0), pipeline_mode=resident),
        pl.BlockSpec((None, d_ff, d), lambda i: (layer, 0, 0), pipeline_mode=resident),
    ]
    args = [x, norm_w, w_up, w_down]
    if final:
        in_specs.append(pl.BlockSpec((1, d), lambda i: (0, 0)))
        args.append(final_w)
    return pl.pallas_call(
        functools.partial(_ffn_kernel, d_ff=d_ff, tf=tf, tn=tn, final=final),
        grid=(n // tm,),
        in_specs=in_specs,
        out_specs=pl.BlockSpec((tm, d), lambda i: (i, 0)),
        out_shape=jax.ShapeDtypeStruct((n, d), F32),
        scratch_shapes=[pltpu.VMEM((tm, d_ff), BF16)],
        compiler_params=_params(("parallel",)),
        name="ffn",
    )(*args)


def _inproj_kernel(x_ref, nw_ref, w_ref, oa_ref, of_ref, ob_ref, oc_ref, os_ref, *, widths):
    h = _rms(x_ref[...], nw_ref[...]).astype(BF16)
    c0 = 0
    for ref, width in zip((oa_ref, of_ref, ob_ref, oc_ref, os_ref), widths):
        step = min(width, 512)
        for c in range(0, width, step):
            ref[:, c:c + step] = _dot(h, w_ref[:, c0 + c:c0 + c + step]).astype(ref.dtype)
        c0 += width


def _inproj(x, norm_w, w_cat, layer, widths):
    n, d = x.shape
    tm = _pick_tile(n, 512)
    dtypes = (BF16, F32, BF16, BF16, F32)
    return pl.pallas_call(
        functools.partial(_inproj_kernel, widths=widths),
        grid=(n // tm,),
        in_specs=[
            pl.BlockSpec((tm, d), lambda i: (i, 0)),
            pl.BlockSpec((None, 1, d), lambda i: (layer, 0, 0)),
            pl.BlockSpec((None, d, sum(widths)), lambda i: (layer, 0, 0)),
        ],
        out_specs=[pl.BlockSpec((tm, w), lambda i: (i, 0)) for w in widths],
        out_shape=[jax.ShapeDtypeStruct((n, w), dt) for w, dt in zip(widths, dtypes)],
        compiler_params=_params(("parallel",)),
        name="inproj",
    )(x, norm_w, w_cat)


def _merge_kernel(x_ref, nw_ref, oa_ref, ob_ref, oc_ref, wm_ref, wb_ref, wo_ref, o_ref):
    x = x_ref[...]
    d = x.shape[1]
    h = _rms(x, nw_ref[...]).astype(BF16)
    merged = None
    for nb, br_ref in enumerate((oa_ref, ob_ref, oc_ref)):
        gate = _sigmoid(_dot(h, wm_ref[:, nb * d:(nb + 1) * d]))
        term = gate * _dot(br_ref[...], wb_ref[nb])
        merged = term if merged is None else merged + term
    o_ref[...] = x + _dot(merged.astype(BF16), wo_ref[...])


def _merge(x, norm_w, o_a, o_b, o_c, w_mg, w_branch, w_out, layer):
    n, d = x.shape
    bw = o_a.shape[1]
    tm = _pick_tile(n, 512)
    br_spec = pl.BlockSpec((tm, bw), lambda i: (i, 0))
    return pl.pallas_call(
        _merge_kernel,
        grid=(n // tm,),
        in_specs=[
            pl.BlockSpec((tm, d), lambda i: (i, 0)),
            pl.BlockSpec((None, 1, d), lambda i: (layer, 0, 0)),
            br_spec, br_spec, br_spec,
            pl.BlockSpec((None, d, N_BRANCH * d), lambda i: (layer, 0, 0)),
            pl.BlockSpec((None, N_BRANCH, bw, d), lambda i: (layer, 0, 0, 0)),
            pl.BlockSpec((None, d, d), lambda i: (layer, 0, 0)),
        ],
        out_specs=pl.BlockSpec((tm, d), lambda i: (i, 0)),
        out_shape=jax.ShapeDtypeStruct((n, d), F32),
        compiler_params=_params(("parallel",)),
        name="merge",
    )(x, norm_w, o_a, o_b, o_c, w_mg, w_branch, w_out)


def _chunk_consts(length, n_chunks=1):
    tb = length * n_chunks
    rb = lax.broadcasted_iota(jnp.int32, (tb, tb), 0)
    cb = lax.broadcasted_iota(jnp.int32, (tb, tb), 1)
    causal = rb >= cb
    for i in range(1, n_chunks):
        causal = causal & ((rb < i * length) | (cb >= i * length))
    return causal, jnp.where(causal, 1.0, 0.0).astype(BF16)


def _head_norm(o, w):
    return o * lax.rsqrt(jnp.mean(o * o, axis=-1, keepdims=True) + EPS) * w


def _glr_intra_exact(q, k, g, vs, masks, tmp_ref, tri):
    length = q.shape[0]
    b = _cumsum_rows(tri, g)
    tmp_ref[0, 0:length, :] = k
    tmp_ref[1, 0:length, :] = b
    for h, v in enumerate(vs):
        tmp_ref[2 + h, 0:length, :] = v
    t_idx = lax.broadcasted_iota(jnp.int32, (length, 1), 0)

    def body(s, accs):
        k_s = tmp_ref[0, pl.ds(s, 1), :]
        b_s = tmp_ref[1, pl.ds(s, 1), :]
        a = q * k_s * jnp.exp(jnp.minimum(b - b_s, 0.0))
        new = []
        for h, m in enumerate(masks):
            ah = a if m is None else a * m
            col = jnp.where(t_idx >= s, jnp.sum(ah, axis=1, keepdims=True), 0.0)
            new.append(accs[h] + col * tmp_ref[2 + h, pl.ds(s, 1), :])
        return tuple(new)

    init = tuple(jnp.zeros((length, v.shape[1]), F32) for v in vs)
    return lax.fori_loop(0, length, body, init)


def _lane_masks(n_sub):
    if n_sub == 1:
        return [None]
    lane = lax.broadcasted_iota(jnp.int32, (1, LANE), 1)
    w = LANE // n_sub
    return [jnp.where((lane >= h * w) & (lane < (h + 1) * w), 1.0, 0.0) for h in range(n_sub)]


def _glr_kernel(*refs, kind, layer, length, n_chunks, n_groups, n_sub, has_state):
    if kind == "hgrn":
        q_ref, v_ref, og_ref, f_ref, lbl_ref, nw_ref = refs[:6]
        refs = refs[6:]
    else:
        q_ref, k_ref, v_ref, og_ref, s_ref, w2_ref, ba_ref, nw_ref = refs[:8]
        refs = refs[8:]
    if has_state:
        s0_ref, refs = refs[0], refs[1:]
    (o_ref, sout_ref, st_scr, qt_scr, qin_scr, kt_scr, kst_scr, dec_scr, o_scr, inter_scr,
     tmp_scr) = refs
    t = pl.program_id(1)
    n_heads = n_groups * n_sub
    dv = LANE
    tb = length * n_chunks

    xw = n_chunks * LANE

    def xcols(idx, c):
        return slice(idx * xw + c * LANE, idx * xw + (c + 1) * LANE)

    @pl.when(t == 0)
    def _():
        for gi in range(n_groups):
            if has_state:
                st_scr[gi] = s0_ref[gi].T
            else:
                st_scr[gi] = jnp.zeros((dv, LANE), F32)
        if n_chunks > 1:
            kst_scr[...] = jnp.zeros_like(kst_scr)
            qin_scr[...] = jnp.zeros_like(qin_scr)

    causal, tri_blk = _chunk_consts(length, n_chunks)
    masks = _lane_masks(n_sub)

    if kind == "hgrn":
        logits = lbl_ref[...]
        e = jnp.exp(logits - jnp.max(logits, axis=0, keepdims=True))
        p = e / jnp.sum(e, axis=0, keepdims=True)
        lb = jnp.zeros((1, p.shape[1]), F32)
        for j in range(1, layer + 1):
            lb = lb + p[j:j + 1, :]

    def load_group(rows, gi, with_v=True):
        lanes = slice(gi * LANE, (gi + 1) * LANE)
        if kind == "hgrn":
            aq = q_ref[rows, lanes].astype(F32)
            q = aq * _sigmoid(aq)
            x = f_ref[rows, lanes]
            lbh = lb[:, lanes]
            g = jnp.log(lbh + (1.0 - lbh) * _sigmoid(x))
            k = (1.0 - lbh) * _sigmoid(-x)
        else:
            q = q_ref[rows, lanes].astype(F32) * (float(LANE // n_sub) ** -0.5)
            k = k_ref[rows, lanes].astype(F32)
            pre = _dot(s_ref[rows, :].astype(BF16), w2_ref[:, lanes]) + ba_ref[:, lanes]
            g = _log_sigmoid(pre) / GLA_GATE_TEMP
        vs = [v_ref[rows, (gi * n_sub + h) * dv:(gi * n_sub + h + 1) * dv].astype(F32)
              for h in range(n_sub)] if with_v else None
        return q, k, g, vs

    def finish(rows, head, o):
        lanes = slice(head * dv, (head + 1) * dv)
        og = og_ref[rows, lanes].astype(F32)
        if kind == "hgrn":
            y = _head_norm(o * _sigmoid(og), nw_ref[:, lanes])
        else:
            y = _head_norm(o, nw_ref[:, lanes]) * (og * _sigmoid(og))
        o_ref[rows, lanes] = y.astype(o_ref.dtype)

    span_vec = jnp.zeros((1, LANE), F32)
    for gi in range(n_groups):
        glanes = slice(gi * LANE, (gi + 1) * LANE)
        q, k, g, _ = load_group(slice(0, tb), gi, with_v=False)
        b = _cumsum_rows(tri_blk, g)
        for c in range(n_chunks):
            sl = slice(c * length, (c + 1) * length)
            bc, qc, kc = b[sl], q[sl], k[sl]
            b_mid = bc[length // 2:length // 2 + 1, :]
            b_last = bc[length - 1:length, :]
            span_vec = jnp.maximum(span_vec, jnp.maximum(bc[0:1, :] - b_mid, b_mid - b_last))
            qt = qc * jnp.exp(bc - b_mid)
            q_in = qc * jnp.exp(bc)
            kt_scr[sl, glanes] = (kc * jnp.exp(b_mid - bc)).astype(BF16)
            kst_scr[sl, xcols(gi, c)] = (kc * jnp.exp(b_last - bc)).astype(BF16)
            dec_scr[c:c + 1, glanes] = jnp.exp(b_last)
            for h, m in enumerate(masks):
                head = gi * n_sub + h
                qt_scr[sl, head * LANE:(head + 1) * LANE] = (qt if m is None else qt * m).astype(BF16)
                qin_scr[sl, xcols(head, c)] = (q_in if m is None else q_in * m).astype(BF16)

    eye = jnp.where(lax.broadcasted_iota(jnp.int32, (LANE, LANE), 0)
                    == lax.broadcasted_iota(jnp.int32, (LANE, LANE), 1), 1.0, 0.0).astype(BF16)
    for gi in range(n_groups):
        glanes = slice(gi * LANE, (gi + 1) * LANE)
        kt = kt_scr[:, glanes]
        kst_x = kst_scr[:, gi * xw:(gi + 1) * xw]
        o_intra, upd = [], None
        for h, m in enumerate(masks):
            hl = slice((gi * n_sub + h) * LANE, (gi * n_sub + h + 1) * LANE)
            vb = v_ref[:, hl]
            sc = jnp.where(causal, _dot_nt(qt_scr[:, hl], kt), 0.0).astype(BF16)
            o_intra.append(_dot(sc, vb))
            v_t = _dot_nt(eye, vb).astype(BF16)
            u = _dot(v_t, kst_x)
            if m is not None:
                u = u * jnp.concatenate([m] * n_chunks, axis=1)
            upd = u if upd is None else upd + u
        st = st_scr[gi]
        starts = []
        for c in range(n_chunks):
            starts.append(st.astype(BF16))
            st = st * dec_scr[c:c + 1, glanes] + upd[:, c * LANE:(c + 1) * LANE]
        st_scr[gi] = st
        st_x = starts[0] if n_chunks == 1 else jnp.concatenate(starts, axis=1)
        for h in range(n_sub):
            head = gi * n_sub + h
            hl = slice(head * LANE, (head + 1) * LANE)
            o_inter = _dot_nt(qin_scr[:, head * xw:(head + 1) * xw], st_x)
            inter_scr[:, hl] = o_inter
            o_scr[:, hl] = o_intra[h] + o_inter

    for head in range(n_heads):
        finish(slice(0, tb), head, o_scr[:, head * dv:(head + 1) * dv])

    @pl.when(jnp.max(span_vec) > SAFE_DECAY_SPAN)
    def _():
        tri = tri_blk[0:length, 0:length]

        def chunk(c, carry):
            rows = pl.ds(pl.multiple_of(c * length, length), length)
            for gi in range(n_groups):
                q, k, g, vs = load_group(rows, gi)
                intra = _glr_intra_exact(q, k, g, vs, masks, tmp_scr, tri)
                for h in range(n_sub):
                    head = gi * n_sub + h
                    finish(rows, head, intra[h] + inter_scr[rows, head * dv:(head + 1) * dv])
            return carry

        lax.fori_loop(0, n_chunks, chunk, 0)

    @pl.when(t == pl.num_programs(1) - 1)
    def _():
        for gi in range(n_groups):
            sout_ref[gi] = st_scr[gi].T


def _time_block(kind, t_len):
    tb = _pick_tile(t_len, 256)
    length = min(CHUNK, t_len) if kind == "hgrn" else tb
    assert tb % length == 0
    return length, tb


def _glr(kind, layer, bsz, t_len, ins, s0):
    length, tb = _time_block(kind, t_len)
    n_groups, n_sub = (4, 1) if kind == "hgrn" else (2, 2)
    n_heads = n_groups * n_sub
    bw = n_heads * LANE
    has_state = s0 is not None

    def tok(width, blk):
        return pl.BlockSpec((None, tb, width), lambda b, t: (b, t, blk))

    def full(shape):
        return pl.BlockSpec(shape, lambda b, t: (0,) * len(shape))

    if kind == "hgrn":
        a, f, lbl, nw = ins
        a = a.reshape(bsz, t_len, -1)
        f = f.reshape(bsz, t_len, -1)
        args = [a, a, a, f, lbl, nw]
        in_specs = [tok(bw, 0), tok(bw, 1), tok(bw, 2), tok(bw, 0), full(lbl.shape),
                    pl.BlockSpec((None, 1, bw), lambda b, t: (layer, 0, 0))]
    else:
        c, s, w2, ba, nw = ins
        c = c.reshape(bsz, t_len, -1)
        s = s.reshape(bsz, t_len, -1)
        kw = n_groups * LANE
        args = [c, c, c, c, s, w2, ba, nw]
        in_specs = [tok(kw, 0), tok(kw, 1), tok(bw, 1), tok(bw, 2), tok(LANE, 0),
                    pl.BlockSpec((None, LANE, kw), lambda b, t: (layer, 0, 0)),
                    pl.BlockSpec((None, 1, kw), lambda b, t: (layer, 0, 0)),
                    pl.BlockSpec((None, 1, bw), lambda b, t: (layer, 0, 0))]
    st_spec = pl.BlockSpec((None, n_groups, LANE, LANE), lambda b, t: (b, 0, 0, 0))
    if has_state:
        args.append(s0)
        in_specs.append(st_spec)
    o, s_new = pl.pallas_call(
        functools.partial(_glr_kernel, kind=kind, layer=layer, length=length,
                          n_chunks=tb // length, n_groups=n_groups, n_sub=n_sub,
                          has_state=has_state),
        grid=(bsz, t_len // tb),
        in_specs=in_specs,
        out_specs=[pl.BlockSpec((None, tb, bw), lambda b, t: (b, t, 0)), st_spec],
        out_shape=[jax.ShapeDtypeStruct((bsz, t_len, bw), BF16),
                   jax.ShapeDtypeStruct((bsz, n_groups, LANE, LANE), F32)],
        scratch_shapes=[pltpu.VMEM((n_groups, LANE, LANE), F32),
                        pltpu.VMEM((tb, bw), BF16),
                        pltpu.VMEM((tb, bw * (tb // length)), BF16),
                        pltpu.VMEM((tb, n_groups * LANE), BF16),
                        pltpu.VMEM((tb, n_groups * LANE * (tb // length)), BF16),
                        pltpu.VMEM((max(tb // length, 8), n_groups * LANE), F32),
                        pltpu.VMEM((tb, bw), F32),
                        pltpu.VMEM((tb, bw), F32),
                        pltpu.VMEM((2 + n_sub, length, LANE), F32)],
        compiler_params=_params(("parallel", "arbitrary")),
        name=kind,
    )(*args)
    return o.reshape(bsz * t_len, bw), s_new


def _mlstm_kernel(*refs, length, n_heads, has_state):
    q_ref, k_ref, v_ref, og_ref, s_ref, bias_ref, nw_ref = refs[:7]
    refs = refs[7:]
    if has_state:
        c0_ref, n0_ref, m0_ref = refs[:3]
        refs = refs[3:]
    o_ref, cout_ref, nout_ref, mout_ref, cn_scr, m_scr = refs
    t = pl.program_id(1)
    dk = dv = LANE
    scale = float(dk) ** -0.5
    lane = lax.broadcasted_iota(jnp.int32, (1, LANE), 1)

    @pl.when(t == 0)
    def _():
        for h in range(n_heads):
            if has_state:
                n_cols = jnp.broadcast_to(n0_ref[h:h + 1, :], (dv, dk)).T
                cn_scr[h] = jnp.concatenate([c0_ref[h], n_cols], axis=1)
                m_scr[h:h + 1, :] = jnp.broadcast_to(m0_ref[:, h:h + 1], (1, LANE))
            else:
                cn_scr[h] = jnp.zeros((dk, 2 * dv), F32)
                m_scr[h:h + 1, :] = jnp.zeros((1, LANE), F32)

    causal, tri = _chunk_consts(length)
    sel = jnp.where(lax.broadcasted_iota(jnp.int32, (8, LANE), 0)
                    == lax.broadcasted_iota(jnp.int32, (8, LANE), 1), 1.0, 0.0).astype(BF16)
    reps = -(-length // LANE)

    x = s_ref[...] + bias_ref[...]
    y = jnp.where(lane < n_heads, x, _cumsum_rows(tri, _log_sigmoid(x)))
    y_t = _rows_to_lanes(sel, y)
    ones = jnp.ones((length, dv), BF16)
    m_row = jnp.zeros((1, LANE), F32)

    for h in range(n_heads):
        lanes = slice(h * dk, (h + 1) * dk)
        q = q_ref[:, lanes]
        kf = k_ref[:, lanes].astype(F32) * scale
        v_ext = jnp.concatenate([v_ref[:, lanes], ones], axis=1)
        i_c = jnp.broadcast_to(y[:, h:h + 1], (length, LANE))
        b_c = jnp.broadcast_to(y[:, n_heads + h:n_heads + h + 1], (length, LANE))
        z_r = y_t[h:h + 1, :] - y_t[n_heads + h:n_heads + h + 1, :]
        b_last = b_c[length - 1:length, :]
        b_cw = jnp.concatenate([b_c] * reps, axis=1)[:, :length]

        d_intra = jnp.where(causal, b_cw + z_r, -jnp.inf)
        a_t = jnp.max(d_intra, axis=1, keepdims=True)
        p = jnp.exp(d_intra - a_t) * _dot_nt(q, kf.astype(BF16))
        num_ext = _dot(p.astype(BF16), v_ext)
        e_intra = b_last - b_c + i_c
        e_max = jnp.max(e_intra, axis=0, keepdims=True)
        kw = jnp.exp(e_intra - e_max) * kf
        u_ext = _dot_tn(kw.astype(BF16), v_ext)

        m_old = m_scr[h:h + 1, :]
        cn_old = cn_scr[h]
        d_inter = b_c + m_old
        m_t = jnp.maximum(d_inter, a_t)
        f_intra = jnp.exp(a_t - m_t)
        w_inter = jnp.exp(d_inter - m_t)
        qc_ext = _dot(q, cn_old.astype(BF16))
        tot = (jnp.concatenate([f_intra, f_intra], axis=1) * num_ext
               + jnp.concatenate([w_inter, w_inter], axis=1) * qc_ext)
        hid = tot[:, :dv] / jnp.maximum(jnp.abs(tot[:, dv:]), jnp.exp(-m_t))

        e_inter = b_last + m_old
        m_new = jnp.maximum(e_inter, e_max)
        w_c = jnp.exp(e_inter - m_new)
        f_s = jnp.exp(e_max - m_new)
        cn_scr[h] = (jnp.concatenate([w_c, w_c], axis=1) * cn_old
                     + jnp.concatenate([f_s, f_s], axis=1) * u_ext)
        m_scr[h:h + 1, :] = m_new
        m_row = jnp.where(lane == h, m_new, m_row)

        og = og_ref[:, lanes].astype(F32)
        y_out = _sigmoid(og) * _head_norm(hid, nw_ref[:, lanes])
        o_ref[:, lanes] = y_out.astype(o_ref.dtype)

    @pl.when(t == pl.num_programs(1) - 1)
    def _():
        for h in range(n_heads):
            cn = cn_scr[h]
            cout_ref[h] = cn[:, :dv]
            nout_ref[h:h + 1, :] = cn[:, dv:].T[0:1, :]
        mout_ref[...] = m_row


def _mlstm_kernel_old(*refs, length, n_chunks, n_heads, has_state):
    q_ref, k_ref, v_ref, og_ref, s_ref, bias_ref, nw_ref = refs[:7]
    refs = refs[7:]
    if has_state:
        c0_ref, n0_ref, m0_ref = refs[:3]
        refs = refs[3:]
    o_ref, cout_ref, nout_ref, mout_ref, c_scr, n_scr, m_scr, o_scr = refs
    t = pl.program_id(1)
    dk = LANE
    scale = float(dk) ** -0.5

    @pl.when(t == 0)
    def _():
        if has_state:
            c_scr[...] = c0_ref[...]
            n_scr[...] = n0_ref[...]
            m_scr[...] = m0_ref[...]
        else:
            c_scr[...] = jnp.zeros_like(c_scr)
            n_scr[...] = jnp.zeros_like(n_scr)
            m_scr[...] = jnp.zeros_like(m_scr)

    causal, tri_blk = _chunk_consts(length, n_chunks)
    sel = jnp.where(lax.broadcasted_iota(jnp.int32, (8, LANE), 0)
                    == lax.broadcasted_iota(jnp.int32, (8, LANE), 1), 1.0, 0.0).astype(BF16)

    x = s_ref[...] + bias_ref[...]
    bc_all = _cumsum_rows(tri_blk, _log_sigmoid(x))
    m_all = m_scr[...]
    c_st = [c_scr[h] for h in range(n_heads)]
    n_st = [n_scr[h:h + 1, :] for h in range(n_heads)]
    m_st = [m_all[:, h:h + 1] for h in range(n_heads)]

    for c in range(n_chunks):
        sl = slice(c * length, (c + 1) * length)
        bt_all = _rows_to_lanes(sel, bc_all[sl])
        it_all = _rows_to_lanes(sel, x[sl])
        for h in range(n_heads):
            lanes = slice(h * dk, (h + 1) * dk)
            q = q_ref[sl, lanes]
            kf = k_ref[sl, lanes].astype(F32) * scale
            vb = v_ref[sl, lanes]
            b_c = bc_all[sl, n_heads + h:n_heads + h + 1]
            i_c = x[sl, h:h + 1]
            b_r = bt_all[n_heads + h:n_heads + h + 1, :]
            i_r = it_all[h:h + 1, :]
            b_last = b_c[length - 1:length, :]

            d_intra = jnp.where(causal, b_c - b_r + i_r, -jnp.inf)
            a_t = jnp.max(d_intra, axis=1, keepdims=True)
            p = jnp.exp(d_intra - a_t) * _dot_nt(q, kf.astype(BF16))
            num_pre = _dot(p.astype(BF16), vb)
            sum_pre = jnp.sum(p, axis=1, keepdims=True)
            e_intra = b_last - b_c + i_c
            e_max = jnp.max(e_intra, axis=0, keepdims=True)
            kw = jnp.exp(e_intra - e_max) * kf
            u_pre = _dot_tn(kw.astype(BF16), vb)
            nk_pre = jnp.sum(kw, axis=0, keepdims=True)

            m_old, c_old, n_old = m_st[h], c_st[h], n_st[h]
            d_inter = b_c + m_old
            m_t = jnp.maximum(d_inter, a_t)
            f_intra = jnp.exp(a_t - m_t)
            w_inter = jnp.exp(d_inter - m_t)
            num = f_intra * num_pre + w_inter * _dot(q, c_old.astype(BF16))
            qn = jnp.sum(q.astype(F32) * n_old, axis=1, keepdims=True)
            den = f_intra * sum_pre + w_inter * qn
            o_scr[sl, lanes] = num / jnp.maximum(jnp.abs(den), jnp.exp(-m_t))

            e_inter = b_last + m_old
            m_new = jnp.maximum(e_inter, e_max)
            w_c = jnp.exp(e_inter - m_new)
            f_s = jnp.exp(e_max - m_new)
            c_st[h] = w_c * c_old + f_s * u_pre
            n_st[h] = w_c * n_old + f_s * nk_pre
            m_st[h] = m_new

    lane = lax.broadcasted_iota(jnp.int32, (1, LANE), 1)
    m_row = jnp.zeros((1, LANE), F32)
    for h in range(n_heads):
        lanes = slice(h * dk, (h + 1) * dk)
        c_scr[h] = c_st[h]
        n_scr[h:h + 1, :] = n_st[h]
        m_row = jnp.where(lane == h, m_st[h], m_row)
        og = og_ref[:, lanes].astype(F32)
        y = _sigmoid(og) * _head_norm(o_scr[:, lanes], nw_ref[:, lanes])
        o_ref[:, lanes] = y.astype(o_ref.dtype)
    m_scr[...] = m_row

    @pl.when(t == pl.num_programs(1) - 1)
    def _():
        cout_ref[...] = c_scr[...]
        nout_ref[...] = n_scr[...]
        mout_ref[...] = m_scr[...]


def _mlstm(layer, bsz, t_len, p_b, p_s, bias_row, norm_w, states):
    length = tb = _pick_tile(t_len, MLSTM_CHUNK)
    n_heads = p_b.shape[1] // (4 * LANE)
    bw = n_heads * LANE
    has_state = states is not None
    p_b = p_b.reshape(bsz, t_len, -1)
    p_s = p_s.reshape(bsz, t_len, -1)

    def tok(width, blk):
        return pl.BlockSpec((None, tb, width), lambda b, t: (b, t, blk))

    c_spec = pl.BlockSpec((None, n_heads, LANE, LANE), lambda b, t: (b, 0, 0, 0))
    n_spec = pl.BlockSpec((None, n_heads, LANE), lambda b, t: (b, 0, 0))
    m_spec = pl.BlockSpec((None, 1, LANE), lambda b, t: (b, 0, 0))
    args = [p_b, p_b, p_b, p_b, p_s, bias_row, norm_w]
    in_specs = [tok(bw, 0), tok(bw, 1), tok(bw, 2), tok(bw, 3), tok(LANE, 0),
                pl.BlockSpec((None, 1, LANE), lambda b, t: (layer, 0, 0)),
                pl.BlockSpec((None, 1, bw), lambda b, t: (layer, 0, 0))]
    if has_state:
        args += list(states)
        in_specs += [c_spec, n_spec, m_spec]
    o, c_new, n_new, m_new = pl.pallas_call(
        functools.partial(_mlstm_kernel, length=length, n_heads=n_heads, has_state=has_state),
        grid=(bsz, t_len // tb),
        in_specs=in_specs,
        out_specs=[pl.BlockSpec((None, tb, bw), lambda b, t: (b, t, 0)), c_spec, n_spec, m_spec],
        out_shape=[jax.ShapeDtypeStruct((bsz, t_len, bw), BF16),
                   jax.ShapeDtypeStruct((bsz, n_heads, LANE, LANE), F32),
                   jax.ShapeDtypeStruct((bsz, n_heads, LANE), F32),
                   jax.ShapeDtypeStruct((bsz, 1, LANE), F32)],
        scratch_shapes=[pltpu.VMEM((n_heads, LANE, 2 * LANE), F32),
                        pltpu.VMEM((n_heads, LANE), F32)],
        compiler_params=_params(("parallel", "arbitrary")),
        name="mlstm",
    )(*args)
    return o.reshape(bsz * t_len, bw), c_new, n_new, m_new


def _prep_weights(w_in, gla_w_a2, mlstm_gate_bias, heads):
    (ah, adk, adv), (bh, bdk, bdv), (ch, cdk, cdv), rank = heads
    sizes = (ah * adk, ah * adk, ah * adv, ah * adv,
             bh * bdk, bh * bdk, bh * bdv, bh * bdv, 2 * bh,
             ch * cdk, ch * cdk, ch * cdv, ch * cdv, rank)
    offs = [0]
    for s in sizes:
        offs.append(offs[-1] + s)
    col = lambda i: w_in[:, :, offs[i]:offs[i + 1]]
    (a_q, a_f, a_i, a_g, b_q, b_k, b_v, b_o, b_if, c_q, c_k, c_v, c_g, c_lr) = [col(i) for i in range(14)]
    pad = jnp.zeros(w_in.shape[:2] + (LANE - 2 * bh - rank,), w_in.dtype)
    groups = ([a_q, a_i, a_g], [a_f], [b_q, b_k, b_v, b_o], [c_q, c_k, c_v, c_g], [b_if, c_lr, pad])
    widths = tuple(sum(g.shape[-1] for g in grp) for grp in groups)
    w_cat = jnp.concatenate([g for grp in groups for g in grp], axis=-1).astype(BF16)
    w_mg = w_in[:, :, offs[14]:].astype(BF16)
    depth = w_in.shape[0]
    w2_pad = jnp.zeros((depth, LANE, gla_w_a2.shape[-1]), F32)
    w2_pad = w2_pad.at[:, 2 * bh:2 * bh + rank, :].set(gla_w_a2).astype(BF16)
    bias_row = jnp.zeros((depth, 1, LANE), F32).at[:, 0, :2 * bh].set(mlstm_gate_bias)
    return w_cat, widths, w_mg, w2_pad, bias_row


def _run_trunk(x, states, wts, heads):
    (ffn1_norm, ffn1_up, ffn1_down, mix_norm, w_cat, widths, w_mg, hgrn_lb_logits, hgrn_norm,
     bias_row, mlstm_norm, w2_pad, gla_b_a, gla_norm, w_branch, w_out,
     ffn2_norm, ffn2_up, ffn2_down, final_norm) = wts
    (ah, adk, adv), (bh, bdk, bdv), (ch, cdk, cdv), _ = heads
    bsz, t_len, d = x.shape
    depth = ffn1_norm.shape[0]
    x = x.reshape(bsz * t_len, d)
    new_states = ([], [], [], [], [])
    for l in range(depth):
        x = _ffn(x, ffn1_norm, ffn1_up, ffn1_down, l)
        p_a, p_f, p_b, p_c, p_s = _inproj(x, mix_norm, w_cat, l, widths)
        if states is None:
            st_a = st_b = st_c = None
        else:
            s_hgrn, c_ml, n_ml, m_ml, s_gla = states
            st_a = s_hgrn[l]
            m_pad = jnp.zeros((bsz, 1, LANE), F32).at[:, 0, :bh].set(m_ml[l])
            st_b = (c_ml[l], n_ml[l], m_pad)
            st_c = s_gla[l].reshape(bsz, ch // 2, 2 * cdk, cdv)
        o_a, s_a = _glr("hgrn", l, bsz, t_len, (p_a, p_f, hgrn_lb_logits, hgrn_norm), st_a)
        o_b, c_new, n_new, m_new = _mlstm(l, bsz, t_len, p_b, p_s, bias_row, mlstm_norm, st_b)
        o_c, s_c = _glr("gla", l, bsz, t_len, (p_c, p_s, w2_pad, gla_b_a, gla_norm), st_c)
        x = _merge(x, mix_norm, o_a, o_b, o_c, w_mg, w_branch, w_out, l)
        x = _ffn(x, ffn2_norm, ffn2_up, ffn2_down, l,
                 final_w=final_norm if l == depth - 1 else None)
        for acc, s in zip(new_states, (s_a, c_new, n_new, m_new[:, 0, :bh],
                                       s_c.reshape(bsz, ch, cdk, cdv))):
            acc.append(s)
    return x.reshape(bsz, t_len, d), tuple(jnp.stack(acc) for acc in new_states)


def kernel(x_prompt, x_sample, state_hgrn, state_mlstm_c, state_mlstm_n, state_mlstm_m, state_gla, ffn1_norm, ffn1_w_up, ffn1_w_down, mix_norm, w_in, hgrn_lb_logits, hgrn_norm, mlstm_gate_bias, mlstm_norm, gla_w_a2, gla_b_a, gla_norm, w_branch, w_out, ffn2_norm, ffn2_w_up, ffn2_w_down, final_norm):
    heads = (state_hgrn.shape[2:], state_mlstm_c.shape[2:], state_gla.shape[2:], gla_w_a2.shape[1])
    assert heads[0][1:] == (LANE, LANE) and heads[1][1:] == (LANE, LANE)
    assert heads[2][1:] == (LANE // 2, LANE) and heads[2][0] % 2 == 0
    w_cat, widths, w_mg, w2_pad, bias_row = _prep_weights(w_in, gla_w_a2, mlstm_gate_bias, heads)
    row = lambda a: a[:, None, :]
    wts = (row(ffn1_norm), ffn1_w_up.astype(BF16), ffn1_w_down.astype(BF16), row(mix_norm),
           w_cat, widths, w_mg, hgrn_lb_logits, row(hgrn_norm), bias_row, row(mlstm_norm),
           w2_pad, row(gla_b_a), row(gla_norm), w_branch.astype(BF16), w_out.astype(BF16),
           row(ffn2_norm), ffn2_w_up.astype(BF16), ffn2_w_down.astype(BF16), final_norm[None, :])
    y_p, st_p = _run_trunk(x_prompt, None, wts, heads)
    sample_states = (state_hgrn, state_mlstm_c, state_mlstm_n, state_mlstm_m, state_gla)
    y_s, st_s = _run_trunk(x_sample, sample_states, wts, heads)
    return (y_p, y_s) + st_p + st_s
```

```python
import functools

import jax
import jax.numpy as jnp
from jax import lax
from jax.experimental import pallas as pl
from jax.experimental.pallas import tpu as pltpu

F32 = jnp.float32
BF16 = jnp.bfloat16

EPS = 1e-6
CHUNK = 64
MLSTM_CHUNK = 256
GLA_GATE_TEMP = 16.0
N_BRANCH = 3
LANE = 128
V7X_VMEM_BYTES = 64 * 1024 * 1024
VMEM_LIMIT = V7X_VMEM_BYTES - 8 * 1024 * 1024
SAFE_DECAY_SPAN = 60.0
LOG2E = 1.4426950408889634

_NT = (((1,), (1,)), ((), ()))
_TN = (((0,), (0,)), ((), ()))


def _dot(a, b):
    return jnp.dot(a, b, preferred_element_type=F32)


def _dot_nt(a, b):
    return lax.dot_general(a, b, _NT, preferred_element_type=F32)


def _dot_tn(a, b):
    return lax.dot_general(a, b, _TN, preferred_element_type=F32)


def _rms(x, w):
    return x * lax.rsqrt(jnp.mean(x * x, axis=-1, keepdims=True) + EPS) * w


def _sigmoid(x):
    return 1.0 / (1.0 + jnp.exp(-x))


def _log_sigmoid(x):
    return jnp.minimum(x, 0.0) - jnp.log(1.0 + jnp.exp(-jnp.abs(x)))


def _split3(x):
    hi = x.astype(BF16)
    r = x - hi.astype(F32)
    mid = r.astype(BF16)
    lo = (r - mid.astype(F32)).astype(BF16)
    return hi, mid, lo


def _cumsum_rows(tri, g):
    hi, mid, lo = _split3(g)
    return (_dot(tri, hi) + _dot(tri, mid)) + _dot(tri, lo)


def _rows_to_lanes(sel, x):
    hi, mid, lo = _split3(x)
    return (_dot_nt(sel, hi) + _dot_nt(sel, mid)) + _dot_nt(sel, lo)


def _pick_tile(n, cap):
    t = min(n, cap)
    while n % t or (t % 8 and t != n):
        t -= 1
    return t


def _params(sem):
    return pltpu.CompilerParams(dimension_semantics=sem, vmem_limit_bytes=VMEM_LIMIT)


def _ffn_kernel(x_ref, nw_ref, wup_ref, wd_ref, *rest, d_ff, tf, tn, final):
    if final:
        fw_ref, o_ref, act_scr = rest
    else:
        o_ref, act_scr = rest
    d = x_ref.shape[1]
    h = _rms(x_ref[...], nw_ref[...]).astype(BF16)
    for c in range(0, d_ff, tf):
        g = _dot(h, wup_ref[:, c:c + tf])
        u = _dot(h, wup_ref[:, d_ff + c:d_ff + c + tf])
        act_scr[:, c:c + tf] = (g * _sigmoid(g) * u).astype(BF16)
    for c in range(0, d, tn):
        o_ref[:, c:c + tn] = x_ref[:, c:c + tn] + 0.5 * _dot(act_scr[...], wd_ref[:, c:c + tn])
    if final:
        o_ref[...] = _rms(o_ref[...], fw_ref[...])


def _ffn(x, norm_w, w_up, w_down, layer, final_w=None):
    n, d = x.shape
    d_ff = w_down.shape[1]
    tm = _pick_tile(n, 1024)
    tf = tn = 256
    assert d_ff % tf == 0 and d % tn == 0
    final = final_w is not None
    resident = pl.Buffered(1)
    in_specs = [
        pl.BlockSpec((tm, d), lambda i: (i, 0)),
        pl.BlockSpec((None, 1, d), lambda i: (layer, 0, 0)),
        pl.BlockSpec((None, d, 2 * d_ff), lambda i: (layer, 0, 0), pipeline_mode=resident),
        pl.BlockSpec((None, d_ff, d), lambda i: (layer, 0, 0), pipeline_mode=resident),
    ]
    args = [x, norm_w, w_up, w_down]
    if final:
        in_specs.append(pl.BlockSpec((1, d), lambda i: (0, 0)))
        args.append(final_w)
    return pl.pallas_call(
        functools.partial(_ffn_kernel, d_ff=d_ff, tf=tf, tn=tn, final=final),
        grid=(n // tm,),
        in_specs=in_specs,
        out_specs=pl.BlockSpec((tm, d), lambda i: (i, 0)),
        out_shape=jax.ShapeDtypeStruct((n, d), F32),
        scratch_shapes=[pltpu.VMEM((tm, d_ff), BF16)],
        compiler_params=_params(("parallel",)),
        name="ffn",
    )(*args)


def _inproj_kernel(x_ref, nw_ref, w_ref, oa_ref, of_ref, ob_ref, oc_ref, os_ref, *, widths):
    h = _rms(x_ref[...], nw_ref[...]).astype(BF16)
    c0 = 0
    for ref, width in zip((oa_ref, of_ref, ob_ref, oc_ref, os_ref), widths):
        step = min(width, 512)
        for c in range(0, width, step):
            ref[:, c:c + step] = _dot(h, w_ref[:, c0 + c:c0 + c + step]).astype(ref.dtype)
        c0 += width


def _inproj(x, norm_w, w_cat, layer, widths):
    n, d = x.shape
    tm = _pick_tile(n, 1024)
    dtypes = (BF16, F32, BF16, BF16, F32)
    return pl.pallas_call(
        functools.partial(_inproj_kernel, widths=widths),
        grid=(n // tm,),
        in_specs=[
            pl.BlockSpec((tm, d), lambda i: (i, 0)),
            pl.BlockSpec((None, 1, d), lambda i: (layer, 0, 0)),
            pl.BlockSpec((None, d, sum(widths)), lambda i: (layer, 0, 0),
                         pipeline_mode=pl.Buffered(1)),
        ],
        out_specs=[pl.BlockSpec((tm, w), lambda i: (i, 0)) for w in widths],
        out_shape=[jax.ShapeDtypeStruct((n, w), dt) for w, dt in zip(widths, dtypes)],
        compiler_params=_params(("parallel",)),
        name="inproj",
    )(x, norm_w, w_cat)


def _merge_kernel(x_ref, nw_ref, oa_ref, ob_ref, oc_ref, wm_ref, wb_ref, wo_ref, o_ref):
    x = x_ref[...]
    d = x.shape[1]
    h = _rms(x, nw_ref[...]).astype(BF16)
    merged = None
    for nb, br_ref in enumerate((oa_ref, ob_ref, oc_ref)):
        gate = _sigmoid(_dot(h, wm_ref[:, nb * d:(nb + 1) * d]))
        term = gate * _dot(br_ref[...], wb_ref[nb])
        merged = term if merged is None else merged + term
    o_ref[...] = x + _dot(merged.astype(BF16), wo_ref[...])


def _merge(x, norm_w, o_a, o_b, o_c, w_mg, w_branch, w_out, layer):
    n, d = x.shape
    bw = o_a.shape[1]
    tm = _pick_tile(n, 1024)
    br_spec = pl.BlockSpec((tm, bw), lambda i: (i, 0))
    resident = pl.Buffered(1)
    return pl.pallas_call(
        _merge_kernel,
        grid=(n // tm,),
        in_specs=[
            pl.BlockSpec((tm, d), lambda i: (i, 0)),
            pl.BlockSpec((None, 1, d), lambda i: (layer, 0, 0)),
            br_spec, br_spec, br_spec,
            pl.BlockSpec((None, d, N_BRANCH * d), lambda i: (layer, 0, 0), pipeline_mode=resident),
            pl.BlockSpec((None, N_BRANCH, bw, d), lambda i: (layer, 0, 0, 0),
                         pipeline_mode=resident),
            pl.BlockSpec((None, d, d), lambda i: (layer, 0, 0), pipeline_mode=resident),
        ],
        out_specs=pl.BlockSpec((tm, d), lambda i: (i, 0)),
        out_shape=jax.ShapeDtypeStruct((n, d), F32),
        compiler_params=_params(("parallel",)),
        name="merge",
    )(x, norm_w, o_a, o_b, o_c, w_mg, w_branch, w_out)


def _chunk_consts(length, n_chunks=1):
    tb = length * n_chunks
    rb = lax.broadcasted_iota(jnp.int32, (tb, tb), 0)
    cb = lax.broadcasted_iota(jnp.int32, (tb, tb), 1)
    causal = rb >= cb
    for i in range(1, n_chunks):
        causal = causal & ((rb < i * length) | (cb >= i * length))
    return causal, jnp.where(causal, 1.0, 0.0).astype(BF16)


def _head_norm(o, w):
    return o * lax.rsqrt(jnp.mean(o * o, axis=-1, keepdims=True) + EPS) * w


def _glr_intra_exact(q, k, g, vs, masks, tmp_ref, tri):
    length = q.shape[0]
    b = _cumsum_rows(tri, g)
    tmp_ref[0, 0:length, :] = k
    tmp_ref[1, 0:length, :] = b
    for h, v in enumerate(vs):
        tmp_ref[2 + h, 0:length, :] = v
    t_idx = lax.broadcasted_iota(jnp.int32, (length, 1), 0)

    def body(s, accs):
        k_s = tmp_ref[0, pl.ds(s, 1), :]
        b_s = tmp_ref[1, pl.ds(s, 1), :]
        a = q * k_s * jnp.exp2(jnp.minimum(b - b_s, 0.0))
        new = []
        for h, m in enumerate(masks):
            ah = a if m is None else a * m
            col = jnp.where(t_idx >= s, jnp.sum(ah, axis=1, keepdims=True), 0.0)
            new.append(accs[h] + col * tmp_ref[2 + h, pl.ds(s, 1), :])
        return tuple(new)

    init = tuple(jnp.zeros((length, v.shape[1]), F32) for v in vs)
    return lax.fori_loop(0, length, body, init)


def _lane_masks(n_sub):
    if n_sub == 1:
        return [None]
    lane = lax.broadcasted_iota(jnp.int32, (1, LANE), 1)
    w = LANE // n_sub
    return [jnp.where((lane >= h * w) & (lane < (h + 1) * w), 1.0, 0.0) for h in range(n_sub)]


def _glr_kernel(*refs, kind, layer, length, n_chunks, n_groups, n_sub, has_state):
    if kind == "hgrn":
        q_ref, v_ref, og_ref, f_ref, lbl_ref, nw_ref = refs[:6]
        refs = refs[6:]
    else:
        q_ref, k_ref, v_ref, og_ref, s_ref, w2_ref, ba_ref, nw_ref = refs[:8]
        refs = refs[8:]
    if has_state:
        s0_ref, refs = refs[0], refs[1:]
    (o_ref, sout_ref, st_scr, qt_scr, qin_scr, kt_scr, kst_scr, dec_scr, p_scr, inter_scr,
     tmp_scr) = refs
    t = pl.program_id(1)
    n_heads = n_groups * n_sub
    dv = LANE
    tb = length * n_chunks

    xw = n_chunks * LANE

    def xcols(idx, c):
        return slice(idx * xw + c * LANE, idx * xw + (c + 1) * LANE)

    @pl.when(t == 0)
    def _():
        for gi in range(n_groups):
            if has_state:
                st_scr[gi] = s0_ref[gi].T
            else:
                st_scr[gi] = jnp.zeros((dv, LANE), F32)
        if n_chunks > 1:
            kst_scr[...] = jnp.zeros_like(kst_scr)

    causal, tri_blk = _chunk_consts(length, n_chunks)
    masks = _lane_masks(n_sub)

    if kind == "hgrn":
        logits = lbl_ref[...]
        e = jnp.exp(logits - jnp.max(logits, axis=0, keepdims=True))
        p = e / jnp.sum(e, axis=0, keepdims=True)
        lb = jnp.zeros((1, p.shape[1]), F32)
        for j in range(1, layer + 1):
            lb = lb + p[j:j + 1, :]

    def load_group(rows, gi, with_v=True):
        lanes = slice(gi * LANE, (gi + 1) * LANE)
        if kind == "hgrn":
            aq = q_ref[rows, lanes].astype(F32)
            q = aq * _sigmoid(aq)
            x = f_ref[rows, lanes]
            lbh = lb[:, lanes]
            forget = lbh + (1.0 - lbh) * _sigmoid(x)
            g = jnp.log(forget) * LOG2E
            k = 1.0 - forget
        else:
            q = q_ref[rows, lanes].astype(F32) * (float(LANE // n_sub) ** -0.5)
            k = k_ref[rows, lanes].astype(F32)
            pre = _dot(s_ref[rows, :].astype(BF16), w2_ref[:, lanes]) + ba_ref[:, lanes]
            g = _log_sigmoid(pre) * (LOG2E / GLA_GATE_TEMP)
        vs = [v_ref[rows, (gi * n_sub + h) * dv:(gi * n_sub + h + 1) * dv].astype(F32)
              for h in range(n_sub)] if with_v else None
        return q, k, g, vs

    def finish(rows, head, o):
        lanes = slice(head * dv, (head + 1) * dv)
        og = og_ref[rows, lanes].astype(F32)
        if kind == "hgrn":
            y = _head_norm(o * _sigmoid(og), nw_ref[:, lanes])
        else:
            y = _head_norm(o, nw_ref[:, lanes]) * (og * _sigmoid(og))
        o_ref[rows, lanes] = y.astype(o_ref.dtype)

    spans = []

    gates = [load_group(slice(0, tb), gi, with_v=False) for gi in range(n_groups)]
    b_all = _cumsum_rows(tri_blk, jnp.concatenate([g for _, _, g, _ in gates], axis=1))

    def phase1(gi):
        glanes = slice(gi * LANE, (gi + 1) * LANE)
        q, k, _, _ = gates[gi]
        b = b_all[:, glanes]
        for c in range(n_chunks):
            sl = slice(c * length, (c + 1) * length)
            bc, qc, kc = b[sl], q[sl], k[sl]
            b_mid = bc[length // 2:length // 2 + 1, :]
            b_last = bc[length - 1:length, :]
            spans.append(jnp.maximum(bc[0:1, :] - b_mid, b_mid - b_last))
            qt = qc * jnp.exp2(bc - b_mid)
            q_in = qc * jnp.exp2(bc)
            kt_scr[sl, glanes] = (kc * jnp.exp2(b_mid - bc)).astype(BF16)
            kst_scr[sl, xcols(gi, c)] = (kc * jnp.exp2(b_last - bc)).astype(BF16)
            dec_scr[c:c + 1, glanes] = jnp.exp2(b_last)
            for h, m in enumerate(masks):
                head = gi * n_sub + h
                qt_scr[sl, head * LANE:(head + 1) * LANE] = (qt if m is None else qt * m).astype(BF16)
                qin_scr[sl, head * LANE:(head + 1) * LANE] = (q_in if m is None else q_in * m).astype(BF16)

    starts = {}

    def phase2(gi):
        glanes = slice(gi * LANE, (gi + 1) * LANE)
        kt = kt_scr[:, glanes]
        kst_x = kst_scr[:, gi * xw:(gi + 1) * xw]
        upd = None
        for h, m in enumerate(masks):
            head = gi * n_sub + h
            hl = slice(head * LANE, (head + 1) * LANE)
            p_scr[head] = jnp.where(causal, _dot_nt(qt_scr[:, hl], kt), 0.0).astype(BF16)
            u = _dot_tn(v_ref[:, hl], kst_x)
            if m is not None:
                u = u * jnp.concatenate([m] * n_chunks, axis=1)
            upd = u if upd is None else upd + u
        st = st_scr[gi]
        starts[gi] = []
        for c in range(n_chunks):
            starts[gi].append(st.astype(BF16))
            st = st * dec_scr[c:c + 1, glanes] + upd[:, c * LANE:(c + 1) * LANE]
        st_scr[gi] = st

    def phase3(gi):
        for h in range(n_sub):
            head = gi * n_sub + h
            hl = slice(head * LANE, (head + 1) * LANE)
            o_intra = _dot(p_scr[head], v_ref[:, hl])
            parts = [_dot_nt(qin_scr[c * length:(c + 1) * length, hl], starts[gi][c])
                     for c in range(n_chunks)]
            o_inter = parts[0] if n_chunks == 1 else jnp.concatenate(parts, axis=0)
            inter_scr[:, hl] = o_inter
            finish(slice(0, tb), head, o_intra + o_inter)

    for gi in range(n_groups):
        phase1(gi)
        phase2(gi)
    for gi in range(n_groups):
        phase3(gi)
    span_vec = functools.reduce(jnp.maximum, spans)

    @pl.when(jnp.max(span_vec) > SAFE_DECAY_SPAN * LOG2E)
    def _():
        tri = tri_blk[0:length, 0:length]

        def chunk(c, carry):
            rows = pl.ds(pl.multiple_of(c * length, length), length)
            for gi in range(n_groups):
                q, k, g, vs = load_group(rows, gi)
                intra = _glr_intra_exact(q, k, g, vs, masks, tmp_scr, tri)
                for h in range(n_sub):
                    head = gi * n_sub + h
                    finish(rows, head, intra[h] + inter_scr[rows, head * dv:(head + 1) * dv])
            return carry

        lax.fori_loop(0, n_chunks, chunk, 0)

    @pl.when(t == pl.num_programs(1) - 1)
    def _():
        for gi in range(n_groups):
            sout_ref[gi] = st_scr[gi].T


def _time_block(kind, t_len):
    tb = _pick_tile(t_len, 256)
    length = min(CHUNK, t_len) if kind == "hgrn" else tb
    assert tb % length == 0
    return length, tb


def _glr(kind, layer, bsz, t_len, ins, s0):
    length, tb = _time_block(kind, t_len)
    n_groups, n_sub = (4, 1) if kind == "hgrn" else (2, 2)
    n_heads = n_groups * n_sub
    bw = n_heads * LANE
    has_state = s0 is not None

    def tok(width, blk):
        return pl.BlockSpec((None, tb, width), lambda b, t: (b, t, blk))

    def full(shape):
        return pl.BlockSpec(shape, lambda b, t: (0,) * len(shape))

    if kind == "hgrn":
        a, f, lbl, nw = ins
        a = a.reshape(bsz, t_len, -1)
        f = f.reshape(bsz, t_len, -1)
        args = [a, a, a, f, lbl, nw]
        in_specs = [tok(bw, 0), tok(bw, 1), tok(bw, 2), tok(bw, 0), full(lbl.shape),
                    pl.BlockSpec((None, 1, bw), lambda b, t: (layer, 0, 0))]
    else:
        c, s, w2, ba, nw = ins
        c = c.reshape(bsz, t_len, -1)
        s = s.reshape(bsz, t_len, -1)
        kw = n_groups * LANE
        args = [c, c, c, c, s, w2, ba, nw]
        in_specs = [tok(kw, 0), tok(kw, 1), tok(bw, 1), tok(bw, 2), tok(LANE, 0),
                    pl.BlockSpec((None, LANE, kw), lambda b, t: (layer, 0, 0)),
                    pl.BlockSpec((None, 1, kw), lambda b, t: (layer, 0, 0)),
                    pl.BlockSpec((None, 1, bw), lambda b, t: (layer, 0, 0))]
    st_spec = pl.BlockSpec((None, n_groups, LANE, LANE), lambda b, t: (b, 0, 0, 0))
    if has_state:
        args.append(s0)
        in_specs.append(st_spec)
    o, s_new = pl.pallas_call(
        functools.partial(_glr_kernel, kind=kind, layer=layer, length=length,
                          n_chunks=tb // length, n_groups=n_groups, n_sub=n_sub,
                          has_state=has_state),
        grid=(bsz, t_len // tb),
        in_specs=in_specs,
        out_specs=[pl.BlockSpec((None, tb, bw), lambda b, t: (b, t, 0)), st_spec],
        out_shape=[jax.ShapeDtypeStruct((bsz, t_len, bw), BF16),
                   jax.ShapeDtypeStruct((bsz, n_groups, LANE, LANE), F32)],
        scratch_shapes=[pltpu.VMEM((n_groups, LANE, LANE), F32),
                        pltpu.VMEM((tb, bw), BF16),
                        pltpu.VMEM((tb, bw), BF16),
                        pltpu.VMEM((tb, n_groups * LANE), BF16),
                        pltpu.VMEM((tb, n_groups * LANE * (tb // length)), BF16),
                        pltpu.VMEM((max(tb // length, 8), n_groups * LANE), F32),
                        pltpu.VMEM((n_heads, tb, tb), BF16),
                        pltpu.VMEM((tb, bw), F32),
                        pltpu.VMEM((2 + n_sub, length, LANE), F32)],
        compiler_params=_params(("parallel", "arbitrary")),
        name=kind,
    )(*args)
    return o.reshape(bsz * t_len, bw), s_new


def _mlstm_kernel(*refs, length, n_heads, has_state):
    q_ref, k_ref, v_ref, og_ref, s_ref, bias_ref, nw_ref = refs[:7]
    refs = refs[7:]
    if has_state:
        c0_ref, n0_ref, m0_ref = refs[:3]
        refs = refs[3:]
    o_ref, cout_ref, nout_ref, mout_ref, cn_scr, m_scr, s_scr = refs
    t = pl.program_id(1)
    dk = dv = LANE
    scale = float(dk) ** -0.5
    lane = lax.broadcasted_iota(jnp.int32, (1, LANE), 1)

    @pl.when(t == 0)
    def _():
        for h in range(n_heads):
            if has_state:
                n_cols = jnp.broadcast_to(n0_ref[h:h + 1, :], (dv, dk)).T
                cn_scr[h] = jnp.concatenate([c0_ref[h], n_cols], axis=1)
                m_scr[h:h + 1, :] = jnp.broadcast_to(m0_ref[:, h:h + 1], (1, LANE)) * LOG2E
            else:
                cn_scr[h] = jnp.zeros((dk, 2 * dv), F32)
                m_scr[h:h + 1, :] = jnp.zeros((1, LANE), F32)

    causal, tri = _chunk_consts(length)
    sel = jnp.where(lax.broadcasted_iota(jnp.int32, (8, LANE), 0)
                    == lax.broadcasted_iota(jnp.int32, (8, LANE), 1), 1.0, 0.0).astype(BF16)
    reps = -(-length // LANE)

    x = s_ref[...] + bias_ref[...]
    y = jnp.where(lane < n_heads, x, _cumsum_rows(tri, _log_sigmoid(x))) * LOG2E
    y_t = _rows_to_lanes(sel, y)
    ones = jnp.ones((length, dv), BF16)
    m_row = jnp.zeros((1, LANE), F32)

    def gate_cols(h):
        i_c = jnp.broadcast_to(y[:, h:h + 1], (length, LANE))
        b_c = jnp.broadcast_to(y[:, n_heads + h:n_heads + h + 1], (length, LANE))
        return i_c, b_c, b_c[length - 1:length, :]

    def intra_logits(h, b_c):
        z_r = y_t[h:h + 1, :] - y_t[n_heads + h:n_heads + h + 1, :]
        b_cw = jnp.concatenate([b_c] * reps, axis=1)[:, :length]
        return jnp.where(causal, b_cw + z_r, -jnp.inf)

    staged = []
    for h in range(n_heads):
        lanes = slice(h * dk, (h + 1) * dk)
        q = q_ref[:, lanes]
        kf = k_ref[:, lanes].astype(F32) * scale
        v_ext = jnp.concatenate([v_ref[:, lanes], ones], axis=1)
        i_c, b_c, b_last = gate_cols(h)
        s_scr[h] = _dot_nt(q, kf.astype(BF16))
        e_intra = b_last - b_c + i_c
        e_max = jnp.max(e_intra, axis=0, keepdims=True)
        kw = jnp.exp2(e_intra - e_max) * kf
        u_ext = _dot_tn(kw.astype(BF16), v_ext)
        m_old = m_scr[h:h + 1, :]
        cn_old = cn_scr[h]
        qc_ext = _dot(q, cn_old.astype(BF16))
        e_inter = b_last + m_old
        m_new = jnp.maximum(e_inter, e_max)
        w_c = jnp.exp2(e_inter - m_new)
        f_s = jnp.exp2(e_max - m_new)
        cn_scr[h] = (jnp.concatenate([w_c, w_c], axis=1) * cn_old
                     + jnp.concatenate([f_s, f_s], axis=1) * u_ext)
        m_scr[h:h + 1, :] = m_new
        m_row = jnp.where(lane == h, m_new, m_row)
        staged.append((v_ext, b_c + m_old, qc_ext))

    for h in range(n_heads):
        lanes = slice(h * dk, (h + 1) * dk)
        v_ext, d_inter, qc_ext = staged[h]
        d_intra = intra_logits(h, gate_cols(h)[1])
        a_t = jnp.max(d_intra, axis=1, keepdims=True)
        p = jnp.exp2(d_intra - a_t) * s_scr[h]
        num_ext = _dot(p.astype(BF16), v_ext)
        m_t = jnp.maximum(d_inter, a_t)
        f_intra = jnp.exp2(a_t - m_t)
        w_inter = jnp.exp2(d_inter - m_t)
        tot = (jnp.concatenate([f_intra, f_intra], axis=1) * num_ext
               + jnp.concatenate([w_inter, w_inter], axis=1) * qc_ext)
        hid = tot[:, :dv] / jnp.maximum(jnp.abs(tot[:, dv:]), jnp.exp2(-m_t))
        og = og_ref[:, lanes].astype(F32)
        y_out = _sigmoid(og) * _head_norm(hid, nw_ref[:, lanes])
        o_ref[:, lanes] = y_out.astype(o_ref.dtype)

    @pl.when(t == pl.num_programs(1) - 1)
    def _():
        for h in range(n_heads):
            cn = cn_scr[h]
            cout_ref[h] = cn[:, :dv]
            nout_ref[h:h + 1, :] = cn[:, dv:].T[0:1, :]
        mout_ref[...] = m_row * (1.0 / LOG2E)


def _mlstm(layer, bsz, t_len, p_b, p_s, bias_row, norm_w, states):
    length = tb = _pick_tile(t_len, MLSTM_CHUNK)
    n_heads = p_b.shape[1] // (4 * LANE)
    bw = n_heads * LANE
    has_state = states is not None
    p_b = p_b.reshape(bsz, t_len, -1)
    p_s = p_s.reshape(bsz, t_len, -1)

    def tok(width, blk):
        return pl.BlockSpec((None, tb, width), lambda b, t: (b, t, blk))

    c_spec = pl.BlockSpec((None, n_heads, LANE, LANE), lambda b, t: (b, 0, 0, 0))
    n_spec = pl.BlockSpec((None, n_heads, LANE), lambda b, t: (b, 0, 0))
    m_spec = pl.BlockSpec((None, 1, LANE), lambda b, t: (b, 0, 0))
    args = [p_b, p_b, p_b, p_b, p_s, bias_row, norm_w]
    in_specs = [tok(bw, 0), tok(bw, 1), tok(bw, 2), tok(bw, 3), tok(LANE, 0),
                pl.BlockSpec((None, 1, LANE), lambda b, t: (layer, 0, 0)),
                pl.BlockSpec((None, 1, bw), lambda b, t: (layer, 0, 0))]
    if has_state:
        args += list(states)
        in_specs += [c_spec, n_spec, m_spec]
    o, c_new, n_new, m_new = pl.pallas_call(
        functools.partial(_mlstm_kernel, length=length, n_heads=n_heads, has_state=has_state),
        grid=(bsz, t_len // tb),
        in_specs=in_specs,
        out_specs=[pl.BlockSpec((None, tb, bw), lambda b, t: (b, t, 0)), c_spec, n_spec, m_spec],
        out_shape=[jax.ShapeDtypeStruct((bsz, t_len, bw), BF16),
                   jax.ShapeDtypeStruct((bsz, n_heads, LANE, LANE), F32),
                   jax.ShapeDtypeStruct((bsz, n_heads, LANE), F32),
                   jax.ShapeDtypeStruct((bsz, 1, LANE), F32)],
        scratch_shapes=[pltpu.VMEM((n_heads, LANE, 2 * LANE), F32),
                        pltpu.VMEM((n_heads, LANE), F32),
                        pltpu.VMEM((n_heads, tb, tb), F32)],
        compiler_params=_params(("parallel", "arbitrary")),
        name="mlstm",
    )(*args)
    return o.reshape(bsz * t_len, bw), c_new, n_new, m_new


def _prep_weights(w_in, gla_w_a2, mlstm_gate_bias, heads):
    (ah, adk, adv), (bh, bdk, bdv), (ch, cdk, cdv), rank = heads
    sizes = (ah * adk, ah * adk, ah * adv, ah * adv,
             bh * bdk, bh * bdk, bh * bdv, bh * bdv, 2 * bh,
             ch * cdk, ch * cdk, ch * cdv, ch * cdv, rank)
    offs = [0]
    for s in sizes:
        offs.append(offs[-1] + s)
    col = lambda i: w_in[:, :, offs[i]:offs[i + 1]]
    (a_q, a_f, a_i, a_g, b_q, b_k, b_v, b_o, b_if, c_q, c_k, c_v, c_g, c_lr) = [col(i) for i in range(14)]
    pad = jnp.zeros(w_in.shape[:2] + (LANE - 2 * bh - rank,), w_in.dtype)
    groups = ([a_q, a_i, a_g], [a_f], [b_q, b_k, b_v, b_o], [c_q, c_k, c_v, c_g], [b_if, c_lr, pad])
    widths = tuple(sum(g.shape[-1] for g in grp) for grp in groups)
    w_cat = jnp.concatenate([g for grp in groups for g in grp], axis=-1).astype(BF16)
    w_mg = w_in[:, :, offs[14]:].astype(BF16)
    depth = w_in.shape[0]
    w2_pad = jnp.zeros((depth, LANE, gla_w_a2.shape[-1]), F32)
    w2_pad = w2_pad.at[:, 2 * bh:2 * bh + rank, :].set(gla_w_a2).astype(BF16)
    bias_row = jnp.zeros((depth, 1, LANE), F32).at[:, 0, :2 * bh].set(mlstm_gate_bias)
    return w_cat, widths, w_mg, w2_pad, bias_row


def _run_trunk(x, states, wts, heads):
    (ffn1_norm, ffn1_up, ffn1_down, mix_norm, w_cat, widths, w_mg, hgrn_lb_logits, hgrn_norm,
     bias_row, mlstm_norm, w2_pad, gla_b_a, gla_norm, w_branch, w_out,
     ffn2_norm, ffn2_up, ffn2_down, final_norm) = wts
    (ah, adk, adv), (bh, bdk, bdv), (ch, cdk, cdv), _ = heads
    bsz, t_len, d = x.shape
    depth = ffn1_norm.shape[0]
    x = x.reshape(bsz * t_len, d)
    new_states = ([], [], [], [], [])
    for l in range(depth):
        x = _ffn(x, ffn1_norm, ffn1_up, ffn1_down, l)
        p_a, p_f, p_b, p_c, p_s = _inproj(x, mix_norm, w_cat, l, widths)
        if states is None:
            st_a = st_b = st_c = None
        else:
            s_hgrn, c_ml, n_ml, m_ml, s_gla = states
            st_a = s_hgrn[l]
            m_pad = jnp.zeros((bsz, 1, LANE), F32).at[:, 0, :bh].set(m_ml[l])
            st_b = (c_ml[l], n_ml[l], m_pad)
            st_c = s_gla[l].reshape(bsz, ch // 2, 2 * cdk, cdv)
        o_a, s_a = _glr("hgrn", l, bsz, t_len, (p_a, p_f, hgrn_lb_logits, hgrn_norm), st_a)
        o_b, c_new, n_new, m_new = _mlstm(l, bsz, t_len, p_b, p_s, bias_row, mlstm_norm, st_b)
        o_c, s_c = _glr("gla", l, bsz, t_len, (p_c, p_s, w2_pad, gla_b_a, gla_norm), st_c)
        x = _merge(x, mix_norm, o_a, o_b, o_c, w_mg, w_branch, w_out, l)
        x = _ffn(x, ffn2_norm, ffn2_up, ffn2_down, l,
                 final_w=final_norm if l == depth - 1 else None)
        for acc, s in zip(new_states, (s_a, c_new, n_new, m_new[:, 0, :bh],
                                       s_c.reshape(bsz, ch, cdk, cdv))):
            acc.append(s)
    return x.reshape(bsz, t_len, d), tuple(jnp.stack(acc) for acc in new_states)


def kernel(x_prompt, x_sample, state_hgrn, state_mlstm_c, state_mlstm_n, state_mlstm_m, state_gla, ffn1_norm, ffn1_w_up, ffn1_w_down, mix_norm, w_in, hgrn_lb_logits, hgrn_norm, mlstm_gate_bias, mlstm_norm, gla_w_a2, gla_b_a, gla_norm, w_branch, w_out, ffn2_norm, ffn2_w_up, ffn2_w_down, final_norm):
    heads = (state_hgrn.shape[2:], state_mlstm_c.shape[2:], state_gla.shape[2:], gla_w_a2.shape[1])
    assert heads[0][1:] == (LANE, LANE) and heads[1][1:] == (LANE, LANE)
    assert heads[2][1:] == (LANE // 2, LANE) and heads[2][0] % 2 == 0
    w_cat, widths, w_mg, w2_pad, bias_row = _prep_weights(w_in, gla_w_a2, mlstm_gate_bias, heads)
    row = lambda a: a[:, None, :]
    wts = (row(ffn1_norm), ffn1_w_up.astype(BF16), ffn1_w_down.astype(BF16), row(mix_norm),
           w_cat, widths, w_mg, hgrn_lb_logits, row(hgrn_norm), bias_row, row(mlstm_norm),
           w2_pad, row(gla_b_a), row(gla_norm), w_branch.astype(BF16), w_out.astype(BF16),
           row(ffn2_norm), ffn2_w_up.astype(BF16), ffn2_w_down.astype(BF16), final_norm[None, :])
    y_p, st_p = _run_trunk(x_prompt, None, wts, heads)
    sample_states = (state_hgrn, state_mlstm_c, state_mlstm_n, state_mlstm_m, state_gla)
    y_s, st_s = _run_trunk(x_sample, sample_states, wts, heads)
    return (y_p, y_s) + st_p + st_s
```

```python
import functools

import jax
import jax.numpy as jnp
from jax import lax
from jax.experimental import pallas as pl
from jax.experimental.pallas import tpu as pltpu

F32 = jnp.float32
BF16 = jnp.bfloat16

EPS = 1e-6
CHUNK = 64
MLSTM_CHUNK = 256
GLA_GATE_TEMP = 16.0
N_BRANCH = 3
LANE = 128
V7X_VMEM_BYTES = 64 * 1024 * 1024
VMEM_LIMIT = V7X_VMEM_BYTES - 8 * 1024 * 1024
SAFE_DECAY_SPAN = 60.0
LOG2E = 1.4426950408889634

_NT = (((1,), (1,)), ((), ()))
_TN = (((0,), (0,)), ((), ()))


def _dot(a, b):
    return jnp.dot(a, b, preferred_element_type=F32)


def _dot_nt(a, b):
    return lax.dot_general(a, b, _NT, preferred_element_type=F32)


def _dot_tn(a, b):
    return lax.dot_general(a, b, _TN, preferred_element_type=F32)


def _rms(x, w):
    return x * lax.rsqrt(jnp.mean(x * x, axis=-1, keepdims=True) + EPS) * w


def _sigmoid(x):
    return 1.0 / (1.0 + jnp.exp2(x * (-LOG2E)))


def _log_sigmoid(x):
    return jnp.minimum(x, 0.0) - jnp.log(1.0 + jnp.exp(-jnp.abs(x)))


def _split3(x):
    hi = x.astype(BF16)
    r = x - hi.astype(F32)
    mid = r.astype(BF16)
    lo = (r - mid.astype(F32)).astype(BF16)
    return hi, mid, lo


def _cumsum_rows(tri, g):
    hi, mid, lo = _split3(g)
    return (_dot(tri, hi) + _dot(tri, mid)) + _dot(tri, lo)


def _rows_to_lanes(sel, x):
    hi, mid, lo = _split3(x)
    return (_dot_nt(sel, hi) + _dot_nt(sel, mid)) + _dot_nt(sel, lo)


def _pick_tile(n, cap):
    t = min(n, cap)
    while n % t or (t % 8 and t != n):
        t -= 1
    return t


def _row_parts(tm):
    n = 2 if tm % 512 == 0 and tm >= 1024 else 1
    return [slice(r * (tm // n), (r + 1) * (tm // n)) for r in range(n)]


def _params(sem):
    return pltpu.CompilerParams(dimension_semantics=sem, vmem_limit_bytes=VMEM_LIMIT)


def _ffn_kernel(x_ref, nw_ref, wup_ref, wd_ref, *rest, d_ff, tf, tn, final):
    if final:
        fw_ref, o_ref, act_scr = rest
    else:
        o_ref, act_scr = rest
    d = x_ref.shape[1]
    for rows in _row_parts(x_ref.shape[0]):
        h = _rms(x_ref[rows, :], nw_ref[...]).astype(BF16)
        for c in range(0, d_ff, tf):
            g = _dot(h, wup_ref[:, c:c + tf])
            u = _dot(h, wup_ref[:, d_ff + c:d_ff + c + tf])
            act_scr[rows, c:c + tf] = (g * _sigmoid(g) * u).astype(BF16)
        for c in range(0, d, tn):
            o_ref[rows, c:c + tn] = (x_ref[rows, c:c + tn]
                                     + 0.5 * _dot(act_scr[rows, :], wd_ref[:, c:c + tn]))
        if final:
            o_ref[rows, :] = _rms(o_ref[rows, :], fw_ref[...])


def _ffn(x, norm_w, w_up, w_down, layer, final_w=None):
    n, d = x.shape
    d_ff = w_down.shape[1]
    tm = _pick_tile(n, 1024)
    tf = tn = 256
    assert d_ff % tf == 0 and d % tn == 0
    final = final_w is not None
    resident = pl.Buffered(1)
    in_specs = [
        pl.BlockSpec((tm, d), lambda i: (i, 0)),
        pl.BlockSpec((None, 1, d), lambda i: (layer, 0, 0)),
        pl.BlockSpec((None, d, 2 * d_ff), lambda i: (layer, 0, 0), pipeline_mode=resident),
        pl.BlockSpec((None, d_ff, d), lambda i: (layer, 0, 0), pipeline_mode=resident),
    ]
    args = [x, norm_w, w_up, w_down]
    if final:
        in_specs.append(pl.BlockSpec((1, d), lambda i: (0, 0)))
        args.append(final_w)
    return pl.pallas_call(
        functools.partial(_ffn_kernel, d_ff=d_ff, tf=tf, tn=tn, final=final),
        grid=(n // tm,),
        in_specs=in_specs,
        out_specs=pl.BlockSpec((tm, d), lambda i: (i, 0)),
        out_shape=jax.ShapeDtypeStruct((n, d), F32),
        scratch_shapes=[pltpu.VMEM((tm, d_ff), BF16)],
        compiler_params=_params(("parallel",)),
        name="ffn",
    )(*args)


def _inproj_kernel(x_ref, nw_ref, w_ref, oa_ref, of_ref, ob_ref, oc_ref, os_ref, *, widths):
    for rows in _row_parts(x_ref.shape[0]):
        h = _rms(x_ref[rows, :], nw_ref[...]).astype(BF16)
        c0 = 0
        for ref, width in zip((oa_ref, of_ref, ob_ref, oc_ref, os_ref), widths):
            step = min(width, 512)
            for c in range(0, width, step):
                ref[rows, c:c + step] = _dot(h, w_ref[:, c0 + c:c0 + c + step]).astype(ref.dtype)
            c0 += width


def _inproj(x, norm_w, w_cat, layer, widths):
    n, d = x.shape
    tm = _pick_tile(n, 1024)
    dtypes = (BF16, F32, BF16, BF16, F32)
    return pl.pallas_call(
        functools.partial(_inproj_kernel, widths=widths),
        grid=(n // tm,),
        in_specs=[
            pl.BlockSpec((tm, d), lambda i: (i, 0)),
            pl.BlockSpec((None, 1, d), lambda i: (layer, 0, 0)),
            pl.BlockSpec((None, d, sum(widths)), lambda i: (layer, 0, 0),
                         pipeline_mode=pl.Buffered(1)),
        ],
        out_specs=[pl.BlockSpec((tm, w), lambda i: (i, 0)) for w in widths],
        out_shape=[jax.ShapeDtypeStruct((n, w), dt) for w, dt in zip(widths, dtypes)],
        compiler_params=_params(("parallel",)),
        name="inproj",
    )(x, norm_w, w_cat)


def _merge_kernel(x_ref, nw_ref, oa_ref, ob_ref, oc_ref, wm_ref, wb_ref, wo_ref, o_ref):
    d = x_ref.shape[1]
    for rows in _row_parts(x_ref.shape[0]):
        x = x_ref[rows, :]
        h = _rms(x, nw_ref[...]).astype(BF16)
        merged = None
        for nb, br_ref in enumerate((oa_ref, ob_ref, oc_ref)):
            gate = _sigmoid(_dot(h, wm_ref[:, nb * d:(nb + 1) * d]))
            term = gate * _dot(br_ref[rows, :], wb_ref[nb])
            merged = term if merged is None else merged + term
        o_ref[rows, :] = x + _dot(merged.astype(BF16), wo_ref[...])


def _merge(x, norm_w, o_a, o_b, o_c, w_mg, w_branch, w_out, layer):
    n, d = x.shape
    bw = o_a.shape[1]
    tm = _pick_tile(n, 1024)
    br_spec = pl.BlockSpec((tm, bw), lambda i: (i, 0))
    resident = pl.Buffered(1)
    return pl.pallas_call(
        _merge_kernel,
        grid=(n // tm,),
        in_specs=[
            pl.BlockSpec((tm, d), lambda i: (i, 0)),
            pl.BlockSpec((None, 1, d), lambda i: (layer, 0, 0)),
            br_spec, br_spec, br_spec,
            pl.BlockSpec((None, d, N_BRANCH * d), lambda i: (layer, 0, 0), pipeline_mode=resident),
            pl.BlockSpec((None, N_BRANCH, bw, d), lambda i: (layer, 0, 0, 0),
                         pipeline_mode=resident),
            pl.BlockSpec((None, d, d), lambda i: (layer, 0, 0), pipeline_mode=resident),
        ],
        out_specs=pl.BlockSpec((tm, d), lambda i: (i, 0)),
        out_shape=jax.ShapeDtypeStruct((n, d), F32),
        compiler_params=_params(("parallel",)),
        name="merge",
    )(x, norm_w, o_a, o_b, o_c, w_mg, w_branch, w_out)


def _chunk_consts(length, n_chunks=1):
    tb = length * n_chunks
    rb = lax.broadcasted_iota(jnp.int32, (tb, tb), 0)
    cb = lax.broadcasted_iota(jnp.int32, (tb, tb), 1)
    causal = rb >= cb
    for i in range(1, n_chunks):
        causal = causal & ((rb < i * length) | (cb >= i * length))
    return causal, jnp.where(causal, 1.0, 0.0).astype(BF16)


def _head_norm(o, w):
    return o * lax.rsqrt(jnp.mean(o * o, axis=-1, keepdims=True) + EPS) * w


def _glr_intra_exact(q, k, g, vs, masks, tmp_ref, tri):
    length = q.shape[0]
    b = _cumsum_rows(tri, g)
    tmp_ref[0, 0:length, :] = k
    tmp_ref[1, 0:length, :] = b
    for h, v in enumerate(vs):
        tmp_ref[2 + h, 0:length, :] = v
    t_idx = lax.broadcasted_iota(jnp.int32, (length, 1), 0)

    def body(s, accs):
        k_s = tmp_ref[0, pl.ds(s, 1), :]
        b_s = tmp_ref[1, pl.ds(s, 1), :]
        a = q * k_s * jnp.exp2(jnp.minimum(b - b_s, 0.0))
        new = []
        for h, m in enumerate(masks):
            ah = a if m is None else a * m
            col = jnp.where(t_idx >= s, jnp.sum(ah, axis=1, keepdims=True), 0.0)
            new.append(accs[h] + col * tmp_ref[2 + h, pl.ds(s, 1), :])
        return tuple(new)

    init = tuple(jnp.zeros((length, v.shape[1]), F32) for v in vs)
    return lax.fori_loop(0, length, body, init)


def _lane_masks(n_sub):
    if n_sub == 1:
        return [None]
    lane = lax.broadcasted_iota(jnp.int32, (1, LANE), 1)
    w = LANE // n_sub
    return [jnp.where((lane >= h * w) & (lane < (h + 1) * w), 1.0, 0.0) for h in range(n_sub)]


def _when(cond, always):
    return (lambda f: f()) if always else pl.when(cond)


def _per_row(body, row_flags, g_rows):
    def kern(*refs):
        deferred = []
        for g in range(g_rows):
            body(*[r.at[g] if f else r for r, f in zip(refs, row_flags)], deferred=deferred)
        for cond, fn in deferred:
            pl.when(cond)(fn)
    return kern


def _rows_per_step(bsz, single_block):
    g = min(bsz, 8) if single_block else 1
    while bsz % g:
        g -= 1
    return g


def _glr_kernel(*refs, kind, layer, length, n_chunks, n_groups, n_sub, has_state, single_block,
                deferred):
    if kind == "hgrn":
        q_ref, v_ref, og_ref, f_ref, lbl_ref, nw_ref = refs[:6]
        refs = refs[6:]
    else:
        q_ref, k_ref, v_ref, og_ref, s_ref, w2_ref, ba_ref, nw_ref = refs[:8]
        refs = refs[8:]
    if has_state:
        s0_ref, refs = refs[0], refs[1:]
    (o_ref, sout_ref, st_scr, qt_scr, qin_scr, kt_scr, kst_scr, dec_scr, p_scr, inter_scr,
     tmp_scr) = refs
    t = pl.program_id(1)
    n_heads = n_groups * n_sub
    dv = LANE
    tb = length * n_chunks

    xw = n_chunks * LANE

    def xcols(idx, c):
        return slice(idx * xw + c * LANE, idx * xw + (c + 1) * LANE)

    @_when(t == 0, single_block)
    def _():
        for gi in range(n_groups):
            if has_state:
                st_scr[gi] = s0_ref[gi].T
            else:
                st_scr[gi] = jnp.zeros((dv, LANE), F32)
        if n_chunks > 1:
            kst_scr[...] = jnp.zeros_like(kst_scr)

    causal, tri_blk = _chunk_consts(length, n_chunks)
    masks = _lane_masks(n_sub)

    if kind == "hgrn":
        logits = lbl_ref[...]
        e = jnp.exp(logits - jnp.max(logits, axis=0, keepdims=True))
        p = e / jnp.sum(e, axis=0, keepdims=True)
        lb = jnp.zeros((1, p.shape[1]), F32)
        for j in range(1, layer + 1):
            lb = lb + p[j:j + 1, :]

    def load_group(rows, gi, with_v=True):
        lanes = slice(gi * LANE, (gi + 1) * LANE)
        if kind == "hgrn":
            aq = q_ref[rows, lanes].astype(F32)
            q = aq * _sigmoid(aq)
            x = f_ref[rows, lanes]
            lbh = lb[:, lanes]
            forget = lbh + (1.0 - lbh) * _sigmoid(x)
            g = jnp.log(forget) * LOG2E
            k = 1.0 - forget
        else:
            q = q_ref[rows, lanes].astype(F32) * (float(LANE // n_sub) ** -0.5)
            k = k_ref[rows, lanes].astype(F32)
            pre = _dot(s_ref[rows, :].astype(BF16), w2_ref[:, lanes]) + ba_ref[:, lanes]
            g = _log_sigmoid(pre) * (LOG2E / GLA_GATE_TEMP)
        vs = [v_ref[rows, (gi * n_sub + h) * dv:(gi * n_sub + h + 1) * dv].astype(F32)
              for h in range(n_sub)] if with_v else None
        return q, k, g, vs

    def finish(rows, head, o):
        lanes = slice(head * dv, (head + 1) * dv)
        og = og_ref[rows, lanes].astype(F32)
        if kind == "hgrn":
            y = _head_norm(o * _sigmoid(og), nw_ref[:, lanes])
        else:
            y = _head_norm(o, nw_ref[:, lanes]) * (og * _sigmoid(og))
        o_ref[rows, lanes] = y.astype(o_ref.dtype)

    spans = []

    gates, b_of = [], []
    for g0 in range(0, n_groups, 2):
        pair = [load_group(slice(0, tb), gi, with_v=False) for gi in range(g0, min(g0 + 2, n_groups))]
        b_pair = _cumsum_rows(tri_blk, jnp.concatenate([g for _, _, g, _ in pair], axis=1))
        gates += pair
        b_of += [b_pair[:, j * LANE:(j + 1) * LANE] for j in range(len(pair))]

    def phase1(gi):
        glanes = slice(gi * LANE, (gi + 1) * LANE)
        q, k, _, _ = gates[gi]
        b = b_of[gi]
        for c in range(n_chunks):
            sl = slice(c * length, (c + 1) * length)
            bc, qc, kc = b[sl], q[sl], k[sl]
            b_mid = bc[length // 2:length // 2 + 1, :]
            b_last = bc[length - 1:length, :]
            spans.append(jnp.maximum(bc[0:1, :] - b_mid, b_mid - b_last))
            qt = qc * jnp.exp2(bc - b_mid)
            q_in = qc * jnp.exp2(bc)
            kt_scr[sl, glanes] = (kc * jnp.exp2(b_mid - bc)).astype(BF16)
            kst_scr[sl, xcols(gi, c)] = (kc * jnp.exp2(b_last - bc)).astype(BF16)
            dec_scr[c:c + 1, glanes] = jnp.exp2(b_last)
            for h, m in enumerate(masks):
                head = gi * n_sub + h
                qt_scr[sl, head * LANE:(head + 1) * LANE] = (qt if m is None else qt * m).astype(BF16)
                qin_scr[sl, head * LANE:(head + 1) * LANE] = (q_in if m is None else q_in * m).astype(BF16)

    starts = {}

    def phase2(gi):
        glanes = slice(gi * LANE, (gi + 1) * LANE)
        kt = kt_scr[:, glanes]
        kst_x = kst_scr[:, gi * xw:(gi + 1) * xw]
        upd = None
        for h, m in enumerate(masks):
            head = gi * n_sub + h
            hl = slice(head * LANE, (head + 1) * LANE)
            p_scr[head] = jnp.where(causal, _dot_nt(qt_scr[:, hl], kt), 0.0).astype(BF16)
            u = _dot_tn(v_ref[:, hl], kst_x)
            if m is not None:
                u = u * jnp.concatenate([m] * n_chunks, axis=1)
            upd = u if upd is None else upd + u
        st = st_scr[gi]
        starts[gi] = []
        for c in range(n_chunks):
            starts[gi].append(st.astype(BF16))
            st = st * dec_scr[c:c + 1, glanes] + upd[:, c * LANE:(c + 1) * LANE]
        st_scr[gi] = st

    def phase3(gi):
        for h in range(n_sub):
            head = gi * n_sub + h
            hl = slice(head * LANE, (head + 1) * LANE)
            o_intra = _dot(p_scr[head], v_ref[:, hl])
            parts = [_dot_nt(qin_scr[c * length:(c + 1) * length, hl], starts[gi][c])
                     for c in range(n_chunks)]
            o_inter = parts[0] if n_chunks == 1 else jnp.concatenate(parts, axis=0)
            inter_scr[:, hl] = o_inter
            finish(slice(0, tb), head, o_intra + o_inter)

    for gi in range(n_groups):
        phase1(gi)
        phase2(gi)
    for gi in range(n_groups):
        phase3(gi)
    span_vec = functools.reduce(jnp.maximum, spans)

    def redo_intra():
        tri = tri_blk[0:length, 0:length]

        def chunk(c, carry):
            rows = pl.ds(pl.multiple_of(c * length, length), length)
            for gi in range(n_groups):
                q, k, g, vs = load_group(rows, gi)
                intra = _glr_intra_exact(q, k, g, vs, masks, tmp_scr, tri)
                for h in range(n_sub):
                    head = gi * n_sub + h
                    finish(rows, head, intra[h] + inter_scr[rows, head * dv:(head + 1) * dv])
            return carry

        lax.fori_loop(0, n_chunks, chunk, 0)

    deferred.append((jnp.max(span_vec) > SAFE_DECAY_SPAN * LOG2E, redo_intra))

    @_when(t == pl.num_programs(1) - 1, single_block)
    def _():
        for gi in range(n_groups):
            sout_ref[gi] = st_scr[gi].T


def _time_block(kind, t_len):
    tb = _pick_tile(t_len, 256)
    length = min(CHUNK, t_len) if kind == "hgrn" else tb
    assert tb % length == 0
    return length, tb


def _glr(kind, layer, bsz, t_len, ins, s0):
    length, tb = _time_block(kind, t_len)
    n_groups, n_sub = (4, 1) if kind == "hgrn" else (2, 2)
    n_heads = n_groups * n_sub
    bw = n_heads * LANE
    has_state = s0 is not None
    single_block = t_len == tb
    rows = _rows_per_step(bsz, single_block)

    def tok(width, blk):
        return pl.BlockSpec((rows, tb, width), lambda b, t: (b, t, blk))

    def full(shape):
        return pl.BlockSpec(shape, lambda b, t: (0,) * len(shape))

    if kind == "hgrn":
        a, f, lbl, nw = ins
        a = a.reshape(bsz, t_len, -1)
        f = f.reshape(bsz, t_len, -1)
        args = [a, a, a, f, lbl, nw]
        in_specs = [tok(bw, 0), tok(bw, 1), tok(bw, 2), tok(bw, 0), full(lbl.shape),
                    pl.BlockSpec((None, 1, bw), lambda b, t: (layer, 0, 0))]
        row_flags = [True] * 4 + [False] * 2
    else:
        c, s, w2, ba, nw = ins
        c = c.reshape(bsz, t_len, -1)
        s = s.reshape(bsz, t_len, -1)
        kw = n_groups * LANE
        args = [c, c, c, c, s, w2, ba, nw]
        in_specs = [tok(kw, 0), tok(kw, 1), tok(bw, 1), tok(bw, 2), tok(LANE, 0),
                    pl.BlockSpec((None, LANE, kw), lambda b, t: (layer, 0, 0)),
                    pl.BlockSpec((None, 1, kw), lambda b, t: (layer, 0, 0)),
                    pl.BlockSpec((None, 1, bw), lambda b, t: (layer, 0, 0))]
        row_flags = [True] * 5 + [False] * 3
    st_spec = pl.BlockSpec((rows, n_groups, LANE, LANE), lambda b, t: (b, 0, 0, 0))
    if has_state:
        args.append(s0)
        in_specs.append(st_spec)
        row_flags.append(True)
    scratch = [((n_groups, LANE, LANE), F32),
               ((tb, bw), BF16),
               ((tb, bw), BF16),
               ((tb, n_groups * LANE), BF16),
               ((tb, n_groups * LANE * (tb // length)), BF16),
               ((max(tb // length, 8), n_groups * LANE), F32),
               ((n_heads, tb, tb), BF16),
               ((tb, bw), F32),
               ((2 + n_sub, length, LANE), F32)]
    row_flags += [True] * (2 + len(scratch))
    body = functools.partial(_glr_kernel, kind=kind, layer=layer, length=length,
                             n_chunks=tb // length, n_groups=n_groups, n_sub=n_sub,
                             has_state=has_state, single_block=single_block)
    o, s_new = pl.pallas_call(
        _per_row(body, row_flags, rows),
        grid=(bsz // rows, t_len // tb),
        in_specs=in_specs,
        out_specs=[pl.BlockSpec((rows, tb, bw), lambda b, t: (b, t, 0)), st_spec],
        out_shape=[jax.ShapeDtypeStruct((bsz, t_len, bw), BF16),
                   jax.ShapeDtypeStruct((bsz, n_groups, LANE, LANE), F32)],
        scratch_shapes=[pltpu.VMEM((rows,) + shape, dt) for shape, dt in scratch],
        compiler_params=_params(("parallel", "arbitrary")),
        name=kind,
    )(*args)
    return o.reshape(bsz * t_len, bw), s_new


def _mlstm_kernel(*refs, length, n_heads, has_state, single_block, deferred):
    q_ref, k_ref, v_ref, og_ref, s_ref, bias_ref, nw_ref = refs[:7]
    refs = refs[7:]
    if has_state:
        c0_ref, n0_ref, m0_ref = refs[:3]
        refs = refs[3:]
    o_ref, cout_ref, nout_ref, mout_ref, cn_scr, m_scr, s_scr = refs
    t = pl.program_id(1)
    dk = dv = LANE
    scale = float(dk) ** -0.5
    lane = lax.broadcasted_iota(jnp.int32, (1, LANE), 1)

    del deferred

    @_when(t == 0, single_block)
    def _():
        for h in range(n_heads):
            if has_state:
                n_cols = jnp.broadcast_to(n0_ref[h:h + 1, :], (dv, dk)).T
                cn_scr[h] = jnp.concatenate([c0_ref[h], n_cols], axis=1)
                m_scr[h:h + 1, :] = jnp.broadcast_to(m0_ref[:, h:h + 1], (1, LANE)) * LOG2E
            else:
                cn_scr[h] = jnp.zeros((dk, 2 * dv), F32)
                m_scr[h:h + 1, :] = jnp.zeros((1, LANE), F32)

    causal, tri = _chunk_consts(length)
    sel = jnp.where(lax.broadcasted_iota(jnp.int32, (8, LANE), 0)
                    == lax.broadcasted_iota(jnp.int32, (8, LANE), 1), 1.0, 0.0).astype(BF16)
    reps = -(-length // LANE)

    x = s_ref[...] + bias_ref[...]
    y = jnp.where(lane < n_heads, x, _cumsum_rows(tri, _log_sigmoid(x))) * LOG2E
    y_t = _rows_to_lanes(sel, y)
    ones = jnp.ones((length, dv), BF16)
    m_row = jnp.zeros((1, LANE), F32)

    def gate_cols(h):
        i_c = jnp.broadcast_to(y[:, h:h + 1], (length, LANE))
        b_c = jnp.broadcast_to(y[:, n_heads + h:n_heads + h + 1], (length, LANE))
        return i_c, b_c, b_c[length - 1:length, :]

    def intra_logits(h, b_c):
        z_r = y_t[h:h + 1, :] - y_t[n_heads + h:n_heads + h + 1, :]
        b_cw = jnp.concatenate([b_c] * reps, axis=1)[:, :length]
        return jnp.where(causal, b_cw + z_r, -jnp.inf)

    staged = []
    for h in range(n_heads):
        lanes = slice(h * dk, (h + 1) * dk)
        q = q_ref[:, lanes]
        kf = k_ref[:, lanes].astype(F32) * scale
        v_ext = jnp.concatenate([v_ref[:, lanes], ones], axis=1)
        i_c, b_c, b_last = gate_cols(h)
        s_scr[h] = _dot_nt(q, kf.astype(BF16))
        e_intra = b_last - b_c + i_c
        e_max = jnp.max(e_intra, axis=0, keepdims=True)
        kw = jnp.exp2(e_intra - e_max) * kf
        u_ext = _dot_tn(kw.astype(BF16), v_ext)
        m_old = m_scr[h:h + 1, :]
        cn_old = cn_scr[h]
        qc_ext = _dot(q, cn_old.astype(BF16))
        e_inter = b_last + m_old
        m_new = jnp.maximum(e_inter, e_max)
        w_c = jnp.exp2(e_inter - m_new)
        f_s = jnp.exp2(e_max - m_new)
        cn_scr[h] = (jnp.concatenate([w_c, w_c], axis=1) * cn_old
                     + jnp.concatenate([f_s, f_s], axis=1) * u_ext)
        m_scr[h:h + 1, :] = m_new
        m_row = jnp.where(lane == h, m_new, m_row)
        staged.append((v_ext, b_c + m_old, qc_ext))

    for h in range(n_heads):
        lanes = slice(h * dk, (h + 1) * dk)
        v_ext, d_inter, qc_ext = staged[h]
        d_intra = intra_logits(h, gate_cols(h)[1])
        a_t = jnp.max(d_intra, axis=1, keepdims=True)
        p = jnp.exp2(d_intra - a_t) * s_scr[h]
        num_ext = _dot(p.astype(BF16), v_ext)
        m_t = jnp.maximum(d_inter, a_t)
        f_intra = jnp.exp2(a_t - m_t)
        w_inter = jnp.exp2(d_inter - m_t)
        tot = (jnp.concatenate([f_intra, f_intra], axis=1) * num_ext
               + jnp.concatenate([w_inter, w_inter], axis=1) * qc_ext)
        hid = tot[:, :dv] / jnp.maximum(jnp.abs(tot[:, dv:]), jnp.exp2(-m_t))
        og = og_ref[:, lanes].astype(F32)
        y_out = _sigmoid(og) * _head_norm(hid, nw_ref[:, lanes])
        o_ref[:, lanes] = y_out.astype(o_ref.dtype)

    @_when(t == pl.num_programs(1) - 1, single_block)
    def _():
        for h in range(n_heads):
            cn = cn_scr[h]
            cout_ref[h] = cn[:, :dv]
            nout_ref[h:h + 1, :] = cn[:, dv:].T[0:1, :]
        mout_ref[...] = m_row * (1.0 / LOG2E)


def _mlstm(layer, bsz, t_len, p_b, p_s, bias_row, norm_w, states):
    length = tb = _pick_tile(t_len, MLSTM_CHUNK)
    n_heads = p_b.shape[1] // (4 * LANE)
    bw = n_heads * LANE
    has_state = states is not None
    p_b = p_b.reshape(bsz, t_len, -1)
    p_s = p_s.reshape(bsz, t_len, -1)

    single_block = t_len == tb
    rows = _rows_per_step(bsz, single_block)

    def tok(width, blk):
        return pl.BlockSpec((rows, tb, width), lambda b, t: (b, t, blk))

    c_spec = pl.BlockSpec((rows, n_heads, LANE, LANE), lambda b, t: (b, 0, 0, 0))
    n_spec = pl.BlockSpec((rows, n_heads, LANE), lambda b, t: (b, 0, 0))
    m_spec = pl.BlockSpec((rows, 1, LANE), lambda b, t: (b, 0, 0))
    args = [p_b, p_b, p_b, p_b, p_s, bias_row, norm_w]
    in_specs = [tok(bw, 0), tok(bw, 1), tok(bw, 2), tok(bw, 3), tok(LANE, 0),
                pl.BlockSpec((None, 1, LANE), lambda b, t: (layer, 0, 0)),
                pl.BlockSpec((None, 1, bw), lambda b, t: (layer, 0, 0))]
    row_flags = [True] * 5 + [False] * 2
    if has_state:
        args += list(states)
        in_specs += [c_spec, n_spec, m_spec]
        row_flags += [True] * 3
    scratch = [((n_heads, LANE, 2 * LANE), F32), ((n_heads, LANE), F32), ((n_heads, tb, tb), F32)]
    row_flags += [True] * (4 + len(scratch))
    body = functools.partial(_mlstm_kernel, length=length, n_heads=n_heads, has_state=has_state,
                             single_block=single_block)
    o, c_new, n_new, m_new = pl.pallas_call(
        _per_row(body, row_flags, rows),
        grid=(bsz // rows, t_len // tb),
        in_specs=in_specs,
        out_specs=[pl.BlockSpec((rows, tb, bw), lambda b, t: (b, t, 0)), c_spec, n_spec, m_spec],
        out_shape=[jax.ShapeDtypeStruct((bsz, t_len, bw), BF16),
                   jax.ShapeDtypeStruct((bsz, n_heads, LANE, LANE), F32),
                   jax.ShapeDtypeStruct((bsz, n_heads, LANE), F32),
                   jax.ShapeDtypeStruct((bsz, 1, LANE), F32)],
        scratch_shapes=[pltpu.VMEM((rows,) + shape, dt) for shape, dt in scratch],
        compiler_params=_params(("parallel", "arbitrary")),
        name="mlstm",
    )(*args)
    return o.reshape(bsz * t_len, bw), c_new, n_new, m_new


def _prep_weights(w_in, gla_w_a2, mlstm_gate_bias, heads):
    (ah, adk, adv), (bh, bdk, bdv), (ch, cdk, cdv), rank = heads
    sizes = (ah * adk, ah * adk, ah * adv, ah * adv,
             bh * bdk, bh * bdk, bh * bdv, bh * bdv, 2 * bh,
             ch * cdk, ch * cdk, ch * cdv, ch * cdv, rank)
    offs = [0]
    for s in sizes:
        offs.append(offs[-1] + s)
    col = lambda i: w_in[:, :, offs[i]:offs[i + 1]]
    (a_q, a_f, a_i, a_g, b_q, b_k, b_v, b_o, b_if, c_q, c_k, c_v, c_g, c_lr) = [col(i) for i in range(14)]
    pad = jnp.zeros(w_in.shape[:2] + (LANE - 2 * bh - rank,), w_in.dtype)
    groups = ([a_q, a_i, a_g], [a_f], [b_q, b_k, b_v, b_o], [c_q, c_k, c_v, c_g], [b_if, c_lr, pad])
    widths = tuple(sum(g.shape[-1] for g in grp) for grp in groups)
    w_cat = jnp.concatenate([g for grp in groups for g in grp], axis=-1).astype(BF16)
    w_mg = w_in[:, :, offs[14]:].astype(BF16)
    depth = w_in.shape[0]
    w2_pad = jnp.zeros((depth, LANE, gla_w_a2.shape[-1]), F32)
    w2_pad = w2_pad.at[:, 2 * bh:2 * bh + rank, :].set(gla_w_a2).astype(BF16)
    bias_row = jnp.zeros((depth, 1, LANE), F32).at[:, 0, :2 * bh].set(mlstm_gate_bias)
    return w_cat, widths, w_mg, w2_pad, bias_row


def _run_trunk(x, states, wts, heads):
    (ffn1_norm, ffn1_up, ffn1_down, mix_norm, w_cat, widths, w_mg, hgrn_lb_logits, hgrn_norm,
     bias_row, mlstm_norm, w2_pad, gla_b_a, gla_norm, w_branch, w_out,
     ffn2_norm, ffn2_up, ffn2_down, final_norm) = wts
    (ah, adk, adv), (bh, bdk, bdv), (ch, cdk, cdv), _ = heads
    bsz, t_len, d = x.shape
    depth = ffn1_norm.shape[0]
    x = x.reshape(bsz * t_len, d)
    new_states = ([], [], [], [], [])
    for l in range(depth):
        x = _ffn(x, ffn1_norm, ffn1_up, ffn1_down, l)
        p_a, p_f, p_b, p_c, p_s = _inproj(x, mix_norm, w_cat, l, widths)
        if states is None:
            st_a = st_b = st_c = None
        else:
            s_hgrn, c_ml, n_ml, m_ml, s_gla = states
            st_a = s_hgrn[l]
            m_pad = jnp.zeros((bsz, 1, LANE), F32).at[:, 0, :bh].set(m_ml[l])
            st_b = (c_ml[l], n_ml[l], m_pad)
            st_c = s_gla[l].reshape(bsz, ch // 2, 2 * cdk, cdv)
        o_a, s_a = _glr("hgrn", l, bsz, t_len, (p_a, p_f, hgrn_lb_logits, hgrn_norm), st_a)
        o_b, c_new, n_new, m_new = _mlstm(l, bsz, t_len, p_b, p_s, bias_row, mlstm_norm, st_b)
        o_c, s_c = _glr("gla", l, bsz, t_len, (p_c, p_s, w2_pad, gla_b_a, gla_norm), st_c)
        x = _merge(x, mix_norm, o_a, o_b, o_c, w_mg, w_branch, w_out, l)
        x = _ffn(x, ffn2_norm, ffn2_up, ffn2_down, l,
                 final_w=final_norm if l == depth - 1 else None)
        for acc, s in zip(new_states, (s_a, c_new, n_new, m_new[:, 0, :bh],
                                       s_c.reshape(bsz, ch, cdk, cdv))):
            acc.append(s)
    return x.reshape(bsz, t_len, d), tuple(jnp.stack(acc) for acc in new_states)


def kernel(x_prompt, x_sample, state_hgrn, state_mlstm_c, state_mlstm_n, state_mlstm_m, state_gla, ffn1_norm, ffn1_w_up, ffn1_w_down, mix_norm, w_in, hgrn_lb_logits, hgrn_norm, mlstm_gate_bias, mlstm_norm, gla_w_a2, gla_b_a, gla_norm, w_branch, w_out, ffn2_norm, ffn2_w_up, ffn2_w_down, final_norm):
    heads = (state_hgrn.shape[2:], state_mlstm_c.shape[2:], state_gla.shape[2:], gla_w_a2.shape[1])
    assert heads[0][1:] == (LANE, LANE) and heads[1][1:] == (LANE, LANE)
    assert heads[2][1:] == (LANE // 2, LANE) and heads[2][0] % 2 == 0
    w_cat, widths, w_mg, w2_pad, bias_row = _prep_weights(w_in, gla_w_a2, mlstm_gate_bias, heads)
    row = lambda a: a[:, None, :]
    wts = (row(ffn1_norm), ffn1_w_up.astype(BF16), ffn1_w_down.astype(BF16), row(mix_norm),
           w_cat, widths, w_mg, hgrn_lb_logits, row(hgrn_norm), bias_row, row(mlstm_norm),
           w2_pad, row(gla_b_a), row(gla_norm), w_branch.astype(BF16), w_out.astype(BF16),
           row(ffn2_norm), ffn2_w_up.astype(BF16), ffn2_w_down.astype(BF16), final_norm[None, :])
    y_p, st_p = _run_trunk(x_prompt, None, wts, heads)
    sample_states = (state_hgrn, state_mlstm_c, state_mlstm_n, state_mlstm_m, state_gla)
    y_s, st_s = _run_trunk(x_sample, sample_states, wts, heads)
    return (y_p, y_s) + st_p + st_s
```

```python
import functools

import jax
import jax.numpy as jnp
from jax import lax
from jax.experimental import pallas as pl
from jax.experimental.pallas import tpu as pltpu

F32 = jnp.float32
BF16 = jnp.bfloat16

EPS = 1e-6
CHUNK = 64
MLSTM_CHUNK = 256
GLA_GATE_TEMP = 16.0
N_BRANCH = 3
LANE = 128
V7X_VMEM_BYTES = 64 * 1024 * 1024
VMEM_LIMIT = V7X_VMEM_BYTES - 8 * 1024 * 1024
SAFE_DECAY_SPAN = 60.0
LOG2E = 1.4426950408889634

_NT = (((1,), (1,)), ((), ()))
_TN = (((0,), (0,)), ((), ()))


def _dot(a, b):
    return jnp.dot(a, b, preferred_element_type=F32)


def _dot_nt(a, b):
    return lax.dot_general(a, b, _NT, preferred_element_type=F32)


def _dot_tn(a, b):
    return lax.dot_general(a, b, _TN, preferred_element_type=F32)


def _rms(x, w):
    return x * lax.rsqrt(jnp.mean(x * x, axis=-1, keepdims=True) + EPS) * w


def _sigmoid(x):
    return 1.0 / (1.0 + jnp.exp2(x * (-LOG2E)))


def _log_sigmoid(x):
    return jnp.minimum(x, 0.0) - jnp.log(1.0 + jnp.exp(-jnp.abs(x)))


def _split3(x):
    hi = x.astype(BF16)
    r = x - hi.astype(F32)
    mid = r.astype(BF16)
    lo = (r - mid.astype(F32)).astype(BF16)
    return hi, mid, lo


def _cumsum_rows(tri, g):
    hi, mid, lo = _split3(g)
    return (_dot(tri, hi) + _dot(tri, mid)) + _dot(tri, lo)


def _rows_to_lanes(sel, x):
    hi, mid, lo = _split3(x)
    return (_dot_nt(sel, hi) + _dot_nt(sel, mid)) + _dot_nt(sel, lo)


def _pick_tile(n, cap):
    t = min(n, cap)
    while n % t or (t % 8 and t != n):
        t -= 1
    return t


def _row_parts(tm):
    n = 2 if tm % 512 == 0 and tm >= 1024 else 1
    return [slice(r * (tm // n), (r + 1) * (tm // n)) for r in range(n)]


def _params(sem):
    return pltpu.CompilerParams(dimension_semantics=sem, vmem_limit_bytes=VMEM_LIMIT)


def _ffn_kernel(x_ref, nw_ref, wup_ref, wd_ref, *rest, d_ff, tf, tn, final):
    if final:
        fw_ref, o_ref, act_scr = rest
    else:
        o_ref, act_scr = rest
    d = x_ref.shape[1]
    for rows in _row_parts(x_ref.shape[0]):
        h = _rms(x_ref[rows, :], nw_ref[...]).astype(BF16)
        for c in range(0, d_ff, tf):
            g = _dot(h, wup_ref[:, c:c + tf])
            u = _dot(h, wup_ref[:, d_ff + c:d_ff + c + tf])
            act_scr[rows, c:c + tf] = (g * _sigmoid(g) * u).astype(BF16)
        for c in range(0, d, tn):
            o_ref[rows, c:c + tn] = (x_ref[rows, c:c + tn]
                                     + 0.5 * _dot(act_scr[rows, :], wd_ref[:, c:c + tn]))
        if final:
            o_ref[rows, :] = _rms(o_ref[rows, :], fw_ref[...])


def _ffn(x, norm_w, w_up, w_down, layer, final_w=None):
    n, d = x.shape
    d_ff = w_down.shape[1]
    tm = _pick_tile(n, 1024)
    tf = tn = 256
    assert d_ff % tf == 0 and d % tn == 0
    final = final_w is not None
    resident = pl.Buffered(1)
    in_specs = [
        pl.BlockSpec((tm, d), lambda i: (i, 0)),
        pl.BlockSpec((None, 1, d), lambda i: (layer, 0, 0)),
        pl.BlockSpec((None, d, 2 * d_ff), lambda i: (layer, 0, 0), pipeline_mode=resident),
        pl.BlockSpec((None, d_ff, d), lambda i: (layer, 0, 0), pipeline_mode=resident),
    ]
    args = [x, norm_w, w_up, w_down]
    if final:
        in_specs.append(pl.BlockSpec((1, d), lambda i: (0, 0)))
        args.append(final_w)
    return pl.pallas_call(
        functools.partial(_ffn_kernel, d_ff=d_ff, tf=tf, tn=tn, final=final),
        grid=(n // tm,),
        in_specs=in_specs,
        out_specs=pl.BlockSpec((tm, d), lambda i: (i, 0)),
        out_shape=jax.ShapeDtypeStruct((n, d), F32),
        scratch_shapes=[pltpu.VMEM((tm, d_ff), BF16)],
        compiler_params=_params(("parallel",)),
        name="ffn",
    )(*args)


def _inproj_kernel(x_ref, nw_ref, w_ref, oa_ref, of_ref, ob_ref, oc_ref, os_ref, *, widths):
    for rows in _row_parts(x_ref.shape[0]):
        h = _rms(x_ref[rows, :], nw_ref[...]).astype(BF16)
        c0 = 0
        for ref, width in zip((oa_ref, of_ref, ob_ref, oc_ref, os_ref), widths):
            step = min(width, 512)
            for c in range(0, width, step):
                ref[rows, c:c + step] = _dot(h, w_ref[:, c0 + c:c0 + c + step]).astype(ref.dtype)
            c0 += width


def _inproj(x, norm_w, w_cat, layer, widths):
    n, d = x.shape
    tm = _pick_tile(n, 1024)
    dtypes = (BF16, F32, BF16, BF16, F32)
    return pl.pallas_call(
        functools.partial(_inproj_kernel, widths=widths),
        grid=(n // tm,),
        in_specs=[
            pl.BlockSpec((tm, d), lambda i: (i, 0)),
            pl.BlockSpec((None, 1, d), lambda i: (layer, 0, 0)),
            pl.BlockSpec((None, d, sum(widths)), lambda i: (layer, 0, 0),
                         pipeline_mode=pl.Buffered(1)),
        ],
        out_specs=[pl.BlockSpec((tm, w), lambda i: (i, 0)) for w in widths],
        out_shape=[jax.ShapeDtypeStruct((n, w), dt) for w, dt in zip(widths, dtypes)],
        compiler_params=_params(("parallel",)),
        name="inproj",
    )(x, norm_w, w_cat)


def _merge_kernel(x_ref, nw_ref, oa_ref, ob_ref, oc_ref, wm_ref, wb_ref, wo_ref, o_ref):
    d = x_ref.shape[1]
    for rows in _row_parts(x_ref.shape[0]):
        x = x_ref[rows, :]
        h = _rms(x, nw_ref[...]).astype(BF16)
        merged = None
        for nb, br_ref in enumerate((oa_ref, ob_ref, oc_ref)):
            gate = _sigmoid(_dot(h, wm_ref[:, nb * d:(nb + 1) * d]))
            term = gate * _dot(br_ref[rows, :], wb_ref[nb])
            merged = term if merged is None else merged + term
        o_ref[rows, :] = x + _dot(merged.astype(BF16), wo_ref[...])


def _merge(x, norm_w, o_a, o_b, o_c, w_mg, w_branch, w_out, layer):
    n, d = x.shape
    bw = o_a.shape[1]
    tm = _pick_tile(n, 1024)
    br_spec = pl.BlockSpec((tm, bw), lambda i: (i, 0))
    resident = pl.Buffered(1)
    return pl.pallas_call(
        _merge_kernel,
        grid=(n // tm,),
        in_specs=[
            pl.BlockSpec((tm, d), lambda i: (i, 0)),
            pl.BlockSpec((None, 1, d), lambda i: (layer, 0, 0)),
            br_spec, br_spec, br_spec,
            pl.BlockSpec((None, d, N_BRANCH * d), lambda i: (layer, 0, 0), pipeline_mode=resident),
            pl.BlockSpec((None, N_BRANCH, bw, d), lambda i: (layer, 0, 0, 0),
                         pipeline_mode=resident),
            pl.BlockSpec((None, d, d), lambda i: (layer, 0, 0), pipeline_mode=resident),
        ],
        out_specs=pl.BlockSpec((tm, d), lambda i: (i, 0)),
        out_shape=jax.ShapeDtypeStruct((n, d), F32),
        compiler_params=_params(("parallel",)),
        name="merge",
    )(x, norm_w, o_a, o_b, o_c, w_mg, w_branch, w_out)


def _chunk_consts(length, n_chunks=1):
    tb = length * n_chunks
    rb = lax.broadcasted_iota(jnp.int32, (tb, tb), 0)
    cb = lax.broadcasted_iota(jnp.int32, (tb, tb), 1)
    causal = rb >= cb
    for i in range(1, n_chunks):
        causal = causal & ((rb < i * length) | (cb >= i * length))
    return causal, jnp.where(causal, 1.0, 0.0).astype(BF16)


def _head_norm(o, w):
    return o * lax.rsqrt(jnp.mean(o * o, axis=-1, keepdims=True) + EPS) * w


def _glr_intra_exact(q, k, g, vs, masks, tmp_ref, tri):
    length = q.shape[0]
    b = _cumsum_rows(tri, g)
    tmp_ref[0, 0:length, :] = k
    tmp_ref[1, 0:length, :] = b
    for h, v in enumerate(vs):
        tmp_ref[2 + h, 0:length, :] = v
    t_idx = lax.broadcasted_iota(jnp.int32, (length, 1), 0)

    def body(s, accs):
        k_s = tmp_ref[0, pl.ds(s, 1), :]
        b_s = tmp_ref[1, pl.ds(s, 1), :]
        a = q * k_s * jnp.exp2(jnp.minimum(b - b_s, 0.0))
        new = []
        for h, m in enumerate(masks):
            ah = a if m is None else a * m
            col = jnp.where(t_idx >= s, jnp.sum(ah, axis=1, keepdims=True), 0.0)
            new.append(accs[h] + col * tmp_ref[2 + h, pl.ds(s, 1), :])
        return tuple(new)

    init = tuple(jnp.zeros((length, v.shape[1]), F32) for v in vs)
    return lax.fori_loop(0, length, body, init)


def _lane_masks(n_sub):
    if n_sub == 1:
        return [None]
    lane = lax.broadcasted_iota(jnp.int32, (1, LANE), 1)
    w = LANE // n_sub
    return [jnp.where((lane >= h * w) & (lane < (h + 1) * w), 1.0, 0.0) for h in range(n_sub)]


def _when(cond, always):
    return (lambda f: f()) if always else pl.when(cond)


def _per_row(body, row_flags, g_rows):
    def kern(*refs):
        deferred = []
        live = [body(*[r.at[g] if f else r for r, f in zip(refs, row_flags)], deferred=deferred)
                for g in range(g_rows)]
        while live:
            still = []
            for gen in live:
                try:
                    next(gen)
                    still.append(gen)
                except StopIteration:
                    pass
            live = still
        for cond, fn in deferred:
            pl.when(cond)(fn)
    return kern


def _rows_per_step(bsz, single_block):
    g = min(bsz, 8 if single_block else 2)
    while bsz % g:
        g -= 1
    return g


def _glr_kernel(*refs, kind, layer, length, n_chunks, n_groups, n_sub, has_state, single_block,
                deferred):
    if kind == "hgrn":
        q_ref, v_ref, og_ref, f_ref, lbl_ref, nw_ref = refs[:6]
        refs = refs[6:]
    else:
        q_ref, k_ref, v_ref, og_ref, s_ref, w2_ref, ba_ref, nw_ref = refs[:8]
        refs = refs[8:]
    if has_state:
        s0_ref, refs = refs[0], refs[1:]
    (o_ref, sout_ref, st_scr, qt_scr, qin_scr, kt_scr, kst_scr, dec_scr, p_scr, inter_scr,
     tmp_scr) = refs
    t = pl.program_id(1)
    n_heads = n_groups * n_sub
    dv = LANE
    tb = length * n_chunks

    xw = n_chunks * LANE

    def xcols(idx, c):
        return slice(idx * xw + c * LANE, idx * xw + (c + 1) * LANE)

    @_when(t == 0, single_block)
    def _():
        for gi in range(n_groups):
            if has_state:
                st_scr[gi] = s0_ref[gi].T
            else:
                st_scr[gi] = jnp.zeros((dv, LANE), F32)
        if n_chunks > 1:
            kst_scr[...] = jnp.zeros_like(kst_scr)

    yield
    causal, tri_blk = _chunk_consts(length, n_chunks)
    masks = _lane_masks(n_sub)

    if kind == "hgrn":
        logits = lbl_ref[...]
        e = jnp.exp(logits - jnp.max(logits, axis=0, keepdims=True))
        p = e / jnp.sum(e, axis=0, keepdims=True)
        lb = jnp.zeros((1, p.shape[1]), F32)
        for j in range(1, layer + 1):
            lb = lb + p[j:j + 1, :]

    def load_group(rows, gi, with_v=True):
        lanes = slice(gi * LANE, (gi + 1) * LANE)
        if kind == "hgrn":
            aq = q_ref[rows, lanes].astype(F32)
            q = aq * _sigmoid(aq)
            x = f_ref[rows, lanes]
            lbh = lb[:, lanes]
            forget = lbh + (1.0 - lbh) * _sigmoid(x)
            g = jnp.log(forget) * LOG2E
            k = 1.0 - forget
        else:
            q = q_ref[rows, lanes].astype(F32) * (float(LANE // n_sub) ** -0.5)
            k = k_ref[rows, lanes].astype(F32)
            pre = _dot(s_ref[rows, :].astype(BF16), w2_ref[:, lanes]) + ba_ref[:, lanes]
            g = _log_sigmoid(pre) * (LOG2E / GLA_GATE_TEMP)
        vs = [v_ref[rows, (gi * n_sub + h) * dv:(gi * n_sub + h + 1) * dv].astype(F32)
              for h in range(n_sub)] if with_v else None
        return q, k, g, vs

    def finish(rows, head, o):
        lanes = slice(head * dv, (head + 1) * dv)
        og = og_ref[rows, lanes].astype(F32)
        if kind == "hgrn":
            y = _head_norm(o * _sigmoid(og), nw_ref[:, lanes])
        else:
            y = _head_norm(o, nw_ref[:, lanes]) * (og * _sigmoid(og))
        o_ref[rows, lanes] = y.astype(o_ref.dtype)

    spans = []

    gates, b_of = [], []
    for g0 in range(0, n_groups, 2):
        pair = [load_group(slice(0, tb), gi, with_v=False) for gi in range(g0, min(g0 + 2, n_groups))]
        b_pair = _cumsum_rows(tri_blk, jnp.concatenate([g for _, _, g, _ in pair], axis=1))
        gates += pair
        b_of += [b_pair[:, j * LANE:(j + 1) * LANE] for j in range(len(pair))]
        yield

    def phase1(gi):
        glanes = slice(gi * LANE, (gi + 1) * LANE)
        q, k, _, _ = gates[gi]
        b = b_of[gi]
        for c in range(n_chunks):
            sl = slice(c * length, (c + 1) * length)
            bc, qc, kc = b[sl], q[sl], k[sl]
            b_mid = bc[length // 2:length // 2 + 1, :]
            b_last = bc[length - 1:length, :]
            spans.append(jnp.maximum(bc[0:1, :] - b_mid, b_mid - b_last))
            qt = qc * jnp.exp2(bc - b_mid)
            q_in = qc * jnp.exp2(bc)
            kt_scr[sl, glanes] = (kc * jnp.exp2(b_mid - bc)).astype(BF16)
            kst_scr[sl, xcols(gi, c)] = (kc * jnp.exp2(b_last - bc)).astype(BF16)
            dec_scr[c:c + 1, glanes] = jnp.exp2(b_last)
            for h, m in enumerate(masks):
                head = gi * n_sub + h
                qt_scr[sl, head * LANE:(head + 1) * LANE] = (qt if m is None else qt * m).astype(BF16)
                qin_scr[sl, head * LANE:(head + 1) * LANE] = (q_in if m is None else q_in * m).astype(BF16)

    starts = {}

    def phase2(gi):
        glanes = slice(gi * LANE, (gi + 1) * LANE)
        kt = kt_scr[:, glanes]
        kst_x = kst_scr[:, gi * xw:(gi + 1) * xw]
        upd = None
        for h, m in enumerate(masks):
            head = gi * n_sub + h
            hl = slice(head * LANE, (head + 1) * LANE)
            p_scr[head] = jnp.where(causal, _dot_nt(qt_scr[:, hl], kt), 0.0).astype(BF16)
            u = _dot_tn(v_ref[:, hl], kst_x)
            if m is not None:
                u = u * jnp.concatenate([m] * n_chunks, axis=1)
            upd = u if upd is None else upd + u
        st = st_scr[gi]
        starts[gi] = []
        for c in range(n_chunks):
            starts[gi].append(st.astype(BF16))
            st = st * dec_scr[c:c + 1, glanes] + upd[:, c * LANE:(c + 1) * LANE]
        st_scr[gi] = st

    def phase3(gi):
        for h in range(n_sub):
            head = gi * n_sub + h
            hl = slice(head * LANE, (head + 1) * LANE)
            o_intra = _dot(p_scr[head], v_ref[:, hl])
            parts = [_dot_nt(qin_scr[c * length:(c + 1) * length, hl], starts[gi][c])
                     for c in range(n_chunks)]
            o_inter = parts[0] if n_chunks == 1 else jnp.concatenate(parts, axis=0)
            inter_scr[:, hl] = o_inter
            finish(slice(0, tb), head, o_intra + o_inter)

    for gi in range(n_groups):
        phase1(gi)
        phase2(gi)
        yield
    for gi in range(n_groups):
        phase3(gi)
        yield
    span_vec = functools.reduce(jnp.maximum, spans)

    def redo_intra():
        tri = tri_blk[0:length, 0:length]

        def chunk(c, carry):
            rows = pl.ds(pl.multiple_of(c * length, length), length)
            for gi in range(n_groups):
                q, k, g, vs = load_group(rows, gi)
                intra = _glr_intra_exact(q, k, g, vs, masks, tmp_scr, tri)
                for h in range(n_sub):
                    head = gi * n_sub + h
                    finish(rows, head, intra[h] + inter_scr[rows, head * dv:(head + 1) * dv])
            return carry

        lax.fori_loop(0, n_chunks, chunk, 0)

    deferred.append((jnp.max(span_vec) > SAFE_DECAY_SPAN * LOG2E, redo_intra))
    yield

    @_when(t == pl.num_programs(1) - 1, single_block)
    def _():
        for gi in range(n_groups):
            sout_ref[gi] = st_scr[gi].T


def _time_block(kind, t_len):
    tb = _pick_tile(t_len, 256)
    length = min(CHUNK, t_len) if kind == "hgrn" else tb
    assert tb % length == 0
    return length, tb


def _glr(kind, layer, bsz, t_len, ins, s0):
    length, tb = _time_block(kind, t_len)
    n_groups, n_sub = (4, 1) if kind == "hgrn" else (2, 2)
    n_heads = n_groups * n_sub
    bw = n_heads * LANE
    has_state = s0 is not None
    single_block = t_len == tb
    rows = _rows_per_step(bsz, single_block)

    def tok(width, blk):
        return pl.BlockSpec((rows, tb, width), lambda b, t: (b, t, blk))

    def full(shape):
        return pl.BlockSpec(shape, lambda b, t: (0,) * len(shape))

    if kind == "hgrn":
        a, f, lbl, nw = ins
        a = a.reshape(bsz, t_len, -1)
        f = f.reshape(bsz, t_len, -1)
        args = [a, a, a, f, lbl, nw]
        in_specs = [tok(bw, 0), tok(bw, 1), tok(bw, 2), tok(bw, 0), full(lbl.shape),
                    pl.BlockSpec((None, 1, bw), lambda b, t: (layer, 0, 0))]
        row_flags = [True] * 4 + [False] * 2
    else:
        c, s, w2, ba, nw = ins
        c = c.reshape(bsz, t_len, -1)
        s = s.reshape(bsz, t_len, -1)
        kw = n_groups * LANE
        args = [c, c, c, c, s, w2, ba, nw]
        in_specs = [tok(kw, 0), tok(kw, 1), tok(bw, 1), tok(bw, 2), tok(LANE, 0),
                    pl.BlockSpec((None, LANE, kw), lambda b, t: (layer, 0, 0)),
                    pl.BlockSpec((None, 1, kw), lambda b, t: (layer, 0, 0)),
                    pl.BlockSpec((None, 1, bw), lambda b, t: (layer, 0, 0))]
        row_flags = [True] * 5 + [False] * 3
    st_spec = pl.BlockSpec((rows, n_groups, LANE, LANE), lambda b, t: (b, 0, 0, 0))
    if has_state:
        args.append(s0)
        in_specs.append(st_spec)
        row_flags.append(True)
    scratch = [((n_groups, LANE, LANE), F32),
               ((tb, bw), BF16),
               ((tb, bw), BF16),
               ((tb, n_groups * LANE), BF16),
               ((tb, n_groups * LANE * (tb // length)), BF16),
               ((max(tb // length, 8), n_groups * LANE), F32),
               ((n_heads, tb, tb), BF16),
               ((tb, bw), F32),
               ((2 + n_sub, length, LANE), F32)]
    row_flags += [True] * (2 + len(scratch))
    body = functools.partial(_glr_kernel, kind=kind, layer=layer, length=length,
                             n_chunks=tb // length, n_groups=n_groups, n_sub=n_sub,
                             has_state=has_state, single_block=single_block)
    o, s_new = pl.pallas_call(
        _per_row(body, row_flags, rows),
        grid=(bsz // rows, t_len // tb),
        in_specs=in_specs,
        out_specs=[pl.BlockSpec((rows, tb, bw), lambda b, t: (b, t, 0)), st_spec],
        out_shape=[jax.ShapeDtypeStruct((bsz, t_len, bw), BF16),
                   jax.ShapeDtypeStruct((bsz, n_groups, LANE, LANE), F32)],
        scratch_shapes=[pltpu.VMEM((rows,) + shape, dt) for shape, dt in scratch],
        compiler_params=_params(("parallel", "arbitrary")),
        name=kind,
    )(*args)
    return o.reshape(bsz * t_len, bw), s_new


def _mlstm_kernel(*refs, length, n_heads, has_state, single_block, deferred):
    q_ref, k_ref, v_ref, og_ref, s_ref, bias_ref, nw_ref = refs[:7]
    refs = refs[7:]
    if has_state:
        c0_ref, n0_ref, m0_ref = refs[:3]
        refs = refs[3:]
    o_ref, cout_ref, nout_ref, mout_ref, cn_scr, m_scr, s_scr = refs
    t = pl.program_id(1)
    dk = dv = LANE
    scale = float(dk) ** -0.5
    lane = lax.broadcasted_iota(jnp.int32, (1, LANE), 1)

    del deferred

    @_when(t == 0, single_block)
    def _():
        for h in range(n_heads):
            if has_state:
                n_cols = jnp.broadcast_to(n0_ref[h:h + 1, :], (dv, dk)).T
                cn_scr[h] = jnp.concatenate([c0_ref[h], n_cols], axis=1)
                m_scr[h:h + 1, :] = jnp.broadcast_to(m0_ref[:, h:h + 1], (1, LANE)) * LOG2E
            else:
                cn_scr[h] = jnp.zeros((dk, 2 * dv), F32)
                m_scr[h:h + 1, :] = jnp.zeros((1, LANE), F32)

    yield
    causal, tri = _chunk_consts(length)
    sel = jnp.where(lax.broadcasted_iota(jnp.int32, (8, LANE), 0)
                    == lax.broadcasted_iota(jnp.int32, (8, LANE), 1), 1.0, 0.0).astype(BF16)
    reps = -(-length // LANE)

    x = s_ref[...] + bias_ref[...]
    y = jnp.where(lane < n_heads, x, _cumsum_rows(tri, _log_sigmoid(x))) * LOG2E
    y_t = _rows_to_lanes(sel, y)
    ones = jnp.ones((length, dv), BF16)
    m_row = jnp.zeros((1, LANE), F32)
    yield

    def gate_cols(h):
        i_c = jnp.broadcast_to(y[:, h:h + 1], (length, LANE))
        b_c = jnp.broadcast_to(y[:, n_heads + h:n_heads + h + 1], (length, LANE))
        return i_c, b_c, b_c[length - 1:length, :]

    def intra_logits(h, b_c):
        z_r = y_t[h:h + 1, :] - y_t[n_heads + h:n_heads + h + 1, :]
        b_cw = jnp.concatenate([b_c] * reps, axis=1)[:, :length]
        return jnp.where(causal, b_cw + z_r, -jnp.inf)

    staged = []
    for h in range(n_heads):
        lanes = slice(h * dk, (h + 1) * dk)
        q = q_ref[:, lanes]
        kf = k_ref[:, lanes].astype(F32) * scale
        v_ext = jnp.concatenate([v_ref[:, lanes], ones], axis=1)
        i_c, b_c, b_last = gate_cols(h)
        s_scr[h] = _dot_nt(q, kf.astype(BF16))
        e_intra = b_last - b_c + i_c
        e_max = jnp.max(e_intra, axis=0, keepdims=True)
        kw = jnp.exp2(e_intra - e_max) * kf
        u_ext = _dot_tn(kw.astype(BF16), v_ext)
        m_old = m_scr[h:h + 1, :]
        cn_old = cn_scr[h]
        qc_ext = _dot(q, cn_old.astype(BF16))
        e_inter = b_last + m_old
        m_new = jnp.maximum(e_inter, e_max)
        w_c = jnp.exp2(e_inter - m_new)
        f_s = jnp.exp2(e_max - m_new)
        cn_scr[h] = (jnp.concatenate([w_c, w_c], axis=1) * cn_old
                     + jnp.concatenate([f_s, f_s], axis=1) * u_ext)
        m_scr[h:h + 1, :] = m_new
        m_row = jnp.where(lane == h, m_new, m_row)
        staged.append((v_ext, b_c + m_old, qc_ext))
        yield

    for h in range(n_heads):
        lanes = slice(h * dk, (h + 1) * dk)
        v_ext, d_inter, qc_ext = staged[h]
        d_intra = intra_logits(h, gate_cols(h)[1])
        a_t = jnp.max(d_intra, axis=1, keepdims=True)
        p = jnp.exp2(d_intra - a_t) * s_scr[h]
        num_ext = _dot(p.astype(BF16), v_ext)
        m_t = jnp.maximum(d_inter, a_t)
        f_intra = jnp.exp2(a_t - m_t)
        w_inter = jnp.exp2(d_inter - m_t)
        tot = (jnp.concatenate([f_intra, f_intra], axis=1) * num_ext
               + jnp.concatenate([w_inter, w_inter], axis=1) * qc_ext)
        hid = tot[:, :dv] / jnp.maximum(jnp.abs(tot[:, dv:]), jnp.exp2(-m_t))
        og = og_ref[:, lanes].astype(F32)
        y_out = _sigmoid(og) * _head_norm(hid, nw_ref[:, lanes])
        o_ref[:, lanes] = y_out.astype(o_ref.dtype)
        yield

    @_when(t == pl.num_programs(1) - 1, single_block)
    def _():
        for h in range(n_heads):
            cn = cn_scr[h]
            cout_ref[h] = cn[:, :dv]
            nout_ref[h:h + 1, :] = cn[:, dv:].T[0:1, :]
        mout_ref[...] = m_row * (1.0 / LOG2E)


def _mlstm(layer, bsz, t_len, p_b, p_s, bias_row, norm_w, states):
    length = tb = _pick_tile(t_len, MLSTM_CHUNK)
    n_heads = p_b.shape[1] // (4 * LANE)
    bw = n_heads * LANE
    has_state = states is not None
    p_b = p_b.reshape(bsz, t_len, -1)
    p_s = p_s.reshape(bsz, t_len, -1)

    single_block = t_len == tb
    rows = _rows_per_step(bsz, single_block)

    def tok(width, blk):
        return pl.BlockSpec((rows, tb, width), lambda b, t: (b, t, blk))

    c_spec = pl.BlockSpec((rows, n_heads, LANE, LANE), lambda b, t: (b, 0, 0, 0))
    n_spec = pl.BlockSpec((rows, n_heads, LANE), lambda b, t: (b, 0, 0))
    m_spec = pl.BlockSpec((rows, 1, LANE), lambda b, t: (b, 0, 0))
    args = [p_b, p_b, p_b, p_b, p_s, bias_row, norm_w]
    in_specs = [tok(bw, 0), tok(bw, 1), tok(bw, 2), tok(bw, 3), tok(LANE, 0),
                pl.BlockSpec((None, 1, LANE), lambda b, t: (layer, 0, 0)),
                pl.BlockSpec((None, 1, bw), lambda b, t: (layer, 0, 0))]
    row_flags = [True] * 5 + [False] * 2
    if has_state:
        args += list(states)
        in_specs += [c_spec, n_spec, m_spec]
        row_flags += [True] * 3
    scratch = [((n_heads, LANE, 2 * LANE), F32), ((n_heads, LANE), F32), ((n_heads, tb, tb), F32)]
    row_flags += [True] * (4 + len(scratch))
    body = functools.partial(_mlstm_kernel, length=length, n_heads=n_heads, has_state=has_state,
                             single_block=single_block)
    o, c_new, n_new, m_new = pl.pallas_call(
        _per_row(body, row_flags, rows),
        grid=(bsz // rows, t_len // tb),
        in_specs=in_specs,
        out_specs=[pl.BlockSpec((rows, tb, bw), lambda b, t: (b, t, 0)), c_spec, n_spec, m_spec],
        out_shape=[jax.ShapeDtypeStruct((bsz, t_len, bw), BF16),
                   jax.ShapeDtypeStruct((bsz, n_heads, LANE, LANE), F32),
                   jax.ShapeDtypeStruct((bsz, n_heads, LANE), F32),
                   jax.ShapeDtypeStruct((bsz, 1, LANE), F32)],
        scratch_shapes=[pltpu.VMEM((rows,) + shape, dt) for shape, dt in scratch],
        compiler_params=_params(("parallel", "arbitrary")),
        name="mlstm",
    )(*args)
    return o.reshape(bsz * t_len, bw), c_new, n_new, m_new


def _prep_weights(w_in, gla_w_a2, mlstm_gate_bias, heads):
    (ah, adk, adv), (bh, bdk, bdv), (ch, cdk, cdv), rank = heads
    sizes = (ah * adk, ah * adk, ah * adv, ah * adv,
             bh * bdk, bh * bdk, bh * bdv, bh * bdv, 2 * bh,
             ch * cdk, ch * cdk, ch * cdv, ch * cdv, rank)
    offs = [0]
    for s in sizes:
        offs.append(offs[-1] + s)
    col = lambda i: w_in[:, :, offs[i]:offs[i + 1]]
    (a_q, a_f, a_i, a_g, b_q, b_k, b_v, b_o, b_if, c_q, c_k, c_v, c_g, c_lr) = [col(i) for i in range(14)]
    pad = jnp.zeros(w_in.shape[:2] + (LANE - 2 * bh - rank,), w_in.dtype)
    groups = ([a_q, a_i, a_g], [a_f], [b_q, b_k, b_v, b_o], [c_q, c_k, c_v, c_g], [b_if, c_lr, pad])
    widths = tuple(sum(g.shape[-1] for g in grp) for grp in groups)
    w_cat = jnp.concatenate([g for grp in groups for g in grp], axis=-1).astype(BF16)
    w_mg = w_in[:, :, offs[14]:].astype(BF16)
    depth = w_in.shape[0]
    w2_pad = jnp.zeros((depth, LANE, gla_w_a2.shape[-1]), F32)
    w2_pad = w2_pad.at[:, 2 * bh:2 * bh + rank, :].set(gla_w_a2).astype(BF16)
    bias_row = jnp.zeros((depth, 1, LANE), F32).at[:, 0, :2 * bh].set(mlstm_gate_bias)
    return w_cat, widths, w_mg, w2_pad, bias_row


def _run_trunk(x, states, wts, heads):
    (ffn1_norm, ffn1_up, ffn1_down, mix_norm, w_cat, widths, w_mg, hgrn_lb_logits, hgrn_norm,
     bias_row, mlstm_norm, w2_pad, gla_b_a, gla_norm, w_branch, w_out,
     ffn2_norm, ffn2_up, ffn2_down, final_norm) = wts
    (ah, adk, adv), (bh, bdk, bdv), (ch, cdk, cdv), _ = heads
    bsz, t_len, d = x.shape
    depth = ffn1_norm.shape[0]
    x = x.reshape(bsz * t_len, d)
    new_states = ([], [], [], [], [])
    for l in range(depth):
        x = _ffn(x, ffn1_norm, ffn1_up, ffn1_down, l)
        p_a, p_f, p_b, p_c, p_s = _inproj(x, mix_norm, w_cat, l, widths)
        if states is None:
            st_a = st_b = st_c = None
        else:
            s_hgrn, c_ml, n_ml, m_ml, s_gla = states
            st_a = s_hgrn[l]
            m_pad = jnp.zeros((bsz, 1, LANE), F32).at[:, 0, :bh].set(m_ml[l])
            st_b = (c_ml[l], n_ml[l], m_pad)
            st_c = s_gla[l].reshape(bsz, ch // 2, 2 * cdk, cdv)
        o_a, s_a = _glr("hgrn", l, bsz, t_len, (p_a, p_f, hgrn_lb_logits, hgrn_norm), st_a)
        o_b, c_new, n_new, m_new = _mlstm(l, bsz, t_len, p_b, p_s, bias_row, mlstm_norm, st_b)
        o_c, s_c = _glr("gla", l, bsz, t_len, (p_c, p_s, w2_pad, gla_b_a, gla_norm), st_c)
        x = _merge(x, mix_norm, o_a, o_b, o_c, w_mg, w_branch, w_out, l)
        x = _ffn(x, ffn2_norm, ffn2_up, ffn2_down, l,
                 final_w=final_norm if l == depth - 1 else None)
        for acc, s in zip(new_states, (s_a, c_new, n_new, m_new[:, 0, :bh],
                                       s_c.reshape(bsz, ch, cdk, cdv))):
            acc.append(s)
    return x.reshape(bsz, t_len, d), tuple(jnp.stack(acc) for acc in new_states)


def kernel(x_prompt, x_sample, state_hgrn, state_mlstm_c, state_mlstm_n, state_mlstm_m, state_gla, ffn1_norm, ffn1_w_up, ffn1_w_down, mix_norm, w_in, hgrn_lb_logits, hgrn_norm, mlstm_gate_bias, mlstm_norm, gla_w_a2, gla_b_a, gla_norm, w_branch, w_out, ffn2_norm, ffn2_w_up, ffn2_w_down, final_norm):
    heads = (state_hgrn.shape[2:], state_mlstm_c.shape[2:], state_gla.shape[2:], gla_w_a2.shape[1])
    assert heads[0][1:] == (LANE, LANE) and heads[1][1:] == (LANE, LANE)
    assert heads[2][1:] == (LANE // 2, LANE) and heads[2][0] % 2 == 0
    w_cat, widths, w_mg, w2_pad, bias_row = _prep_weights(w_in, gla_w_a2, mlstm_gate_bias, heads)
    row = lambda a: a[:, None, :]
    wts = (row(ffn1_norm), ffn1_w_up.astype(BF16), ffn1_w_down.astype(BF16), row(mix_norm),
           w_cat, widths, w_mg, hgrn_lb_logits, row(hgrn_norm), bias_row, row(mlstm_norm),
           w2_pad, row(gla_b_a), row(gla_norm), w_branch.astype(BF16), w_out.astype(BF16),
           row(ffn2_norm), ffn2_w_up.astype(BF16), ffn2_w_down.astype(BF16), final_norm[None, :])
    y_p, st_p = _run_trunk(x_prompt, None, wts, heads)
    sample_states = (state_hgrn, state_mlstm_c, state_mlstm_n, state_mlstm_m, state_gla)
    y_s, st_s = _run_trunk(x_sample, sample_states, wts, heads)
    return (y_p, y_s) + st_p + st_s
```

```python
import functools
from typing import Any, NamedTuple

import jax
import jax.numpy as jnp
from jax import lax
from jax.experimental import pallas as pl
from jax.experimental.pallas import tpu as pltpu

F32 = jnp.float32
BF16 = jnp.bfloat16

EPS = 1e-6
CHUNK = 64
MLSTM_CHUNK = 256
GLA_GATE_TEMP = 16.0
N_BRANCH = 3
LANE = 128
V7X_VMEM_BYTES = 64 * 1024 * 1024
VMEM_LIMIT = V7X_VMEM_BYTES - 8 * 1024 * 1024
SAFE_DECAY_SPAN = 60.0
LOG2E = 1.4426950408889634

_NT = (((1,), (1,)), ((), ()))
_TN = (((0,), (0,)), ((), ()))


def _dot(a, b):
    return jnp.dot(a, b, preferred_element_type=F32)


def _dot_nt(a, b):
    return lax.dot_general(a, b, _NT, preferred_element_type=F32)


def _dot_tn(a, b):
    return lax.dot_general(a, b, _TN, preferred_element_type=F32)


def _rms(x, w):
    return x * lax.rsqrt(jnp.mean(x * x, axis=-1, keepdims=True) + EPS) * w


def _sigmoid(x):
    return 1.0 / (1.0 + jnp.exp2(x * (-LOG2E)))


def _log_sigmoid(x):
    return jnp.minimum(x, 0.0) - jnp.log(1.0 + jnp.exp(-jnp.abs(x)))


def _split3(x):
    hi = x.astype(BF16)
    r = x - hi.astype(F32)
    mid = r.astype(BF16)
    lo = (r - mid.astype(F32)).astype(BF16)
    return hi, mid, lo


def _cumsum_rows(tri, g):
    hi, mid, lo = _split3(g)
    return (_dot(tri, hi) + _dot(tri, mid)) + _dot(tri, lo)


def _rows_to_lanes(sel, x):
    hi, mid, lo = _split3(x)
    return (_dot_nt(sel, hi) + _dot_nt(sel, mid)) + _dot_nt(sel, lo)


def _pick_tile(n, cap):
    t = min(n, cap)
    while n % t or (t % 8 and t != n):
        t -= 1
    return t


def _row_parts(tm):
    n = 2 if tm % 512 == 0 and tm >= 1024 else 1
    return [slice(r * (tm // n), (r + 1) * (tm // n)) for r in range(n)]


def _params(sem):
    return pltpu.CompilerParams(dimension_semantics=sem, vmem_limit_bytes=VMEM_LIMIT)


def _ffn_kernel(x_ref, nw_ref, wup_ref, wd_ref, *rest, d_ff, tf, tn, final):
    if final:
        fw_ref, o_ref, act_scr = rest
    else:
        o_ref, act_scr = rest
    d = x_ref.shape[1]
    for rows in _row_parts(x_ref.shape[0]):
        h = _rms(x_ref[rows, :], nw_ref[...]).astype(BF16)
        for c in range(0, d_ff, tf):
            g = _dot(h, wup_ref[:, c:c + tf])
            u = _dot(h, wup_ref[:, d_ff + c:d_ff + c + tf])
            act_scr[rows, c:c + tf] = (g * _sigmoid(g) * u).astype(BF16)
        for c in range(0, d, tn):
            o_ref[rows, c:c + tn] = (x_ref[rows, c:c + tn]
                                     + 0.5 * _dot(act_scr[rows, :], wd_ref[:, c:c + tn]))
        if final:
            o_ref[rows, :] = _rms(o_ref[rows, :], fw_ref[...])


def _ffn(x, norm_w, w_up, w_down, layer, final_w=None):
    n, d = x.shape
    d_ff = w_down.shape[1]
    tm = _pick_tile(n, 1024)
    tf = tn = 256
    assert d_ff % tf == 0 and d % tn == 0
    final = final_w is not None
    resident = pl.Buffered(1)
    in_specs = [
        pl.BlockSpec((tm, d), lambda i: (i, 0)),
        pl.BlockSpec((None, 1, d), lambda i: (layer, 0, 0)),
        pl.BlockSpec((None, d, 2 * d_ff), lambda i: (layer, 0, 0), pipeline_mode=resident),
        pl.BlockSpec((None, d_ff, d), lambda i: (layer, 0, 0), pipeline_mode=resident),
    ]
    args = [x, norm_w, w_up, w_down]
    if final:
        in_specs.append(pl.BlockSpec((1, d), lambda i: (0, 0)))
        args.append(final_w)
    return pl.pallas_call(
        functools.partial(_ffn_kernel, d_ff=d_ff, tf=tf, tn=tn, final=final),
        grid=(n // tm,),
        in_specs=in_specs,
        out_specs=pl.BlockSpec((tm, d), lambda i: (i, 0)),
        out_shape=jax.ShapeDtypeStruct((n, d), F32),
        scratch_shapes=[pltpu.VMEM((tm, d_ff), BF16)],
        compiler_params=_params(("parallel",)),
        name="ffn",
    )(*args)


def _inproj_kernel(x_ref, nw_ref, w_ref, oa_ref, of_ref, ob_ref, oc_ref, os_ref, *, widths):
    for rows in _row_parts(x_ref.shape[0]):
        h = _rms(x_ref[rows, :], nw_ref[...]).astype(BF16)
        c0 = 0
        for ref, width in zip((oa_ref, of_ref, ob_ref, oc_ref, os_ref), widths):
            step = min(width, 512)
            for c in range(0, width, step):
                ref[rows, c:c + step] = _dot(h, w_ref[:, c0 + c:c0 + c + step]).astype(ref.dtype)
            c0 += width


def _inproj(x, norm_w, w_cat, layer, widths):
    n, d = x.shape
    tm = _pick_tile(n, 1024)
    dtypes = (BF16, F32, BF16, BF16, F32)
    return pl.pallas_call(
        functools.partial(_inproj_kernel, widths=widths),
        grid=(n // tm,),
        in_specs=[
            pl.BlockSpec((tm, d), lambda i: (i, 0)),
            pl.BlockSpec((None, 1, d), lambda i: (layer, 0, 0)),
            pl.BlockSpec((None, d, sum(widths)), lambda i: (layer, 0, 0),
                         pipeline_mode=pl.Buffered(1)),
        ],
        out_specs=[pl.BlockSpec((tm, w), lambda i: (i, 0)) for w in widths],
        out_shape=[jax.ShapeDtypeStruct((n, w), dt) for w, dt in zip(widths, dtypes)],
        compiler_params=_params(("parallel",)),
        name="inproj",
    )(x, norm_w, w_cat)


def _merge_kernel(x_ref, nw_ref, oa_ref, ob_ref, oc_ref, wm_ref, wb_ref, wo_ref, o_ref):
    d = x_ref.shape[1]
    for rows in _row_parts(x_ref.shape[0]):
        x = x_ref[rows, :]
        h = _rms(x, nw_ref[...]).astype(BF16)
        merged = None
        for nb, br_ref in enumerate((oa_ref, ob_ref, oc_ref)):
            gate = _sigmoid(_dot(h, wm_ref[:, nb * d:(nb + 1) * d]))
            term = gate * _dot(br_ref[rows, :], wb_ref[nb])
            merged = term if merged is None else merged + term
        o_ref[rows, :] = x + _dot(merged.astype(BF16), wo_ref[...])


def _merge(x, norm_w, o_a, o_b, o_c, w_mg, w_branch, w_out, layer):
    n, d = x.shape
    bw = o_a.shape[1]
    tm = _pick_tile(n, 1024)
    br_spec = pl.BlockSpec((tm, bw), lambda i: (i, 0))
    resident = pl.Buffered(1)
    return pl.pallas_call(
        _merge_kernel,
        grid=(n // tm,),
        in_specs=[
            pl.BlockSpec((tm, d), lambda i: (i, 0)),
            pl.BlockSpec((None, 1, d), lambda i: (layer, 0, 0)),
            br_spec, br_spec, br_spec,
            pl.BlockSpec((None, d, N_BRANCH * d), lambda i: (layer, 0, 0), pipeline_mode=resident),
            pl.BlockSpec((None, N_BRANCH, bw, d), lambda i: (layer, 0, 0, 0),
                         pipeline_mode=resident),
            pl.BlockSpec((None, d, d), lambda i: (layer, 0, 0), pipeline_mode=resident),
        ],
        out_specs=pl.BlockSpec((tm, d), lambda i: (i, 0)),
        out_shape=jax.ShapeDtypeStruct((n, d), F32),
        compiler_params=_params(("parallel",)),
        name="merge",
    )(x, norm_w, o_a, o_b, o_c, w_mg, w_branch, w_out)


def _chunk_consts(length, n_chunks=1):
    tb = length * n_chunks
    rb = lax.broadcasted_iota(jnp.int32, (tb, tb), 0)
    cb = lax.broadcasted_iota(jnp.int32, (tb, tb), 1)
    causal = rb >= cb
    for i in range(1, n_chunks):
        causal = causal & ((rb < i * length) | (cb >= i * length))
    return causal, jnp.where(causal, 1.0, 0.0).astype(BF16)


def _head_norm(o, w):
    return o * lax.rsqrt(jnp.mean(o * o, axis=-1, keepdims=True) + EPS) * w


def _glr_intra_exact(q, k, g, vs, masks, tmp_ref, tri):
    length = q.shape[0]
    b = _cumsum_rows(tri, g)
    tmp_ref[0, 0:length, :] = k
    tmp_ref[1, 0:length, :] = b
    for h, v in enumerate(vs):
        tmp_ref[2 + h, 0:length, :] = v
    t_idx = lax.broadcasted_iota(jnp.int32, (length, 1), 0)

    def body(s, accs):
        k_s = tmp_ref[0, pl.ds(s, 1), :]
        b_s = tmp_ref[1, pl.ds(s, 1), :]
        a = q * k_s * jnp.exp2(jnp.minimum(b - b_s, 0.0))
        new = []
        for h, m in enumerate(masks):
            ah = a if m is None else a * m
            col = jnp.where(t_idx >= s, jnp.sum(ah, axis=1, keepdims=True), 0.0)
            new.append(accs[h] + col * tmp_ref[2 + h, pl.ds(s, 1), :])
        return tuple(new)

    init = tuple(jnp.zeros((length, v.shape[1]), F32) for v in vs)
    return lax.fori_loop(0, length, body, init)


def _lane_masks(n_sub):
    if n_sub == 1:
        return [None]
    lane = lax.broadcasted_iota(jnp.int32, (1, LANE), 1)
    w = LANE // n_sub
    return [jnp.where((lane >= h * w) & (lane < (h + 1) * w), 1.0, 0.0) for h in range(n_sub)]


def _when(cond, always):
    return (lambda f: f()) if always else pl.when(cond)


class _RecProgram(NamedTuple):
    body: Any
    args: list
    in_specs: list
    in_rows: list
    out_specs: list
    out_shape: list
    scratch: list


def _rows_per_step(bsz, single_block):
    g = min(bsz, 8 if single_block else 2)
    while bsz % g:
        g -= 1
    return g


def _glr_kernel(*refs, kind, layer, length, n_chunks, n_groups, n_sub, has_state, single_block,
                deferred):
    if kind == "hgrn":
        q_ref, v_ref, og_ref, f_ref, lbl_ref, nw_ref = refs[:6]
        refs = refs[6:]
    else:
        q_ref, k_ref, v_ref, og_ref, s_ref, w2_ref, ba_ref, nw_ref = refs[:8]
        refs = refs[8:]
    if has_state:
        s0_ref, refs = refs[0], refs[1:]
    (o_ref, sout_ref, st_scr, qt_scr, qin_scr, kt_scr, kst_scr, dec_scr, p_scr, inter_scr,
     tmp_scr) = refs
    t = pl.program_id(1)
    n_heads = n_groups * n_sub
    dv = LANE
    tb = length * n_chunks

    xw = n_chunks * LANE

    def xcols(idx, c):
        return slice(idx * xw + c * LANE, idx * xw + (c + 1) * LANE)

    @_when(t == 0, single_block)
    def _():
        for gi in range(n_groups):
            if has_state:
                st_scr[gi] = s0_ref[gi].T
            else:
                st_scr[gi] = jnp.zeros((dv, LANE), F32)
        if n_chunks > 1:
            kst_scr[...] = jnp.zeros_like(kst_scr)

    yield
    causal, tri_blk = _chunk_consts(length, n_chunks)
    masks = _lane_masks(n_sub)

    if kind == "hgrn":
        logits = lbl_ref[...]
        e = jnp.exp(logits - jnp.max(logits, axis=0, keepdims=True))
        p = e / jnp.sum(e, axis=0, keepdims=True)
        lb = jnp.zeros((1, p.shape[1]), F32)
        for j in range(1, layer + 1):
            lb = lb + p[j:j + 1, :]

    def load_group(rows, gi, with_v=True):
        lanes = slice(gi * LANE, (gi + 1) * LANE)
        if kind == "hgrn":
            aq = q_ref[rows, lanes].astype(F32)
            q = aq * _sigmoid(aq)
            x = f_ref[rows, lanes]
            lbh = lb[:, lanes]
            forget = lbh + (1.0 - lbh) * _sigmoid(x)
            g = jnp.log(forget) * LOG2E
            k = 1.0 - forget
        else:
            q = q_ref[rows, lanes].astype(F32) * (float(LANE // n_sub) ** -0.5)
            k = k_ref[rows, lanes].astype(F32)
            pre = _dot(s_ref[rows, :].astype(BF16), w2_ref[:, lanes]) + ba_ref[:, lanes]
            g = _log_sigmoid(pre) * (LOG2E / GLA_GATE_TEMP)
        vs = [v_ref[rows, (gi * n_sub + h) * dv:(gi * n_sub + h + 1) * dv].astype(F32)
              for h in range(n_sub)] if with_v else None
        return q, k, g, vs

    def finish(rows, head, o):
        lanes = slice(head * dv, (head + 1) * dv)
        og = og_ref[rows, lanes].astype(F32)
        if kind == "hgrn":
            y = _head_norm(o * _sigmoid(og), nw_ref[:, lanes])
        else:
            y = _head_norm(o, nw_ref[:, lanes]) * (og * _sigmoid(og))
        o_ref[rows, lanes] = y.astype(o_ref.dtype)

    spans = []

    gates, b_of = [], []
    for g0 in range(0, n_groups, 2):
        pair = [load_group(slice(0, tb), gi, with_v=False) for gi in range(g0, min(g0 + 2, n_groups))]
        b_pair = _cumsum_rows(tri_blk, jnp.concatenate([g for _, _, g, _ in pair], axis=1))
        gates += pair
        b_of += [b_pair[:, j * LANE:(j + 1) * LANE] for j in range(len(pair))]
        yield

    def phase1(gi):
        glanes = slice(gi * LANE, (gi + 1) * LANE)
        q, k, _, _ = gates[gi]
        b = b_of[gi]
        for c in range(n_chunks):
            sl = slice(c * length, (c + 1) * length)
            bc, qc, kc = b[sl], q[sl], k[sl]
            b_mid = bc[length // 2:length // 2 + 1, :]
            b_last = bc[length - 1:length, :]
            spans.append(jnp.maximum(bc[0:1, :] - b_mid, b_mid - b_last))
            qt = qc * jnp.exp2(bc - b_mid)
            q_in = qc * jnp.exp2(bc)
            kt_scr[sl, glanes] = (kc * jnp.exp2(b_mid - bc)).astype(BF16)
            kst_scr[sl, xcols(gi, c)] = (kc * jnp.exp2(b_last - bc)).astype(BF16)
            dec_scr[c:c + 1, glanes] = jnp.exp2(b_last)
            for h, m in enumerate(masks):
                head = gi * n_sub + h
                qt_scr[sl, head * LANE:(head + 1) * LANE] = (qt if m is None else qt * m).astype(BF16)
                qin_scr[sl, head * LANE:(head + 1) * LANE] = (q_in if m is None else q_in * m).astype(BF16)

    starts = {}

    def phase2(gi):
        glanes = slice(gi * LANE, (gi + 1) * LANE)
        kt = kt_scr[:, glanes]
        kst_x = kst_scr[:, gi * xw:(gi + 1) * xw]
        upd = None
        for h, m in enumerate(masks):
            head = gi * n_sub + h
            hl = slice(head * LANE, (head + 1) * LANE)
            p_scr[head] = jnp.where(causal, _dot_nt(qt_scr[:, hl], kt), 0.0).astype(BF16)
            u = _dot_tn(v_ref[:, hl], kst_x)
            if m is not None:
                u = u * jnp.concatenate([m] * n_chunks, axis=1)
            upd = u if upd is None else upd + u
        st = st_scr[gi]
        starts[gi] = []
        for c in range(n_chunks):
            starts[gi].append(st.astype(BF16))
            st = st * dec_scr[c:c + 1, glanes] + upd[:, c * LANE:(c + 1) * LANE]
        st_scr[gi] = st

    def phase3(gi):
        for h in range(n_sub):
            head = gi * n_sub + h
            hl = slice(head * LANE, (head + 1) * LANE)
            o_intra = _dot(p_scr[head], v_ref[:, hl])
            parts = [_dot_nt(qin_scr[c * length:(c + 1) * length, hl], starts[gi][c])
                     for c in range(n_chunks)]
            o_inter = parts[0] if n_chunks == 1 else jnp.concatenate(parts, axis=0)
            inter_scr[:, hl] = o_inter
            finish(slice(0, tb), head, o_intra + o_inter)

    for gi in range(n_groups):
        phase1(gi)
        phase2(gi)
        yield
    for gi in range(n_groups):
        phase3(gi)
        yield
    span_vec = functools.reduce(jnp.maximum, spans)

    def redo_intra():
        tri = tri_blk[0:length, 0:length]

        def chunk(c, carry):
            rows = pl.ds(pl.multiple_of(c * length, length), length)
            for gi in range(n_groups):
                q, k, g, vs = load_group(rows, gi)
                intra = _glr_intra_exact(q, k, g, vs, masks, tmp_scr, tri)
                for h in range(n_sub):
                    head = gi * n_sub + h
                    finish(rows, head, intra[h] + inter_scr[rows, head * dv:(head + 1) * dv])
            return carry

        lax.fori_loop(0, n_chunks, chunk, 0)

    deferred.append((jnp.max(span_vec) > SAFE_DECAY_SPAN * LOG2E, redo_intra))
    yield

    @_when(t == pl.num_programs(1) - 1, single_block)
    def _():
        for gi in range(n_groups):
            sout_ref[gi] = st_scr[gi].T


def _time_block(kind, t_len):
    tb = _pick_tile(t_len, 256)
    length = min(CHUNK, t_len) if kind == "hgrn" else tb
    assert tb % length == 0
    return length, tb


def _glr(kind, layer, bsz, t_len, ins, s0):
    length, tb = _time_block(kind, t_len)
    n_groups, n_sub = (4, 1) if kind == "hgrn" else (2, 2)
    n_heads = n_groups * n_sub
    bw = n_heads * LANE
    has_state = s0 is not None
    single_block = t_len == tb
    rows = _rows_per_step(bsz, single_block)

    def tok(width, blk):
        return pl.BlockSpec((rows, tb, width), lambda b, t: (b, t, blk))

    def full(shape):
        return pl.BlockSpec(shape, lambda b, t: (0,) * len(shape))

    if kind == "hgrn":
        a, f, lbl, nw = ins
        a = a.reshape(bsz, t_len, -1)
        f = f.reshape(bsz, t_len, -1)
        args = [a, a, a, f, lbl, nw]
        in_specs = [tok(bw, 0), tok(bw, 1), tok(bw, 2), tok(bw, 0), full(lbl.shape),
                    pl.BlockSpec((None, 1, bw), lambda b, t: (layer, 0, 0))]
        row_flags = [True] * 4 + [False] * 2
    else:
        c, s, w2, ba, nw = ins
        c = c.reshape(bsz, t_len, -1)
        s = s.reshape(bsz, t_len, -1)
        kw = n_groups * LANE
        args = [c, c, c, c, s, w2, ba, nw]
        in_specs = [tok(kw, 0), tok(kw, 1), tok(bw, 1), tok(bw, 2), tok(LANE, 0),
                    pl.BlockSpec((None, LANE, kw), lambda b, t: (layer, 0, 0)),
                    pl.BlockSpec((None, 1, kw), lambda b, t: (layer, 0, 0)),
                    pl.BlockSpec((None, 1, bw), lambda b, t: (layer, 0, 0))]
        row_flags = [True] * 5 + [False] * 3
    st_spec = pl.BlockSpec((rows, n_groups, LANE, LANE), lambda b, t: (b, 0, 0, 0))
    if has_state:
        args.append(s0)
        in_specs.append(st_spec)
        row_flags.append(True)
    scratch = [((n_groups, LANE, LANE), F32),
               ((tb, bw), BF16),
               ((tb, bw), BF16),
               ((tb, n_groups * LANE), BF16),
               ((tb, n_groups * LANE * (tb // length)), BF16),
               ((max(tb // length, 8), n_groups * LANE), F32),
               ((n_heads, tb, tb), BF16),
               ((tb, bw), F32),
               ((2 + n_sub, length, LANE), F32)]
    body = functools.partial(_glr_kernel, kind=kind, layer=layer, length=length,
                             n_chunks=tb // length, n_groups=n_groups, n_sub=n_sub,
                             has_state=has_state, single_block=single_block)
    return _RecProgram(
        body=body, args=args, in_specs=in_specs, in_rows=row_flags,
        out_specs=[pl.BlockSpec((rows, tb, bw), lambda b, t: (b, t, 0)), st_spec],
        out_shape=[jax.ShapeDtypeStruct((bsz, t_len, bw), BF16),
                   jax.ShapeDtypeStruct((bsz, n_groups, LANE, LANE), F32)],
        scratch=scratch)


def _mlstm_kernel(*refs, length, n_heads, has_state, single_block, deferred):
    q_ref, k_ref, v_ref, og_ref, s_ref, bias_ref, nw_ref = refs[:7]
    refs = refs[7:]
    if has_state:
        c0_ref, n0_ref, m0_ref = refs[:3]
        refs = refs[3:]
    o_ref, cout_ref, nout_ref, mout_ref, cn_scr, m_scr, s_scr = refs
    t = pl.program_id(1)
    dk = dv = LANE
    scale = float(dk) ** -0.5
    lane = lax.broadcasted_iota(jnp.int32, (1, LANE), 1)

    del deferred

    @_when(t == 0, single_block)
    def _():
        for h in range(n_heads):
            if has_state:
                n_cols = jnp.broadcast_to(n0_ref[h:h + 1, :], (dv, dk)).T
                cn_scr[h] = jnp.concatenate([c0_ref[h], n_cols], axis=1)
                m_scr[h:h + 1, :] = jnp.broadcast_to(m0_ref[:, h:h + 1], (1, LANE)) * LOG2E
            else:
                cn_scr[h] = jnp.zeros((dk, 2 * dv), F32)
                m_scr[h:h + 1, :] = jnp.zeros((1, LANE), F32)

    yield
    causal, tri = _chunk_consts(length)
    sel = jnp.where(lax.broadcasted_iota(jnp.int32, (8, LANE), 0)
                    == lax.broadcasted_iota(jnp.int32, (8, LANE), 1), 1.0, 0.0).astype(BF16)
    reps = -(-length // LANE)

    x = s_ref[...] + bias_ref[...]
    y = jnp.where(lane < n_heads, x, _cumsum_rows(tri, _log_sigmoid(x))) * LOG2E
    y_t = _rows_to_lanes(sel, y)
    ones = jnp.ones((length, dv), BF16)
    m_row = jnp.zeros((1, LANE), F32)
    yield

    def gate_cols(h):
        i_c = jnp.broadcast_to(y[:, h:h + 1], (length, LANE))
        b_c = jnp.broadcast_to(y[:, n_heads + h:n_heads + h + 1], (length, LANE))
        return i_c, b_c, b_c[length - 1:length, :]

    def intra_logits(h, b_c):
        z_r = y_t[h:h + 1, :] - y_t[n_heads + h:n_heads + h + 1, :]
        b_cw = jnp.concatenate([b_c] * reps, axis=1)[:, :length]
        return jnp.where(causal, b_cw + z_r, -jnp.inf)

    staged = []
    for h in range(n_heads):
        lanes = slice(h * dk, (h + 1) * dk)
        q = q_ref[:, lanes]
        kf = k_ref[:, lanes].astype(F32) * scale
        v_ext = jnp.concatenate([v_ref[:, lanes], ones], axis=1)
        i_c, b_c, b_last = gate_cols(h)
        s_scr[h] = _dot_nt(q, kf.astype(BF16))
        e_intra = b_last - b_c + i_c
        e_max = jnp.max(e_intra, axis=0, keepdims=True)
        kw = jnp.exp2(e_intra - e_max) * kf
        u_ext = _dot_tn(kw.astype(BF16), v_ext)
        m_old = m_scr[h:h + 1, :]
        cn_old = cn_scr[h]
        qc_ext = _dot(q, cn_old.astype(BF16))
        e_inter = b_last + m_old
        m_new = jnp.maximum(e_inter, e_max)
        w_c = jnp.exp2(e_inter - m_new)
        f_s = jnp.exp2(e_max - m_new)
        cn_scr[h] = (jnp.concatenate([w_c, w_c], axis=1) * cn_old
                     + jnp.concatenate([f_s, f_s], axis=1) * u_ext)
        m_scr[h:h + 1, :] = m_new
        m_row = jnp.where(lane == h, m_new, m_row)
        staged.append((v_ext, b_c + m_old, qc_ext))
        yield

    for h in range(n_heads):
        lanes = slice(h * dk, (h + 1) * dk)
        v_ext, d_inter, qc_ext = staged[h]
        d_intra = intra_logits(h, gate_cols(h)[1])
        a_t = jnp.max(d_intra, axis=1, keepdims=True)
        p = jnp.exp2(d_intra - a_t) * s_scr[h]
        num_ext = _dot(p.astype(BF16), v_ext)
        m_t = jnp.maximum(d_inter, a_t)
        f_intra = jnp.exp2(a_t - m_t)
        w_inter = jnp.exp2(d_inter - m_t)
        tot = (jnp.concatenate([f_intra, f_intra], axis=1) * num_ext
               + jnp.concatenate([w_inter, w_inter], axis=1) * qc_ext)
        hid = tot[:, :dv] / jnp.maximum(jnp.abs(tot[:, dv:]), jnp.exp2(-m_t))
        og = og_ref[:, lanes].astype(F32)
        y_out = _sigmoid(og) * _head_norm(hid, nw_ref[:, lanes])
        o_ref[:, lanes] = y_out.astype(o_ref.dtype)
        yield

    @_when(t == pl.num_programs(1) - 1, single_block)
    def _():
        for h in range(n_heads):
            cn = cn_scr[h]
            cout_ref[h] = cn[:, :dv]
            nout_ref[h:h + 1, :] = cn[:, dv:].T[0:1, :]
        mout_ref[...] = m_row * (1.0 / LOG2E)


def _mlstm(layer, bsz, t_len, p_b, p_s, bias_row, norm_w, states):
    length = tb = _pick_tile(t_len, MLSTM_CHUNK)
    n_heads = p_b.shape[1] // (4 * LANE)
    bw = n_heads * LANE
    has_state = states is not None
    p_b = p_b.reshape(bsz, t_len, -1)
    p_s = p_s.reshape(bsz, t_len, -1)

    single_block = t_len == tb
    rows = _rows_per_step(bsz, single_block)

    def tok(width, blk):
        return pl.BlockSpec((rows, tb, width), lambda b, t: (b, t, blk))

    c_spec = pl.BlockSpec((rows, n_heads, LANE, LANE), lambda b, t: (b, 0, 0, 0))
    n_spec = pl.BlockSpec((rows, n_heads, LANE), lambda b, t: (b, 0, 0))
    m_spec = pl.BlockSpec((rows, 1, LANE), lambda b, t: (b, 0, 0))
    args = [p_b, p_b, p_b, p_b, p_s, bias_row, norm_w]
    in_specs = [tok(bw, 0), tok(bw, 1), tok(bw, 2), tok(bw, 3), tok(LANE, 0),
                pl.BlockSpec((None, 1, LANE), lambda b, t: (layer, 0, 0)),
                pl.BlockSpec((None, 1, bw), lambda b, t: (layer, 0, 0))]
    row_flags = [True] * 5 + [False] * 2
    if has_state:
        args += list(states)
        in_specs += [c_spec, n_spec, m_spec]
        row_flags += [True] * 3
    scratch = [((n_heads, LANE, 2 * LANE), F32), ((n_heads, LANE), F32), ((n_heads, tb, tb), F32)]
    body = functools.partial(_mlstm_kernel, length=length, n_heads=n_heads, has_state=has_state,
                             single_block=single_block)
    return _RecProgram(
        body=body, args=args, in_specs=in_specs, in_rows=row_flags,
        out_specs=[pl.BlockSpec((rows, tb, bw), lambda b, t: (b, t, 0)), c_spec, n_spec, m_spec],
        out_shape=[jax.ShapeDtypeStruct((bsz, t_len, bw), BF16),
                   jax.ShapeDtypeStruct((bsz, n_heads, LANE, LANE), F32),
                   jax.ShapeDtypeStruct((bsz, n_heads, LANE), F32),
                   jax.ShapeDtypeStruct((bsz, 1, LANE), F32)],
        scratch=scratch)


def _run_rec(progs, bsz, t_len):
    tb = _pick_tile(t_len, 256)
    rows = _rows_per_step(bsz, t_len == tb)
    n_in = [len(p.args) for p in progs]
    n_out = [len(p.out_shape) for p in progs]
    n_scr = [len(p.scratch) for p in progs]

    def kern(*refs):
        ins, outs, scrs = (refs[:sum(n_in)], refs[sum(n_in):sum(n_in) + sum(n_out)],
                           refs[sum(n_in) + sum(n_out):])
        deferred, live = [], []
        for k, p in enumerate(progs):
            mine = (list(ins[sum(n_in[:k]):sum(n_in[:k + 1])])
                    + list(outs[sum(n_out[:k]):sum(n_out[:k + 1])])
                    + list(scrs[sum(n_scr[:k]):sum(n_scr[:k + 1])]))
            flags = list(p.in_rows) + [True] * (n_out[k] + n_scr[k])
            for g in range(rows):
                live.append(p.body(*[r.at[g] if f else r for r, f in zip(mine, flags)],
                                   deferred=deferred))
        while live:
            still = []
            for gen in live:
                try:
                    next(gen)
                    still.append(gen)
                except StopIteration:
                    pass
            live = still
        for cond, fn in deferred:
            pl.when(cond)(fn)

    outs = pl.pallas_call(
        kern,
        grid=(bsz // rows, t_len // tb),
        in_specs=[s for p in progs for s in p.in_specs],
        out_specs=[s for p in progs for s in p.out_specs],
        out_shape=[s for p in progs for s in p.out_shape],
        scratch_shapes=[pltpu.VMEM((rows,) + shape, dt) for p in progs for shape, dt in p.scratch],
        compiler_params=_params(("parallel", "arbitrary")),
        name="rec",
    )(*[a for p in progs for a in p.args])
    return [outs[sum(n_out[:k]):sum(n_out[:k + 1])] for k in range(len(progs))]


def _prep_weights(w_in, gla_w_a2, mlstm_gate_bias, heads):
    (ah, adk, adv), (bh, bdk, bdv), (ch, cdk, cdv), rank = heads
    sizes = (ah * adk, ah * adk, ah * adv, ah * adv,
             bh * bdk, bh * bdk, bh * bdv, bh * bdv, 2 * bh,
             ch * cdk, ch * cdk, ch * cdv, ch * cdv, rank)
    offs = [0]
    for s in sizes:
        offs.append(offs[-1] + s)
    col = lambda i: w_in[:, :, offs[i]:offs[i + 1]]
    (a_q, a_f, a_i, a_g, b_q, b_k, b_v, b_o, b_if, c_q, c_k, c_v, c_g, c_lr) = [col(i) for i in range(14)]
    pad = jnp.zeros(w_in.shape[:2] + (LANE - 2 * bh - rank,), w_in.dtype)
    groups = ([a_q, a_i, a_g], [a_f], [b_q, b_k, b_v, b_o], [c_q, c_k, c_v, c_g], [b_if, c_lr, pad])
    widths = tuple(sum(g.shape[-1] for g in grp) for grp in groups)
    w_cat = jnp.concatenate([g for grp in groups for g in grp], axis=-1).astype(BF16)
    w_mg = w_in[:, :, offs[14]:].astype(BF16)
    depth = w_in.shape[0]
    w2_pad = jnp.zeros((depth, LANE, gla_w_a2.shape[-1]), F32)
    w2_pad = w2_pad.at[:, 2 * bh:2 * bh + rank, :].set(gla_w_a2).astype(BF16)
    bias_row = jnp.zeros((depth, 1, LANE), F32).at[:, 0, :2 * bh].set(mlstm_gate_bias)
    return w_cat, widths, w_mg, w2_pad, bias_row


def _run_trunk(x, states, wts, heads):
    (ffn1_norm, ffn1_up, ffn1_down, mix_norm, w_cat, widths, w_mg, hgrn_lb_logits, hgrn_norm,
     bias_row, mlstm_norm, w2_pad, gla_b_a, gla_norm, w_branch, w_out,
     ffn2_norm, ffn2_up, ffn2_down, final_norm) = wts
    (ah, adk, adv), (bh, bdk, bdv), (ch, cdk, cdv), _ = heads
    bsz, t_len, d = x.shape
    depth = ffn1_norm.shape[0]
    x = x.reshape(bsz * t_len, d)
    new_states = ([], [], [], [], [])
    for l in range(depth):
        x = _ffn(x, ffn1_norm, ffn1_up, ffn1_down, l)
        p_a, p_f, p_b, p_c, p_s = _inproj(x, mix_norm, w_cat, l, widths)
        if states is None:
            st_a = st_b = st_c = None
        else:
            s_hgrn, c_ml, n_ml, m_ml, s_gla = states
            st_a = s_hgrn[l]
            m_pad = jnp.zeros((bsz, 1, LANE), F32).at[:, 0, :bh].set(m_ml[l])
            st_b = (c_ml[l], n_ml[l], m_pad)
            st_c = s_gla[l].reshape(bsz, ch // 2, 2 * cdk, cdv)
        (o_a, s_a), (o_b, c_new, n_new, m_new), (o_c, s_c) = _run_rec(
            [_glr("hgrn", l, bsz, t_len, (p_a, p_f, hgrn_lb_logits, hgrn_norm), st_a),
             _mlstm(l, bsz, t_len, p_b, p_s, bias_row, mlstm_norm, st_b),
             _glr("gla", l, bsz, t_len, (p_c, p_s, w2_pad, gla_b_a, gla_norm), st_c)],
            bsz, t_len)
        o_a, o_b, o_c = (o.reshape(bsz * t_len, -1) for o in (o_a, o_b, o_c))
        x = _merge(x, mix_norm, o_a, o_b, o_c, w_mg, w_branch, w_out, l)
        x = _ffn(x, ffn2_norm, ffn2_up, ffn2_down, l,
                 final_w=final_norm if l == depth - 1 else None)
        for acc, s in zip(new_states, (s_a, c_new, n_new, m_new[:, 0, :bh],
                                       s_c.reshape(bsz, ch, cdk, cdv))):
            acc.append(s)
    return x.reshape(bsz, t_len, d), tuple(jnp.stack(acc) for acc in new_states)


def kernel(x_prompt, x_sample, state_hgrn, state_mlstm_c, state_mlstm_n, state_mlstm_m, state_gla, ffn1_norm, ffn1_w_up, ffn1_w_down, mix_norm, w_in, hgrn_lb_logits, hgrn_norm, mlstm_gate_bias, mlstm_norm, gla_w_a2, gla_b_a, gla_norm, w_branch, w_out, ffn2_norm, ffn2_w_up, ffn2_w_down, final_norm):
    heads = (state_hgrn.shape[2:], state_mlstm_c.shape[2:], state_gla.shape[2:], gla_w_a2.shape[1])
    assert heads[0][1:] == (LANE, LANE) and heads[1][1:] == (LANE, LANE)
    assert heads[2][1:] == (LANE // 2, LANE) and heads[2][0] % 2 == 0
    w_cat, widths, w_mg, w2_pad, bias_row = _prep_weights(w_in, gla_w_a2, mlstm_gate_bias, heads)
    row = lambda a: a[:, None, :]
    wts = (row(ffn1_norm), ffn1_w_up.astype(BF16), ffn1_w_down.astype(BF16), row(mix_norm),
           w_cat, widths, w_mg, hgrn_lb_logits, row(hgrn_norm), bias_row, row(mlstm_norm),
           w2_pad, row(gla_b_a), row(gla_norm), w_branch.astype(BF16), w_out.astype(BF16),
           row(ffn2_norm), ffn2_w_up.astype(BF16), ffn2_w_down.astype(BF16), final_norm[None, :])
    y_p, st_p = _run_trunk(x_prompt, None, wts, heads)
    sample_states = (state_hgrn, state_mlstm_c, state_mlstm_n, state_mlstm_m, state_gla)
    y_s, st_s = _run_trunk(x_sample, sample_states, wts, heads)
    return (y_p, y_s) + st_p + st_s
```

```python
import functools
from typing import Any, NamedTuple

import jax
import jax.numpy as jnp
from jax import lax
from jax.experimental import pallas as pl
from jax.experimental.pallas import tpu as pltpu

F32 = jnp.float32
BF16 = jnp.bfloat16

EPS = 1e-6
CHUNK = 64
MLSTM_CHUNK = 256
GLA_GATE_TEMP = 16.0
N_BRANCH = 3
LANE = 128
INPROJ_DTYPES = (BF16, F32, BF16, BF16, F32)
V7X_VMEM_BYTES = 64 * 1024 * 1024
VMEM_LIMIT = V7X_VMEM_BYTES - 8 * 1024 * 1024
SAFE_DECAY_SPAN = 60.0
LOG2E = 1.4426950408889634

_NT = (((1,), (1,)), ((), ()))
_TN = (((0,), (0,)), ((), ()))


def _dot(a, b):
    return jnp.dot(a, b, preferred_element_type=F32)


def _dot_nt(a, b):
    return lax.dot_general(a, b, _NT, preferred_element_type=F32)


def _dot_tn(a, b):
    return lax.dot_general(a, b, _TN, preferred_element_type=F32)


def _rms(x, w):
    return x * lax.rsqrt(jnp.mean(x * x, axis=-1, keepdims=True) + EPS) * w


def _sigmoid(x):
    return 1.0 / (1.0 + jnp.exp2(x * (-LOG2E)))


def _log_sigmoid(x):
    return jnp.minimum(x, 0.0) - jnp.log(1.0 + jnp.exp(-jnp.abs(x)))


def _split3(x):
    hi = x.astype(BF16)
    r = x - hi.astype(F32)
    mid = r.astype(BF16)
    lo = (r - mid.astype(F32)).astype(BF16)
    return hi, mid, lo


def _cumsum_rows(tri, g):
    hi, mid, lo = _split3(g)
    return (_dot(tri, hi) + _dot(tri, mid)) + _dot(tri, lo)


def _rows_to_lanes(sel, x):
    hi, mid, lo = _split3(x)
    return (_dot_nt(sel, hi) + _dot_nt(sel, mid)) + _dot_nt(sel, lo)


def _pick_tile(n, cap):
    t = min(n, cap)
    while n % t or (t % 8 and t != n):
        t -= 1
    return t


def _row_parts(tm):
    n = 2 if tm % 512 == 0 and tm >= 1024 else 1
    return [slice(r * (tm // n), (r + 1) * (tm // n)) for r in range(n)]


def _params(sem):
    return pltpu.CompilerParams(dimension_semantics=sem, vmem_limit_bytes=VMEM_LIMIT)


def _ffn_kernel(x_ref, nw_ref, wup_ref, wd_ref, *rest, d_ff, tf, tn, final):
    if final:
        fw_ref, o_ref, act_scr = rest
    else:
        o_ref, act_scr = rest
    d = x_ref.shape[1]
    for rows in _row_parts(x_ref.shape[0]):
        h = _rms(x_ref[rows, :], nw_ref[...]).astype(BF16)
        for c in range(0, d_ff, tf):
            g = _dot(h, wup_ref[:, c:c + tf])
            u = _dot(h, wup_ref[:, d_ff + c:d_ff + c + tf])
            act_scr[rows, c:c + tf] = (g * _sigmoid(g) * u).astype(BF16)
        for c in range(0, d, tn):
            o_ref[rows, c:c + tn] = (x_ref[rows, c:c + tn]
                                     + 0.5 * _dot(act_scr[rows, :], wd_ref[:, c:c + tn]))
        if final:
            o_ref[rows, :] = _rms(o_ref[rows, :], fw_ref[...])


def _ffn(x, norm_w, w_up, w_down, layer, final_w=None):
    n, d = x.shape
    d_ff = w_down.shape[1]
    tm = _pick_tile(n, 1024)
    tf = tn = 256
    assert d_ff % tf == 0 and d % tn == 0
    final = final_w is not None
    resident = pl.Buffered(1)
    in_specs = [
        pl.BlockSpec((tm, d), lambda i: (i, 0)),
        pl.BlockSpec((None, 1, d), lambda i: (layer, 0, 0)),
        pl.BlockSpec((None, d, 2 * d_ff), lambda i: (layer, 0, 0), pipeline_mode=resident),
        pl.BlockSpec((None, d_ff, d), lambda i: (layer, 0, 0), pipeline_mode=resident),
    ]
    args = [x, norm_w, w_up, w_down]
    if final:
        in_specs.append(pl.BlockSpec((1, d), lambda i: (0, 0)))
        args.append(final_w)
    return pl.pallas_call(
        functools.partial(_ffn_kernel, d_ff=d_ff, tf=tf, tn=tn, final=final),
        grid=(n // tm,),
        in_specs=in_specs,
        out_specs=pl.BlockSpec((tm, d), lambda i: (i, 0)),
        out_shape=jax.ShapeDtypeStruct((n, d), F32),
        scratch_shapes=[pltpu.VMEM((tm, d_ff), BF16)],
        compiler_params=_params(("parallel",)),
        name="ffn",
    )(*args)


def _inproj_kernel(x_ref, nw_ref, w_ref, oa_ref, of_ref, ob_ref, oc_ref, os_ref, *, widths):
    for rows in _row_parts(x_ref.shape[0]):
        h = _rms(x_ref[rows, :], nw_ref[...]).astype(BF16)
        c0 = 0
        for ref, width in zip((oa_ref, of_ref, ob_ref, oc_ref, os_ref), widths):
            step = min(width, 512)
            for c in range(0, width, step):
                ref[rows, c:c + step] = _dot(h, w_ref[:, c0 + c:c0 + c + step]).astype(ref.dtype)
            c0 += width


def _inproj(x, norm_w, w_cat, layer, widths):
    n, d = x.shape
    tm = _pick_tile(n, 1024)
    dtypes = INPROJ_DTYPES
    return pl.pallas_call(
        functools.partial(_inproj_kernel, widths=widths),
        grid=(n // tm,),
        in_specs=[
            pl.BlockSpec((tm, d), lambda i: (i, 0)),
            pl.BlockSpec((None, 1, d), lambda i: (layer, 0, 0)),
            pl.BlockSpec((None, d, sum(widths)), lambda i: (layer, 0, 0),
                         pipeline_mode=pl.Buffered(1)),
        ],
        out_specs=[pl.BlockSpec((tm, w), lambda i: (i, 0)) for w in widths],
        out_shape=[jax.ShapeDtypeStruct((n, w), dt) for w, dt in zip(widths, dtypes)],
        compiler_params=_params(("parallel",)),
        name="inproj",
    )(x, norm_w, w_cat)


def _inproj_stage(x_ref, nw_ref, w_ref, *, widths, deferred, shared):
    del deferred
    h = _rms(x_ref[...], nw_ref[...]).astype(BF16)
    c0 = 0
    for ref, width in zip(shared, widths):
        step = min(width, 512)
        for c in range(0, width, step):
            ref[:, c:c + step] = _dot(h, w_ref[:, c0 + c:c0 + c + step]).astype(ref.dtype)
        c0 += width
    yield


def _inproj_program(x, norm_w, w_cat, layer, widths, bsz, t_len):
    d = x.shape[-1]
    tb = _pick_tile(t_len, 256)
    rows = _rows_per_step(bsz, t_len == tb)
    prog = _RecProgram(
        body=functools.partial(_inproj_stage, widths=widths),
        args=[x.reshape(bsz, t_len, d), norm_w, w_cat],
        in_specs=[pl.BlockSpec((rows, tb, d), lambda b, t: (b, t, 0)),
                  pl.BlockSpec((None, 1, d), lambda b, t: (layer, 0, 0)),
                  pl.BlockSpec((None, d, sum(widths)), lambda b, t: (layer, 0, 0),
                               pipeline_mode=pl.Buffered(1))],
        in_rows=[True, False, False], out_specs=[], out_shape=[], scratch=[])
    return prog, [((tb, w), dt) for w, dt in zip(widths, INPROJ_DTYPES)]


def _merge_kernel(x_ref, nw_ref, oa_ref, ob_ref, oc_ref, wm_ref, wb_ref, wo_ref, o_ref):
    d = x_ref.shape[1]
    for rows in _row_parts(x_ref.shape[0]):
        x = x_ref[rows, :]
        h = _rms(x, nw_ref[...]).astype(BF16)
        merged = None
        for nb, br_ref in enumerate((oa_ref, ob_ref, oc_ref)):
            gate = _sigmoid(_dot(h, wm_ref[:, nb * d:(nb + 1) * d]))
            term = gate * _dot(br_ref[rows, :], wb_ref[nb])
            merged = term if merged is None else merged + term
        o_ref[rows, :] = x + _dot(merged.astype(BF16), wo_ref[...])


def _merge(x, norm_w, o_a, o_b, o_c, w_mg, w_branch, w_out, layer):
    n, d = x.shape
    bw = o_a.shape[1]
    tm = _pick_tile(n, 1024)
    br_spec = pl.BlockSpec((tm, bw), lambda i: (i, 0))
    resident = pl.Buffered(1)
    return pl.pallas_call(
        _merge_kernel,
        grid=(n // tm,),
        in_specs=[
            pl.BlockSpec((tm, d), lambda i: (i, 0)),
            pl.BlockSpec((None, 1, d), lambda i: (layer, 0, 0)),
            br_spec, br_spec, br_spec,
            pl.BlockSpec((None, d, N_BRANCH * d), lambda i: (layer, 0, 0), pipeline_mode=resident),
            pl.BlockSpec((None, N_BRANCH, bw, d), lambda i: (layer, 0, 0, 0),
                         pipeline_mode=resident),
            pl.BlockSpec((None, d, d), lambda i: (layer, 0, 0), pipeline_mode=resident),
        ],
        out_specs=pl.BlockSpec((tm, d), lambda i: (i, 0)),
        out_shape=jax.ShapeDtypeStruct((n, d), F32),
        compiler_params=_params(("parallel",)),
        name="merge",
    )(x, norm_w, o_a, o_b, o_c, w_mg, w_branch, w_out)


def _chunk_consts(length, n_chunks=1):
    tb = length * n_chunks
    rb = lax.broadcasted_iota(jnp.int32, (tb, tb), 0)
    cb = lax.broadcasted_iota(jnp.int32, (tb, tb), 1)
    causal = rb >= cb
    for i in range(1, n_chunks):
        causal = causal & ((rb < i * length) | (cb >= i * length))
    return causal, jnp.where(causal, 1.0, 0.0).astype(BF16)


def _head_norm(o, w):
    return o * lax.rsqrt(jnp.mean(o * o, axis=-1, keepdims=True) + EPS) * w


def _glr_intra_exact(q, k, g, vs, masks, tmp_ref, tri):
    length = q.shape[0]
    b = _cumsum_rows(tri, g)
    tmp_ref[0, 0:length, :] = k
    tmp_ref[1, 0:length, :] = b
    for h, v in enumerate(vs):
        tmp_ref[2 + h, 0:length, :] = v
    t_idx = lax.broadcasted_iota(jnp.int32, (length, 1), 0)

    def body(s, accs):
        k_s = tmp_ref[0, pl.ds(s, 1), :]
        b_s = tmp_ref[1, pl.ds(s, 1), :]
        a = q * k_s * jnp.exp2(jnp.minimum(b - b_s, 0.0))
        new = []
        for h, m in enumerate(masks):
            ah = a if m is None else a * m
            col = jnp.where(t_idx >= s, jnp.sum(ah, axis=1, keepdims=True), 0.0)
            new.append(accs[h] + col * tmp_ref[2 + h, pl.ds(s, 1), :])
        return tuple(new)

    init = tuple(jnp.zeros((length, v.shape[1]), F32) for v in vs)
    return lax.fori_loop(0, length, body, init)


def _lane_masks(n_sub):
    if n_sub == 1:
        return [None]
    lane = lax.broadcasted_iota(jnp.int32, (1, LANE), 1)
    w = LANE // n_sub
    return [jnp.where((lane >= h * w) & (lane < (h + 1) * w), 1.0, 0.0) for h in range(n_sub)]


def _when(cond, always):
    return (lambda f: f()) if always else pl.when(cond)


class _RecProgram(NamedTuple):
    body: Any
    args: list
    in_specs: list
    in_rows: list
    out_specs: list
    out_shape: list
    scratch: list


def _rows_per_step(bsz, single_block):
    g = min(bsz, 8 if single_block else 2)
    while bsz % g:
        g -= 1
    return g


def _cols(ref, blk, width):
    return ref.at[:, pl.ds(blk * width, width)]


def _glr_kernel(*refs, kind, layer, length, n_chunks, n_groups, n_sub, has_state, single_block,
                deferred, shared=None):
    bw, kw = n_groups * n_sub * LANE, n_groups * LANE
    if kind == "hgrn" and shared is not None:
        lbl_ref, nw_ref = refs[:2]
        refs = refs[2:]
        q_ref, v_ref, og_ref = (_cols(shared[0], i, bw) for i in range(3))
        f_ref = shared[1]
    elif kind == "hgrn":
        q_ref, v_ref, og_ref, f_ref, lbl_ref, nw_ref = refs[:6]
        refs = refs[6:]
    elif shared is not None:
        w2_ref, ba_ref, nw_ref = refs[:3]
        refs = refs[3:]
        q_ref, k_ref = _cols(shared[3], 0, kw), _cols(shared[3], 1, kw)
        v_ref, og_ref = _cols(shared[3], 1, bw), _cols(shared[3], 2, bw)
        s_ref = shared[4]
    else:
        q_ref, k_ref, v_ref, og_ref, s_ref, w2_ref, ba_ref, nw_ref = refs[:8]
        refs = refs[8:]
    if has_state:
        s0_ref, refs = refs[0], refs[1:]
    (o_ref, sout_ref, st_scr, qt_scr, qin_scr, kt_scr, kst_scr, dec_scr, p_scr, inter_scr,
     tmp_scr) = refs
    t = pl.program_id(1)
    n_heads = n_groups * n_sub
    dv = LANE
    tb = length * n_chunks

    xw = n_chunks * LANE

    def xcols(idx, c):
        return slice(idx * xw + c * LANE, idx * xw + (c + 1) * LANE)

    @_when(t == 0, single_block)
    def _():
        for gi in range(n_groups):
            if has_state:
                st_scr[gi] = s0_ref[gi].T
            else:
                st_scr[gi] = jnp.zeros((dv, LANE), F32)
        if n_chunks > 1:
            kst_scr[...] = jnp.zeros_like(kst_scr)

    yield
    causal, tri_blk = _chunk_consts(length, n_chunks)
    masks = _lane_masks(n_sub)

    if kind == "hgrn":
        logits = lbl_ref[...]
        e = jnp.exp(logits - jnp.max(logits, axis=0, keepdims=True))
        p = e / jnp.sum(e, axis=0, keepdims=True)
        lb = jnp.zeros((1, p.shape[1]), F32)
        for j in range(1, layer + 1):
            lb = lb + p[j:j + 1, :]

    def load_group(rows, gi, with_v=True):
        lanes = slice(gi * LANE, (gi + 1) * LANE)
        if kind == "hgrn":
            aq = q_ref[rows, lanes].astype(F32)
            q = aq * _sigmoid(aq)
            x = f_ref[rows, lanes]
            lbh = lb[:, lanes]
            forget = lbh + (1.0 - lbh) * _sigmoid(x)
            g = jnp.log(forget) * LOG2E
            k = 1.0 - forget
        else:
            q = q_ref[rows, lanes].astype(F32) * (float(LANE // n_sub) ** -0.5)
            k = k_ref[rows, lanes].astype(F32)
            pre = _dot(s_ref[rows, :].astype(BF16), w2_ref[:, lanes]) + ba_ref[:, lanes]
            g = _log_sigmoid(pre) * (LOG2E / GLA_GATE_TEMP)
        vs = [v_ref[rows, (gi * n_sub + h) * dv:(gi * n_sub + h + 1) * dv].astype(F32)
              for h in range(n_sub)] if with_v else None
        return q, k, g, vs

    def finish(rows, head, o):
        lanes = slice(head * dv, (head + 1) * dv)
        og = og_ref[rows, lanes].astype(F32)
        if kind == "hgrn":
            y = _head_norm(o * _sigmoid(og), nw_ref[:, lanes])
        else:
            y = _head_norm(o, nw_ref[:, lanes]) * (og * _sigmoid(og))
        o_ref[rows, lanes] = y.astype(o_ref.dtype)

    spans = []

    gates, b_of = [], []
    for g0 in range(0, n_groups, 2):
        pair = [load_group(slice(0, tb), gi, with_v=False) for gi in range(g0, min(g0 + 2, n_groups))]
        b_pair = _cumsum_rows(tri_blk, jnp.concatenate([g for _, _, g, _ in pair], axis=1))
        gates += pair
        b_of += [b_pair[:, j * LANE:(j + 1) * LANE] for j in range(len(pair))]
        yield

    def phase1(gi):
        glanes = slice(gi * LANE, (gi + 1) * LANE)
        q, k, _, _ = gates[gi]
        b = b_of[gi]
        for c in range(n_chunks):
            sl = slice(c * length, (c + 1) * length)
            bc, qc, kc = b[sl], q[sl], k[sl]
            b_mid = bc[length // 2:length // 2 + 1, :]
            b_last = bc[length - 1:length, :]
            spans.append(jnp.maximum(bc[0:1, :] - b_mid, b_mid - b_last))
            qt = qc * jnp.exp2(bc - b_mid)
            q_in = qc * jnp.exp2(bc)
            kt_scr[sl, glanes] = (kc * jnp.exp2(b_mid - bc)).astype(BF16)
            kst_scr[sl, xcols(gi, c)] = (kc * jnp.exp2(b_last - bc)).astype(BF16)
            dec_scr[c:c + 1, glanes] = jnp.exp2(b_last)
            for h, m in enumerate(masks):
                head = gi * n_sub + h
                qt_scr[sl, head * LANE:(head + 1) * LANE] = (qt if m is None else qt * m).astype(BF16)
                qin_scr[sl, head * LANE:(head + 1) * LANE] = (q_in if m is None else q_in * m).astype(BF16)

    starts = {}

    def phase2(gi):
        glanes = slice(gi * LANE, (gi + 1) * LANE)
        kt = kt_scr[:, glanes]
        kst_x = kst_scr[:, gi * xw:(gi + 1) * xw]
        upd = None
        for h, m in enumerate(masks):
            head = gi * n_sub + h
            hl = slice(head * LANE, (head + 1) * LANE)
            p_scr[head] = jnp.where(causal, _dot_nt(qt_scr[:, hl], kt), 0.0).astype(BF16)
            u = _dot_tn(v_ref[:, hl], kst_x)
            if m is not None:
                u = u * jnp.concatenate([m] * n_chunks, axis=1)
            upd = u if upd is None else upd + u
        st = st_scr[gi]
        starts[gi] = []
        for c in range(n_chunks):
            starts[gi].append(st.astype(BF16))
            st = st * dec_scr[c:c + 1, glanes] + upd[:, c * LANE:(c + 1) * LANE]
        st_scr[gi] = st

    def phase3(gi):
        for h in range(n_sub):
            head = gi * n_sub + h
            hl = slice(head * LANE, (head + 1) * LANE)
            o_intra = _dot(p_scr[head], v_ref[:, hl])
            parts = [_dot_nt(qin_scr[c * length:(c + 1) * length, hl], starts[gi][c])
                     for c in range(n_chunks)]
            o_inter = parts[0] if n_chunks == 1 else jnp.concatenate(parts, axis=0)
            inter_scr[:, hl] = o_inter
            finish(slice(0, tb), head, o_intra + o_inter)

    for gi in range(n_groups):
        phase1(gi)
        phase2(gi)
        yield
    for gi in range(n_groups):
        phase3(gi)
        yield
    span_vec = functools.reduce(jnp.maximum, spans)

    def redo_intra():
        tri = tri_blk[0:length, 0:length]

        def chunk(c, carry):
            rows = pl.ds(pl.multiple_of(c * length, length), length)
            for gi in range(n_groups):
                q, k, g, vs = load_group(rows, gi)
                intra = _glr_intra_exact(q, k, g, vs, masks, tmp_scr, tri)
                for h in range(n_sub):
                    head = gi * n_sub + h
                    finish(rows, head, intra[h] + inter_scr[rows, head * dv:(head + 1) * dv])
            return carry

        lax.fori_loop(0, n_chunks, chunk, 0)

    deferred.append((jnp.max(span_vec) > SAFE_DECAY_SPAN * LOG2E, redo_intra))
    yield

    @_when(t == pl.num_programs(1) - 1, single_block)
    def _():
        for gi in range(n_groups):
            sout_ref[gi] = st_scr[gi].T


def _time_block(kind, t_len):
    tb = _pick_tile(t_len, 256)
    length = min(CHUNK, t_len) if kind == "hgrn" else tb
    assert tb % length == 0
    return length, tb


def _glr(kind, layer, bsz, t_len, ins, s0):
    length, tb = _time_block(kind, t_len)
    n_groups, n_sub = (4, 1) if kind == "hgrn" else (2, 2)
    n_heads = n_groups * n_sub
    bw = n_heads * LANE
    has_state = s0 is not None
    single_block = t_len == tb
    rows = _rows_per_step(bsz, single_block)

    def tok(width, blk):
        return pl.BlockSpec((rows, tb, width), lambda b, t: (b, t, blk))

    def full(shape):
        return pl.BlockSpec(shape, lambda b, t: (0,) * len(shape))

    if kind == "hgrn":
        a, f, lbl, nw = ins
        args = [lbl, nw]
        in_specs = [full(lbl.shape), pl.BlockSpec((None, 1, bw), lambda b, t: (layer, 0, 0))]
        row_flags = [False] * 2
        if a is not None:
            a = a.reshape(bsz, t_len, -1)
            f = f.reshape(bsz, t_len, -1)
            args = [a, a, a, f] + args
            in_specs = [tok(bw, 0), tok(bw, 1), tok(bw, 2), tok(bw, 0)] + in_specs
            row_flags = [True] * 4 + row_flags
    else:
        c, s, w2, ba, nw = ins
        kw = n_groups * LANE
        args = [w2, ba, nw]
        in_specs = [pl.BlockSpec((None, LANE, kw), lambda b, t: (layer, 0, 0)),
                    pl.BlockSpec((None, 1, kw), lambda b, t: (layer, 0, 0)),
                    pl.BlockSpec((None, 1, bw), lambda b, t: (layer, 0, 0))]
        row_flags = [False] * 3
        if c is not None:
            c = c.reshape(bsz, t_len, -1)
            s = s.reshape(bsz, t_len, -1)
            args = [c, c, c, c, s] + args
            in_specs = [tok(kw, 0), tok(kw, 1), tok(bw, 1), tok(bw, 2), tok(LANE, 0)] + in_specs
            row_flags = [True] * 5 + row_flags
    st_spec = pl.BlockSpec((rows, n_groups, LANE, LANE), lambda b, t: (b, 0, 0, 0))
    if has_state:
        args.append(s0)
        in_specs.append(st_spec)
        row_flags.append(True)
    scratch = [((n_groups, LANE, LANE), F32),
               ((tb, bw), BF16),
               ((tb, bw), BF16),
               ((tb, n_groups * LANE), BF16),
               ((tb, n_groups * LANE * (tb // length)), BF16),
               ((max(tb // length, 8), n_groups * LANE), F32),
               ((n_heads, tb, tb), BF16),
               ((tb, bw), F32),
               ((2 + n_sub, length, LANE), F32)]
    body = functools.partial(_glr_kernel, kind=kind, layer=layer, length=length,
                             n_chunks=tb // length, n_groups=n_groups, n_sub=n_sub,
                             has_state=has_state, single_block=single_block)
    return _RecProgram(
        body=body, args=args, in_specs=in_specs, in_rows=row_flags,
        out_specs=[pl.BlockSpec((rows, tb, bw), lambda b, t: (b, t, 0)), st_spec],
        out_shape=[jax.ShapeDtypeStruct((bsz, t_len, bw), BF16),
                   jax.ShapeDtypeStruct((bsz, n_groups, LANE, LANE), F32)],
        scratch=scratch)


def _mlstm_kernel(*refs, length, n_heads, has_state, single_block, deferred, shared=None):
    if shared is not None:
        bias_ref, nw_ref = refs[:2]
        refs = refs[2:]
        q_ref, k_ref, v_ref, og_ref = (_cols(shared[2], i, n_heads * LANE) for i in range(4))
        s_ref = shared[4]
    else:
        q_ref, k_ref, v_ref, og_ref, s_ref, bias_ref, nw_ref = refs[:7]
        refs = refs[7:]
    if has_state:
        c0_ref, n0_ref, m0_ref = refs[:3]
        refs = refs[3:]
    o_ref, cout_ref, nout_ref, mout_ref, cn_scr, m_scr, s_scr = refs
    t = pl.program_id(1)
    dk = dv = LANE
    scale = float(dk) ** -0.5
    lane = lax.broadcasted_iota(jnp.int32, (1, LANE), 1)

    del deferred

    @_when(t == 0, single_block)
    def _():
        for h in range(n_heads):
            if has_state:
                n_cols = jnp.broadcast_to(n0_ref[h:h + 1, :], (dv, dk)).T
                cn_scr[h] = jnp.concatenate([c0_ref[h], n_cols], axis=1)
                m_scr[h:h + 1, :] = jnp.broadcast_to(m0_ref[:, h:h + 1], (1, LANE)) * LOG2E
            else:
                cn_scr[h] = jnp.zeros((dk, 2 * dv), F32)
                m_scr[h:h + 1, :] = jnp.zeros((1, LANE), F32)

    yield
    causal, tri = _chunk_consts(length)
    sel = jnp.where(lax.broadcasted_iota(jnp.int32, (8, LANE), 0)
                    == lax.broadcasted_iota(jnp.int32, (8, LANE), 1), 1.0, 0.0).astype(BF16)
    reps = -(-length // LANE)

    x = s_ref[...] + bias_ref[...]
    y = jnp.where(lane < n_heads, x, _cumsum_rows(tri, _log_sigmoid(x))) * LOG2E
    y_t = _rows_to_lanes(sel, y)
    ones = jnp.ones((length, dv), BF16)
    m_row = jnp.zeros((1, LANE), F32)
    yield

    def gate_cols(h):
        i_c = jnp.broadcast_to(y[:, h:h + 1], (length, LANE))
        b_c = jnp.broadcast_to(y[:, n_heads + h:n_heads + h + 1], (length, LANE))
        return i_c, b_c, b_c[length - 1:length, :]

    def intra_logits(h, b_c):
        z_r = y_t[h:h + 1, :] - y_t[n_heads + h:n_heads + h + 1, :]
        b_cw = jnp.concatenate([b_c] * reps, axis=1)[:, :length]
        return jnp.where(causal, b_cw + z_r, -jnp.inf)

    staged = []
    for h in range(n_heads):
        lanes = slice(h * dk, (h + 1) * dk)
        q = q_ref[:, lanes]
        kf = k_ref[:, lanes].astype(F32) * scale
        v_ext = jnp.concatenate([v_ref[:, lanes], ones], axis=1)
        i_c, b_c, b_last = gate_cols(h)
        s_scr[h] = _dot_nt(q, kf.astype(BF16))
        e_intra = b_last - b_c + i_c
        e_max = jnp.max(e_intra, axis=0, keepdims=True)
        kw = jnp.exp2(e_intra - e_max) * kf
        u_ext = _dot_tn(kw.astype(BF16), v_ext)
        m_old = m_scr[h:h + 1, :]
        cn_old = cn_scr[h]
        qc_ext = _dot(q, cn_old.astype(BF16))
        e_inter = b_last + m_old
        m_new = jnp.maximum(e_inter, e_max)
        w_c = jnp.exp2(e_inter - m_new)
        f_s = jnp.exp2(e_max - m_new)
        cn_scr[h] = (jnp.concatenate([w_c, w_c], axis=1) * cn_old
                     + jnp.concatenate([f_s, f_s], axis=1) * u_ext)
        m_scr[h:h + 1, :] = m_new
        m_row = jnp.where(lane == h, m_new, m_row)
        staged.append((v_ext, b_c + m_old, qc_ext))
        yield

    for h in range(n_heads):
        lanes = slice(h * dk, (h + 1) * dk)
        v_ext, d_inter, qc_ext = staged[h]
        d_intra = intra_logits(h, gate_cols(h)[1])
        a_t = jnp.max(d_intra, axis=1, keepdims=True)
        p = jnp.exp2(d_intra - a_t) * s_scr[h]
        num_ext = _dot(p.astype(BF16), v_ext)
        m_t = jnp.maximum(d_inter, a_t)
        f_intra = jnp.exp2(a_t - m_t)
        w_inter = jnp.exp2(d_inter - m_t)
        tot = (jnp.concatenate([f_intra, f_intra], axis=1) * num_ext
               + jnp.concatenate([w_inter, w_inter], axis=1) * qc_ext)
        hid = tot[:, :dv] / jnp.maximum(jnp.abs(tot[:, dv:]), jnp.exp2(-m_t))
        og = og_ref[:, lanes].astype(F32)
        y_out = _sigmoid(og) * _head_norm(hid, nw_ref[:, lanes])
        o_ref[:, lanes] = y_out.astype(o_ref.dtype)
        yield

    @_when(t == pl.num_programs(1) - 1, single_block)
    def _():
        for h in range(n_heads):
            cn = cn_scr[h]
            cout_ref[h] = cn[:, :dv]
            nout_ref[h:h + 1, :] = cn[:, dv:].T[0:1, :]
        mout_ref[...] = m_row * (1.0 / LOG2E)


def _mlstm(layer, bsz, t_len, p_b, p_s, bias_row, norm_w, states):
    length = tb = _pick_tile(t_len, MLSTM_CHUNK)
    bw = norm_w.shape[-1]
    n_heads = bw // LANE
    has_state = states is not None
    single_block = t_len == tb
    rows = _rows_per_step(bsz, single_block)

    def tok(width, blk):
        return pl.BlockSpec((rows, tb, width), lambda b, t: (b, t, blk))

    c_spec = pl.BlockSpec((rows, n_heads, LANE, LANE), lambda b, t: (b, 0, 0, 0))
    n_spec = pl.BlockSpec((rows, n_heads, LANE), lambda b, t: (b, 0, 0))
    m_spec = pl.BlockSpec((rows, 1, LANE), lambda b, t: (b, 0, 0))
    args = [bias_row, norm_w]
    in_specs = [pl.BlockSpec((None, 1, LANE), lambda b, t: (layer, 0, 0)),
                pl.BlockSpec((None, 1, bw), lambda b, t: (layer, 0, 0))]
    row_flags = [False] * 2
    if p_b is not None:
        p_b = p_b.reshape(bsz, t_len, -1)
        p_s = p_s.reshape(bsz, t_len, -1)
        args = [p_b, p_b, p_b, p_b, p_s] + args
        in_specs = [tok(bw, 0), tok(bw, 1), tok(bw, 2), tok(bw, 3), tok(LANE, 0)] + in_specs
        row_flags = [True] * 5 + row_flags
    if has_state:
        args += list(states)
        in_specs += [c_spec, n_spec, m_spec]
        row_flags += [True] * 3
    scratch = [((n_heads, LANE, 2 * LANE), F32), ((n_heads, LANE), F32), ((n_heads, tb, tb), F32)]
    body = functools.partial(_mlstm_kernel, length=length, n_heads=n_heads, has_state=has_state,
                             single_block=single_block)
    return _RecProgram(
        body=body, args=args, in_specs=in_specs, in_rows=row_flags,
        out_specs=[pl.BlockSpec((rows, tb, bw), lambda b, t: (b, t, 0)), c_spec, n_spec, m_spec],
        out_shape=[jax.ShapeDtypeStruct((bsz, t_len, bw), BF16),
                   jax.ShapeDtypeStruct((bsz, n_heads, LANE, LANE), F32),
                   jax.ShapeDtypeStruct((bsz, n_heads, LANE), F32),
                   jax.ShapeDtypeStruct((bsz, 1, LANE), F32)],
        scratch=scratch)


def _run_rec(progs, bsz, t_len, shared=()):
    tb = _pick_tile(t_len, 256)
    rows = _rows_per_step(bsz, t_len == tb)
    n_in = [len(p.args) for p in progs]
    n_out = [len(p.out_shape) for p in progs]
    n_scr = [len(p.scratch) for p in progs]

    def kern(*refs):
        ins, outs, scrs = (refs[:sum(n_in)], refs[sum(n_in):sum(n_in) + sum(n_out)],
                           refs[sum(n_in) + sum(n_out):])
        shared_refs = scrs[sum(n_scr):]
        deferred, live = [], []
        for k, p in enumerate(progs):
            mine = (list(ins[sum(n_in[:k]):sum(n_in[:k + 1])])
                    + list(outs[sum(n_out[:k]):sum(n_out[:k + 1])])
                    + list(scrs[sum(n_scr[:k]):sum(n_scr[:k + 1])]))
            flags = list(p.in_rows) + [True] * (n_out[k] + n_scr[k])
            for g in range(rows):
                extra = {"shared": [r.at[g] for r in shared_refs]} if shared else {}
                live.append(p.body(*[r.at[g] if f else r for r, f in zip(mine, flags)],
                                   deferred=deferred, **extra))
        while live:
            still = []
            for gen in live:
                try:
                    next(gen)
                    still.append(gen)
                except StopIteration:
                    pass
            live = still
        for cond, fn in deferred:
            pl.when(cond)(fn)

    outs = pl.pallas_call(
        kern,
        grid=(bsz // rows, t_len // tb),
        in_specs=[s for p in progs for s in p.in_specs],
        out_specs=[s for p in progs for s in p.out_specs],
        out_shape=[s for p in progs for s in p.out_shape],
        scratch_shapes=([pltpu.VMEM((rows,) + shape, dt) for p in progs for shape, dt in p.scratch]
                        + [pltpu.VMEM((rows,) + tuple(shape), dt) for shape, dt in shared]),
        compiler_params=_params(("parallel", "arbitrary")),
        name="rec",
    )(*[a for p in progs for a in p.args])
    return [outs[sum(n_out[:k]):sum(n_out[:k + 1])] for k in range(len(progs))]


def _prep_weights(w_in, gla_w_a2, mlstm_gate_bias, heads):
    (ah, adk, adv), (bh, bdk, bdv), (ch, cdk, cdv), rank = heads
    sizes = (ah * adk, ah * adk, ah * adv, ah * adv,
             bh * bdk, bh * bdk, bh * bdv, bh * bdv, 2 * bh,
             ch * cdk, ch * cdk, ch * cdv, ch * cdv, rank)
    offs = [0]
    for s in sizes:
        offs.append(offs[-1] + s)
    col = lambda i: w_in[:, :, offs[i]:offs[i + 1]]
    (a_q, a_f, a_i, a_g, b_q, b_k, b_v, b_o, b_if, c_q, c_k, c_v, c_g, c_lr) = [col(i) for i in range(14)]
    pad = jnp.zeros(w_in.shape[:2] + (LANE - 2 * bh - rank,), w_in.dtype)
    groups = ([a_q, a_i, a_g], [a_f], [b_q, b_k, b_v, b_o], [c_q, c_k, c_v, c_g], [b_if, c_lr, pad])
    widths = tuple(sum(g.shape[-1] for g in grp) for grp in groups)
    w_cat = jnp.concatenate([g for grp in groups for g in grp], axis=-1).astype(BF16)
    w_mg = w_in[:, :, offs[14]:].astype(BF16)
    depth = w_in.shape[0]
    w2_pad = jnp.zeros((depth, LANE, gla_w_a2.shape[-1]), F32)
    w2_pad = w2_pad.at[:, 2 * bh:2 * bh + rank, :].set(gla_w_a2).astype(BF16)
    bias_row = jnp.zeros((depth, 1, LANE), F32).at[:, 0, :2 * bh].set(mlstm_gate_bias)
    return w_cat, widths, w_mg, w2_pad, bias_row


def _run_trunk(x, states, wts, heads):
    (ffn1_norm, ffn1_up, ffn1_down, mix_norm, w_cat, widths, w_mg, hgrn_lb_logits, hgrn_norm,
     bias_row, mlstm_norm, w2_pad, gla_b_a, gla_norm, w_branch, w_out,
     ffn2_norm, ffn2_up, ffn2_down, final_norm) = wts
    (ah, adk, adv), (bh, bdk, bdv), (ch, cdk, cdv), _ = heads
    bsz, t_len, d = x.shape
    depth = ffn1_norm.shape[0]
    x = x.reshape(bsz * t_len, d)
    new_states = ([], [], [], [], [])
    for l in range(depth):
        x = _ffn(x, ffn1_norm, ffn1_up, ffn1_down, l)
        if states is None:
            p_a = p_f = p_b = p_c = p_s = st_a = st_b = st_c = None
            extra, shared = _inproj_program(x, mix_norm, w_cat, l, widths, bsz, t_len)
        else:
            p_a, p_f, p_b, p_c, p_s = _inproj(x, mix_norm, w_cat, l, widths)
            extra, shared = None, ()
            s_hgrn, c_ml, n_ml, m_ml, s_gla = states
            st_a = s_hgrn[l]
            m_pad = jnp.zeros((bsz, 1, LANE), F32).at[:, 0, :bh].set(m_ml[l])
            st_b = (c_ml[l], n_ml[l], m_pad)
            st_c = s_gla[l].reshape(bsz, ch // 2, 2 * cdk, cdv)
        progs = [_glr("hgrn", l, bsz, t_len, (p_a, p_f, hgrn_lb_logits, hgrn_norm), st_a),
                 _mlstm(l, bsz, t_len, p_b, p_s, bias_row, mlstm_norm, st_b),
                 _glr("gla", l, bsz, t_len, (p_c, p_s, w2_pad, gla_b_a, gla_norm), st_c)]
        (o_a, s_a), (o_b, c_new, n_new, m_new), (o_c, s_c) = _run_rec(
            progs + ([extra] if extra else []), bsz, t_len, shared)[:3]
        o_a, o_b, o_c = (o.reshape(bsz * t_len, -1) for o in (o_a, o_b, o_c))
        x = _merge(x, mix_norm, o_a, o_b, o_c, w_mg, w_branch, w_out, l)
        x = _ffn(x, ffn2_norm, ffn2_up, ffn2_down, l,
                 final_w=final_norm if l == depth - 1 else None)
        for acc, s in zip(new_states, (s_a, c_new, n_new, m_new[:, 0, :bh],
                                       s_c.reshape(bsz, ch, cdk, cdv))):
            acc.append(s)
    return x.reshape(bsz, t_len, d), tuple(jnp.stack(acc) for acc in new_states)


def kernel(x_prompt, x_sample, state_hgrn, state_mlstm_c, state_mlstm_n, state_mlstm_m, state_gla, ffn1_norm, ffn1_w_up, ffn1_w_down, mix_norm, w_in, hgrn_lb_logits, hgrn_norm, mlstm_gate_bias, mlstm_norm, gla_w_a2, gla_b_a, gla_norm, w_branch, w_out, ffn2_norm, ffn2_w_up, ffn2_w_down, final_norm):
    heads = (state_hgrn.shape[2:], state_mlstm_c.shape[2:], state_gla.shape[2:], gla_w_a2.shape[1])
    assert heads[0][1:] == (LANE, LANE) and heads[1][1:] == (LANE, LANE)
    assert heads[2][1:] == (LANE // 2, LANE) and heads[2][0] % 2 == 0
    w_cat, widths, w_mg, w2_pad, bias_row = _prep_weights(w_in, gla_w_a2, mlstm_gate_bias, heads)
    row = lambda a: a[:, None, :]
    wts = (row(ffn1_norm), ffn1_w_up.astype(BF16), ffn1_w_down.astype(BF16), row(mix_norm),
           w_cat, widths, w_mg, hgrn_lb_logits, row(hgrn_norm), bias_row, row(mlstm_norm),
           w2_pad, row(gla_b_a), row(gla_norm), w_branch.astype(BF16), w_out.astype(BF16),
           row(ffn2_norm), ffn2_w_up.astype(BF16), ffn2_w_down.astype(BF16), final_norm[None, :])
    y_p, st_p = _run_trunk(x_prompt, None, wts, heads)
    sample_states = (state_hgrn, state_mlstm_c, state_mlstm_n, state_mlstm_m, state_gla)
    y_s, st_s = _run_trunk(x_sample, sample_states, wts, heads)
    return (y_p, y_s) + st_p + st_s
```

```python
import functools
from typing import Any, NamedTuple

import jax
import jax.numpy as jnp
from jax import lax
from jax.experimental import pallas as pl
from jax.experimental.pallas import tpu as pltpu

F32 = jnp.float32
BF16 = jnp.bfloat16

EPS = 1e-6
CHUNK = 64
MLSTM_CHUNK = 256
GLA_GATE_TEMP = 16.0
N_BRANCH = 3
LANE = 128
MXU_COLS = 256
TOKEN_TILE = 1024
COL_STEP = 512
TIME_BLOCK = 256
ROWS_SHORT, ROWS_LONG = 8, 2
INPROJ_DTYPES = (BF16, F32, BF16, BF16, F32)
V7X_VMEM_BYTES = 64 * 1024 * 1024
VMEM_LIMIT = V7X_VMEM_BYTES - 8 * 1024 * 1024
SAFE_DECAY_SPAN = 60.0
LOG2E = 1.4426950408889634

_NT = (((1,), (1,)), ((), ()))
_TN = (((0,), (0,)), ((), ()))


def _dot(a, b):
    return jnp.dot(a, b, preferred_element_type=F32)


def _dot_nt(a, b):
    return lax.dot_general(a, b, _NT, preferred_element_type=F32)


def _dot_tn(a, b):
    return lax.dot_general(a, b, _TN, preferred_element_type=F32)


def _rms(x, w):
    return x * lax.rsqrt(jnp.mean(x * x, axis=-1, keepdims=True) + EPS) * w


def _sigmoid(x):
    return 1.0 / (1.0 + jnp.exp2(x * (-LOG2E)))


def _log_sigmoid(x):
    return jnp.minimum(x, 0.0) - jnp.log(1.0 + jnp.exp(-jnp.abs(x)))


def _split3(x):
    hi = x.astype(BF16)
    r = x - hi.astype(F32)
    mid = r.astype(BF16)
    lo = (r - mid.astype(F32)).astype(BF16)
    return hi, mid, lo


def _cumsum_rows(tri, g):
    hi, mid, lo = _split3(g)
    return (_dot(tri, hi) + _dot(tri, mid)) + _dot(tri, lo)


def _rows_to_lanes(sel, x):
    hi, mid, lo = _split3(x)
    return (_dot_nt(sel, hi) + _dot_nt(sel, mid)) + _dot_nt(sel, lo)


def _pick_tile(n, cap):
    t = min(n, cap)
    while n % t or (t % 8 and t != n):
        t -= 1
    return t


def _row_parts(tm):
    n = 2 if tm == TOKEN_TILE else 1
    return [slice(r * (tm // n), (r + 1) * (tm // n)) for r in range(n)]


def _params(sem):
    return pltpu.CompilerParams(dimension_semantics=sem, vmem_limit_bytes=VMEM_LIMIT)


def _ffn_kernel(x_ref, nw_ref, wup_ref, wd_ref, *rest, d_ff, tf, tn, final):
    if final:
        fw_ref, o_ref, act_scr = rest
    else:
        o_ref, act_scr = rest
    d = x_ref.shape[1]
    for rows in _row_parts(x_ref.shape[0]):
        h = _rms(x_ref[rows, :], nw_ref[...]).astype(BF16)
        for c in range(0, d_ff, tf):
            g = _dot(h, wup_ref[:, c:c + tf])
            u = _dot(h, wup_ref[:, d_ff + c:d_ff + c + tf])
            act_scr[rows, c:c + tf] = (g * _sigmoid(g) * u).astype(BF16)
        for c in range(0, d, tn):
            o_ref[rows, c:c + tn] = (x_ref[rows, c:c + tn]
                                     + 0.5 * _dot(act_scr[rows, :], wd_ref[:, c:c + tn]))
        if final:
            o_ref[rows, :] = _rms(o_ref[rows, :], fw_ref[...])


def _ffn(x, norm_w, w_up, w_down, layer, final_w=None):
    n, d = x.shape
    d_ff = w_down.shape[1]
    tm = _pick_tile(n, TOKEN_TILE)
    tf = tn = MXU_COLS
    assert d_ff % tf == 0 and d % tn == 0
    final = final_w is not None
    resident = pl.Buffered(1)
    in_specs = [
        pl.BlockSpec((tm, d), lambda i: (i, 0)),
        pl.BlockSpec((None, 1, d), lambda i: (layer, 0, 0)),
        pl.BlockSpec((None, d, 2 * d_ff), lambda i: (layer, 0, 0), pipeline_mode=resident),
        pl.BlockSpec((None, d_ff, d), lambda i: (layer, 0, 0), pipeline_mode=resident),
    ]
    args = [x, norm_w, w_up, w_down]
    if final:
        in_specs.append(pl.BlockSpec((1, d), lambda i: (0, 0)))
        args.append(final_w)
    return pl.pallas_call(
        functools.partial(_ffn_kernel, d_ff=d_ff, tf=tf, tn=tn, final=final),
        grid=(n // tm,),
        in_specs=in_specs,
        out_specs=pl.BlockSpec((tm, d), lambda i: (i, 0)),
        out_shape=jax.ShapeDtypeStruct((n, d), F32),
        scratch_shapes=[pltpu.VMEM((tm, d_ff), BF16)],
        compiler_params=_params(("parallel",)),
        name="ffn",
    )(*args)


def _inproj_kernel(x_ref, nw_ref, w_ref, oa_ref, of_ref, ob_ref, oc_ref, os_ref, *, widths):
    for rows in _row_parts(x_ref.shape[0]):
        h = _rms(x_ref[rows, :], nw_ref[...]).astype(BF16)
        c0 = 0
        for ref, width in zip((oa_ref, of_ref, ob_ref, oc_ref, os_ref), widths):
            step = min(width, COL_STEP)
            for c in range(0, width, step):
                ref[rows, c:c + step] = _dot(h, w_ref[:, c0 + c:c0 + c + step]).astype(ref.dtype)
            c0 += width


def _inproj(x, norm_w, w_cat, layer, widths):
    n, d = x.shape
    tm = _pick_tile(n, TOKEN_TILE)
    dtypes = INPROJ_DTYPES
    return pl.pallas_call(
        functools.partial(_inproj_kernel, widths=widths),
        grid=(n // tm,),
        in_specs=[
            pl.BlockSpec((tm, d), lambda i: (i, 0)),
            pl.BlockSpec((None, 1, d), lambda i: (layer, 0, 0)),
            pl.BlockSpec((None, d, sum(widths)), lambda i: (layer, 0, 0),
                         pipeline_mode=pl.Buffered(1)),
        ],
        out_specs=[pl.BlockSpec((tm, w), lambda i: (i, 0)) for w in widths],
        out_shape=[jax.ShapeDtypeStruct((n, w), dt) for w, dt in zip(widths, dtypes)],
        compiler_params=_params(("parallel",)),
        name="inproj",
    )(x, norm_w, w_cat)


def _inproj_stage(x_ref, nw_ref, w_ref, *, widths, deferred, shared):
    del deferred
    h = _rms(x_ref[...], nw_ref[...]).astype(BF16)
    starts = [sum(widths[:i]) for i in range(len(widths))]
    for stage in ((0, 1, 3, 4), (2,)):
        for i in stage:
            step = min(widths[i], COL_STEP)
            for c in range(0, widths[i], step):
                cols = slice(starts[i] + c, starts[i] + c + step)
                shared[i][:, c:c + step] = _dot(h, w_ref[:, cols]).astype(shared[i].dtype)
        yield


def _inproj_program(x, norm_w, w_cat, layer, widths, bsz, t_len):
    d = x.shape[-1]
    tb = _pick_tile(t_len, TIME_BLOCK)
    rows = _rows_per_step(bsz, t_len == tb)
    prog = _RecProgram(
        body=functools.partial(_inproj_stage, widths=widths),
        args=[x.reshape(bsz, t_len, d), norm_w, w_cat],
        in_specs=[pl.BlockSpec((rows, tb, d), lambda b, t: (b, t, 0)),
                  pl.BlockSpec((None, 1, d), lambda b, t: (layer, 0, 0)),
                  pl.BlockSpec((None, d, sum(widths)), lambda b, t: (layer, 0, 0),
                               pipeline_mode=pl.Buffered(1))],
        in_rows=[True, False, False], out_specs=[], out_shape=[], scratch=[])
    return prog, [((tb, w), dt) for w, dt in zip(widths, INPROJ_DTYPES)]


def _merge_kernel(x_ref, nw_ref, oa_ref, ob_ref, oc_ref, wm_ref, wb_ref, wo_ref, o_ref):
    d = x_ref.shape[1]
    for rows in _row_parts(x_ref.shape[0]):
        x = x_ref[rows, :]
        h = _rms(x, nw_ref[...]).astype(BF16)
        merged = None
        for nb, br_ref in enumerate((oa_ref, ob_ref, oc_ref)):
            gate = _sigmoid(_dot(h, wm_ref[:, nb * d:(nb + 1) * d]))
            term = gate * _dot(br_ref[rows, :], wb_ref[nb])
            merged = term if merged is None else merged + term
        o_ref[rows, :] = x + _dot(merged.astype(BF16), wo_ref[...])


def _merge(x, norm_w, o_a, o_b, o_c, w_mg, w_branch, w_out, layer):
    n, d = x.shape
    bw = o_a.shape[1]
    tm = _pick_tile(n, TOKEN_TILE)
    br_spec = pl.BlockSpec((tm, bw), lambda i: (i, 0))
    resident = pl.Buffered(1)
    return pl.pallas_call(
        _merge_kernel,
        grid=(n // tm,),
        in_specs=[
            pl.BlockSpec((tm, d), lambda i: (i, 0)),
            pl.BlockSpec((None, 1, d), lambda i: (layer, 0, 0)),
            br_spec, br_spec, br_spec,
            pl.BlockSpec((None, d, N_BRANCH * d), lambda i: (layer, 0, 0), pipeline_mode=resident),
            pl.BlockSpec((None, N_BRANCH, bw, d), lambda i: (layer, 0, 0, 0),
                         pipeline_mode=resident),
            pl.BlockSpec((None, d, d), lambda i: (layer, 0, 0), pipeline_mode=resident),
        ],
        out_specs=pl.BlockSpec((tm, d), lambda i: (i, 0)),
        out_shape=jax.ShapeDtypeStruct((n, d), F32),
        compiler_params=_params(("parallel",)),
        name="merge",
    )(x, norm_w, o_a, o_b, o_c, w_mg, w_branch, w_out)


def _chunk_consts(length, n_chunks=1):
    tb = length * n_chunks
    rb = lax.broadcasted_iota(jnp.int32, (tb, tb), 0)
    cb = lax.broadcasted_iota(jnp.int32, (tb, tb), 1)
    causal = rb >= cb
    for i in range(1, n_chunks):
        causal = causal & ((rb < i * length) | (cb >= i * length))
    return causal, jnp.where(causal, 1.0, 0.0).astype(BF16)


def _head_norm(o, w):
    return o * lax.rsqrt(jnp.mean(o * o, axis=-1, keepdims=True) + EPS) * w


def _glr_intra_exact(q, k, g, vs, masks, tmp_ref, tri):
    length = q.shape[0]
    b = _cumsum_rows(tri, g)
    tmp_ref[0, 0:length, :] = k
    tmp_ref[1, 0:length, :] = b
    for h, v in enumerate(vs):
        tmp_ref[2 + h, 0:length, :] = v
    t_idx = lax.broadcasted_iota(jnp.int32, (length, 1), 0)

    def body(s, accs):
        k_s = tmp_ref[0, pl.ds(s, 1), :]
        b_s = tmp_ref[1, pl.ds(s, 1), :]
        a = q * k_s * jnp.exp2(jnp.minimum(b - b_s, 0.0))
        new = []
        for h, m in enumerate(masks):
            ah = a if m is None else a * m
            col = jnp.where(t_idx >= s, jnp.sum(ah, axis=1, keepdims=True), 0.0)
            new.append(accs[h] + col * tmp_ref[2 + h, pl.ds(s, 1), :])
        return tuple(new)

    init = tuple(jnp.zeros((length, v.shape[1]), F32) for v in vs)
    return lax.fori_loop(0, length, body, init)


def _lane_masks(n_sub):
    if n_sub == 1:
        return [None]
    lane = lax.broadcasted_iota(jnp.int32, (1, LANE), 1)
    w = LANE // n_sub
    return [jnp.where((lane >= h * w) & (lane < (h + 1) * w), 1.0, 0.0) for h in range(n_sub)]


def _when(cond, always):
    return (lambda f: f()) if always else pl.when(cond)


class _RecProgram(NamedTuple):
    body: Any
    args: list
    in_specs: list
    in_rows: list
    out_specs: list
    out_shape: list
    scratch: list


def _rows_per_step(bsz, single_block):
    g = min(bsz, ROWS_SHORT if single_block else ROWS_LONG)
    while bsz % g:
        g -= 1
    return g


def _cols(ref, blk, width):
    return ref.at[:, pl.ds(blk * width, width)]


def _glr_kernel(*refs, kind, layer, length, n_chunks, n_groups, n_sub, has_state, single_block,
                deferred, shared=None):
    bw, kw = n_groups * n_sub * LANE, n_groups * LANE
    if kind == "hgrn" and shared is not None:
        lbl_ref, nw_ref = refs[:2]
        refs = refs[2:]
        q_ref, v_ref, og_ref = (_cols(shared[0], i, bw) for i in range(3))
        f_ref = shared[1]
    elif kind == "hgrn":
        q_ref, v_ref, og_ref, f_ref, lbl_ref, nw_ref = refs[:6]
        refs = refs[6:]
    elif shared is not None:
        w2_ref, ba_ref, nw_ref = refs[:3]
        refs = refs[3:]
        q_ref, k_ref = _cols(shared[3], 0, kw), _cols(shared[3], 1, kw)
        v_ref, og_ref = _cols(shared[3], 1, bw), _cols(shared[3], 2, bw)
        s_ref = shared[4]
    else:
        q_ref, k_ref, v_ref, og_ref, s_ref, w2_ref, ba_ref, nw_ref = refs[:8]
        refs = refs[8:]
    if has_state:
        s0_ref, refs = refs[0], refs[1:]
    (o_ref, sout_ref, st_scr, qt_scr, qin_scr, kt_scr, kst_scr, dec_scr, p_scr, inter_scr,
     tmp_scr) = refs
    t = pl.program_id(1)
    n_heads = n_groups * n_sub
    dv = LANE
    tb = length * n_chunks

    xw = n_chunks * LANE

    def xcols(idx, c):
        return slice(idx * xw + c * LANE, idx * xw + (c + 1) * LANE)

    @_when(t == 0, single_block)
    def _():
        for gi in range(n_groups):
            if has_state:
                st_scr[gi] = s0_ref[gi].T
            else:
                st_scr[gi] = jnp.zeros((dv, LANE), F32)
        if n_chunks > 1:
            kst_scr[...] = jnp.zeros_like(kst_scr)

    yield
    causal, tri_blk = _chunk_consts(length, n_chunks)
    masks = _lane_masks(n_sub)

    if kind == "hgrn":
        logits = lbl_ref[...]
        e = jnp.exp(logits - jnp.max(logits, axis=0, keepdims=True))
        p = e / jnp.sum(e, axis=0, keepdims=True)
        lb = jnp.zeros((1, p.shape[1]), F32)
        for j in range(1, layer + 1):
            lb = lb + p[j:j + 1, :]

    def load_group(rows, gi, with_v=True):
        lanes = slice(gi * LANE, (gi + 1) * LANE)
        if kind == "hgrn":
            aq = q_ref[rows, lanes].astype(F32)
            q = aq * _sigmoid(aq)
            x = f_ref[rows, lanes]
            lbh = lb[:, lanes]
            forget = lbh + (1.0 - lbh) * _sigmoid(x)
            g = jnp.log(forget) * LOG2E
            k = 1.0 - forget
        else:
            q = q_ref[rows, lanes].astype(F32) * (float(LANE // n_sub) ** -0.5)
            k = k_ref[rows, lanes].astype(F32)
            pre = _dot(s_ref[rows, :].astype(BF16), w2_ref[:, lanes]) + ba_ref[:, lanes]
            g = _log_sigmoid(pre) * (LOG2E / GLA_GATE_TEMP)
        vs = [v_ref[rows, (gi * n_sub + h) * dv:(gi * n_sub + h + 1) * dv].astype(F32)
              for h in range(n_sub)] if with_v else None
        return q, k, g, vs

    def finish(rows, head, o):
        lanes = slice(head * dv, (head + 1) * dv)
        og = og_ref[rows, lanes].astype(F32)
        if kind == "hgrn":
            y = _head_norm(o * _sigmoid(og), nw_ref[:, lanes])
        else:
            y = _head_norm(o, nw_ref[:, lanes]) * (og * _sigmoid(og))
        o_ref[rows, lanes] = y.astype(o_ref.dtype)

    spans = []

    gates, b_of = [], []
    for g0 in range(0, n_groups, 2):
        pair = [load_group(slice(0, tb), gi, with_v=False) for gi in range(g0, min(g0 + 2, n_groups))]
        b_pair = _cumsum_rows(tri_blk, jnp.concatenate([g for _, _, g, _ in pair], axis=1))
        gates += pair
        b_of += [b_pair[:, j * LANE:(j + 1) * LANE] for j in range(len(pair))]
        yield

    def phase1(gi):
        glanes = slice(gi * LANE, (gi + 1) * LANE)
        q, k, _, _ = gates[gi]
        b = b_of[gi]
        for c in range(n_chunks):
            sl = slice(c * length, (c + 1) * length)
            bc, qc, kc = b[sl], q[sl], k[sl]
            b_mid = bc[length // 2:length // 2 + 1, :]
            b_last = bc[length - 1:length, :]
            spans.append(jnp.maximum(bc[0:1, :] - b_mid, b_mid - b_last))
            qt = qc * jnp.exp2(bc - b_mid)
            q_in = qc * jnp.exp2(bc)
            kt_scr[sl, glanes] = (kc * jnp.exp2(b_mid - bc)).astype(BF16)
            kst_scr[sl, xcols(gi, c)] = (kc * jnp.exp2(b_last - bc)).astype(BF16)
            dec_scr[c:c + 1, glanes] = jnp.exp2(b_last)
            for h, m in enumerate(masks):
                head = gi * n_sub + h
                qt_scr[sl, head * LANE:(head + 1) * LANE] = (qt if m is None else qt * m).astype(BF16)
                qin_scr[sl, head * LANE:(head + 1) * LANE] = (q_in if m is None else q_in * m).astype(BF16)

    starts = {}

    def phase2(gi):
        glanes = slice(gi * LANE, (gi + 1) * LANE)
        kt = kt_scr[:, glanes]
        kst_x = kst_scr[:, gi * xw:(gi + 1) * xw]
        upd = None
        for h, m in enumerate(masks):
            head = gi * n_sub + h
            hl = slice(head * LANE, (head + 1) * LANE)
            p_scr[head] = jnp.where(causal, _dot_nt(qt_scr[:, hl], kt), 0.0).astype(BF16)
            u = _dot_tn(v_ref[:, hl], kst_x)
            if m is not None:
                u = u * jnp.concatenate([m] * n_chunks, axis=1)
            upd = u if upd is None else upd + u
        st = st_scr[gi]
        starts[gi] = []
        for c in range(n_chunks):
            starts[gi].append(st.astype(BF16))
            st = st * dec_scr[c:c + 1, glanes] + upd[:, c * LANE:(c + 1) * LANE]
        st_scr[gi] = st

    def phase3(gi):
        for h in range(n_sub):
            head = gi * n_sub + h
            hl = slice(head * LANE, (head + 1) * LANE)
            o_intra = _dot(p_scr[head], v_ref[:, hl])
            parts = [_dot_nt(qin_scr[c * length:(c + 1) * length, hl], starts[gi][c])
                     for c in range(n_chunks)]
            o_inter = parts[0] if n_chunks == 1 else jnp.concatenate(parts, axis=0)
            inter_scr[:, hl] = o_inter
            finish(slice(0, tb), head, o_intra + o_inter)

    for gi in range(n_groups):
        phase1(gi)
        phase2(gi)
        yield
    for gi in range(n_groups):
        phase3(gi)
        yield
    span_vec = functools.reduce(jnp.maximum, spans)

    def redo_intra():
        tri = tri_blk[0:length, 0:length]

        def chunk(c, carry):
            rows = pl.ds(pl.multiple_of(c * length, length), length)
            for gi in range(n_groups):
                q, k, g, vs = load_group(rows, gi)
                intra = _glr_intra_exact(q, k, g, vs, masks, tmp_scr, tri)
                for h in range(n_sub):
                    head = gi * n_sub + h
                    finish(rows, head, intra[h] + inter_scr[rows, head * dv:(head + 1) * dv])
            return carry

        lax.fori_loop(0, n_chunks, chunk, 0)

    deferred.append((jnp.max(span_vec) > SAFE_DECAY_SPAN * LOG2E, redo_intra))
    yield

    @_when(t == pl.num_programs(1) - 1, single_block)
    def _():
        for gi in range(n_groups):
            sout_ref[gi] = st_scr[gi].T


def _time_block(kind, t_len):
    tb = _pick_tile(t_len, TIME_BLOCK)
    length = min(CHUNK, t_len) if kind == "hgrn" else tb
    assert tb % length == 0
    return length, tb


def _glr(kind, layer, bsz, t_len, ins, s0):
    length, tb = _time_block(kind, t_len)
    n_groups, n_sub = (4, 1) if kind == "hgrn" else (2, 2)
    n_heads = n_groups * n_sub
    bw = n_heads * LANE
    has_state = s0 is not None
    single_block = t_len == tb
    rows = _rows_per_step(bsz, single_block)

    def tok(width, blk):
        return pl.BlockSpec((rows, tb, width), lambda b, t: (b, t, blk))

    def full(shape):
        return pl.BlockSpec(shape, lambda b, t: (0,) * len(shape))

    if kind == "hgrn":
        a, f, lbl, nw = ins
        args = [lbl, nw]
        in_specs = [full(lbl.shape), pl.BlockSpec((None, 1, bw), lambda b, t: (layer, 0, 0))]
        row_flags = [False] * 2
        if a is not None:
            a = a.reshape(bsz, t_len, -1)
            f = f.reshape(bsz, t_len, -1)
            args = [a, a, a, f] + args
            in_specs = [tok(bw, 0), tok(bw, 1), tok(bw, 2), tok(bw, 0)] + in_specs
            row_flags = [True] * 4 + row_flags
    else:
        c, s, w2, ba, nw = ins
        kw = n_groups * LANE
        args = [w2, ba, nw]
        in_specs = [pl.BlockSpec((None, LANE, kw), lambda b, t: (layer, 0, 0)),
                    pl.BlockSpec((None, 1, kw), lambda b, t: (layer, 0, 0)),
                    pl.BlockSpec((None, 1, bw), lambda b, t: (layer, 0, 0))]
        row_flags = [False] * 3
        if c is not None:
            c = c.reshape(bsz, t_len, -1)
            s = s.reshape(bsz, t_len, -1)
            args = [c, c, c, c, s] + args
            in_specs = [tok(kw, 0), tok(kw, 1), tok(bw, 1), tok(bw, 2), tok(LANE, 0)] + in_specs
            row_flags = [True] * 5 + row_flags
    st_spec = pl.BlockSpec((rows, n_groups, LANE, LANE), lambda b, t: (b, 0, 0, 0))
    if has_state:
        args.append(s0)
        in_specs.append(st_spec)
        row_flags.append(True)
    scratch = [((n_groups, LANE, LANE), F32),
               ((tb, bw), BF16),
               ((tb, bw), BF16),
               ((tb, n_groups * LANE), BF16),
               ((tb, n_groups * LANE * (tb // length)), BF16),
               ((max(tb // length, 8), n_groups * LANE), F32),
               ((n_heads, tb, tb), BF16),
               ((tb, bw), F32),
               ((2 + n_sub, length, LANE), F32)]
    body = functools.partial(_glr_kernel, kind=kind, layer=layer, length=length,
                             n_chunks=tb // length, n_groups=n_groups, n_sub=n_sub,
                             has_state=has_state, single_block=single_block)
    return _RecProgram(
        body=body, args=args, in_specs=in_specs, in_rows=row_flags,
        out_specs=[pl.BlockSpec((rows, tb, bw), lambda b, t: (b, t, 0)), st_spec],
        out_shape=[jax.ShapeDtypeStruct((bsz, t_len, bw), BF16),
                   jax.ShapeDtypeStruct((bsz, n_groups, LANE, LANE), F32)],
        scratch=scratch)


def _mlstm_kernel(*refs, length, n_heads, has_state, single_block, deferred, shared=None):
    if shared is not None:
        bias_ref, nw_ref = refs[:2]
        refs = refs[2:]
        q_ref, k_ref, v_ref, og_ref = (_cols(shared[2], i, n_heads * LANE) for i in range(4))
        s_ref = shared[4]
    else:
        q_ref, k_ref, v_ref, og_ref, s_ref, bias_ref, nw_ref = refs[:7]
        refs = refs[7:]
    if has_state:
        c0_ref, n0_ref, m0_ref = refs[:3]
        refs = refs[3:]
    o_ref, cout_ref, nout_ref, mout_ref, cn_scr, m_scr, s_scr = refs
    t = pl.program_id(1)
    dk = dv = LANE
    scale = float(dk) ** -0.5
    lane = lax.broadcasted_iota(jnp.int32, (1, LANE), 1)

    del deferred

    @_when(t == 0, single_block)
    def _():
        for h in range(n_heads):
            if has_state:
                n_cols = jnp.broadcast_to(n0_ref[h:h + 1, :], (dv, dk)).T
                cn_scr[h] = jnp.concatenate([c0_ref[h], n_cols], axis=1)
                m_scr[h:h + 1, :] = jnp.broadcast_to(m0_ref[:, h:h + 1], (1, LANE)) * LOG2E
            else:
                cn_scr[h] = jnp.zeros((dk, 2 * dv), F32)
                m_scr[h:h + 1, :] = jnp.zeros((1, LANE), F32)

    yield
    causal, tri = _chunk_consts(length)
    sel = jnp.where(lax.broadcasted_iota(jnp.int32, (8, LANE), 0)
                    == lax.broadcasted_iota(jnp.int32, (8, LANE), 1), 1.0, 0.0).astype(BF16)
    reps = -(-length // LANE)

    x = s_ref[...] + bias_ref[...]
    y = jnp.where(lane < n_heads, x, _cumsum_rows(tri, _log_sigmoid(x))) * LOG2E
    y_t = _rows_to_lanes(sel, y)
    ones = jnp.ones((length, dv), BF16)
    m_row = jnp.zeros((1, LANE), F32)
    yield

    def gate_cols(h):
        i_c = jnp.broadcast_to(y[:, h:h + 1], (length, LANE))
        b_c = jnp.broadcast_to(y[:, n_heads + h:n_heads + h + 1], (length, LANE))
        return i_c, b_c, b_c[length - 1:length, :]

    def intra_logits(h, b_c):
        z_r = y_t[h:h + 1, :] - y_t[n_heads + h:n_heads + h + 1, :]
        b_cw = jnp.concatenate([b_c] * reps, axis=1)[:, :length]
        return jnp.where(causal, b_cw + z_r, -jnp.inf)

    staged = []
    for h in range(n_heads):
        lanes = slice(h * dk, (h + 1) * dk)
        q = q_ref[:, lanes]
        kf = k_ref[:, lanes].astype(F32) * scale
        v_ext = jnp.concatenate([v_ref[:, lanes], ones], axis=1)
        i_c, b_c, b_last = gate_cols(h)
        s_scr[h] = _dot_nt(q, kf.astype(BF16))
        e_intra = b_last - b_c + i_c
        e_max = jnp.max(e_intra, axis=0, keepdims=True)
        kw = jnp.exp2(e_intra - e_max) * kf
        u_ext = _dot_tn(kw.astype(BF16), v_ext)
        m_old = m_scr[h:h + 1, :]
        cn_old = cn_scr[h]
        qc_ext = _dot(q, cn_old.astype(BF16))
        e_inter = b_last + m_old
        m_new = jnp.maximum(e_inter, e_max)
        w_c = jnp.exp2(e_inter - m_new)
        f_s = jnp.exp2(e_max - m_new)
        cn_scr[h] = (jnp.concatenate([w_c, w_c], axis=1) * cn_old
                     + jnp.concatenate([f_s, f_s], axis=1) * u_ext)
        m_scr[h:h + 1, :] = m_new
        m_row = jnp.where(lane == h, m_new, m_row)
        staged.append((v_ext, b_c + m_old, qc_ext))
        yield

    for h in range(n_heads):
        lanes = slice(h * dk, (h + 1) * dk)
        v_ext, d_inter, qc_ext = staged[h]
        d_intra = intra_logits(h, gate_cols(h)[1])
        a_t = jnp.max(d_intra, axis=1, keepdims=True)
        p = jnp.exp2(d_intra - a_t) * s_scr[h]
        num_ext = _dot(p.astype(BF16), v_ext)
        m_t = jnp.maximum(d_inter, a_t)
        f_intra = jnp.exp2(a_t - m_t)
        w_inter = jnp.exp2(d_inter - m_t)
        tot = (jnp.concatenate([f_intra, f_intra], axis=1) * num_ext
               + jnp.concatenate([w_inter, w_inter], axis=1) * qc_ext)
        hid = tot[:, :dv] / jnp.maximum(jnp.abs(tot[:, dv:]), jnp.exp2(-m_t))
        og = og_ref[:, lanes].astype(F32)
        y_out = _sigmoid(og) * _head_norm(hid, nw_ref[:, lanes])
        o_ref[:, lanes] = y_out.astype(o_ref.dtype)
        yield

    @_when(t == pl.num_programs(1) - 1, single_block)
    def _():
        for h in range(n_heads):
            cn = cn_scr[h]
            cout_ref[h] = cn[:, :dv]
            nout_ref[h:h + 1, :] = cn[:, dv:].T[0:1, :]
        mout_ref[...] = m_row * (1.0 / LOG2E)


def _mlstm(layer, bsz, t_len, p_b, p_s, bias_row, norm_w, states):
    length = tb = _pick_tile(t_len, MLSTM_CHUNK)
    bw = norm_w.shape[-1]
    n_heads = bw // LANE
    has_state = states is not None
    single_block = t_len == tb
    rows = _rows_per_step(bsz, single_block)

    def tok(width, blk):
        return pl.BlockSpec((rows, tb, width), lambda b, t: (b, t, blk))

    c_spec = pl.BlockSpec((rows, n_heads, LANE, LANE), lambda b, t: (b, 0, 0, 0))
    n_spec = pl.BlockSpec((rows, n_heads, LANE), lambda b, t: (b, 0, 0))
    m_spec = pl.BlockSpec((rows, 1, LANE), lambda b, t: (b, 0, 0))
    args = [bias_row, norm_w]
    in_specs = [pl.BlockSpec((None, 1, LANE), lambda b, t: (layer, 0, 0)),
                pl.BlockSpec((None, 1, bw), lambda b, t: (layer, 0, 0))]
    row_flags = [False] * 2
    if p_b is not None:
        p_b = p_b.reshape(bsz, t_len, -1)
        p_s = p_s.reshape(bsz, t_len, -1)
        args = [p_b, p_b, p_b, p_b, p_s] + args
        in_specs = [tok(bw, 0), tok(bw, 1), tok(bw, 2), tok(bw, 3), tok(LANE, 0)] + in_specs
        row_flags = [True] * 5 + row_flags
    if has_state:
        args += list(states)
        in_specs += [c_spec, n_spec, m_spec]
        row_flags += [True] * 3
    scratch = [((n_heads, LANE, 2 * LANE), F32), ((n_heads, LANE), F32), ((n_heads, tb, tb), F32)]
    body = functools.partial(_mlstm_kernel, length=length, n_heads=n_heads, has_state=has_state,
                             single_block=single_block)
    return _RecProgram(
        body=body, args=args, in_specs=in_specs, in_rows=row_flags,
        out_specs=[pl.BlockSpec((rows, tb, bw), lambda b, t: (b, t, 0)), c_spec, n_spec, m_spec],
        out_shape=[jax.ShapeDtypeStruct((bsz, t_len, bw), BF16),
                   jax.ShapeDtypeStruct((bsz, n_heads, LANE, LANE), F32),
                   jax.ShapeDtypeStruct((bsz, n_heads, LANE), F32),
                   jax.ShapeDtypeStruct((bsz, 1, LANE), F32)],
        scratch=scratch)


def _run_rec(progs, bsz, t_len, shared=()):
    tb = _pick_tile(t_len, TIME_BLOCK)
    rows = _rows_per_step(bsz, t_len == tb)
    n_in = [len(p.args) for p in progs]
    n_out = [len(p.out_shape) for p in progs]
    n_scr = [len(p.scratch) for p in progs]

    def kern(*refs):
        ins, outs, scrs = (refs[:sum(n_in)], refs[sum(n_in):sum(n_in) + sum(n_out)],
                           refs[sum(n_in) + sum(n_out):])
        shared_refs = scrs[sum(n_scr):]
        deferred, live = [], []
        for k, p in enumerate(progs):
            mine = (list(ins[sum(n_in[:k]):sum(n_in[:k + 1])])
                    + list(outs[sum(n_out[:k]):sum(n_out[:k + 1])])
                    + list(scrs[sum(n_scr[:k]):sum(n_scr[:k + 1])]))
            flags = list(p.in_rows) + [True] * (n_out[k] + n_scr[k])
            for g in range(rows):
                extra = {"shared": [r.at[g] for r in shared_refs]} if shared else {}
                live.append(p.body(*[r.at[g] if f else r for r, f in zip(mine, flags)],
                                   deferred=deferred, **extra))
        while live:
            still = []
            for gen in live:
                try:
                    next(gen)
                    still.append(gen)
                except StopIteration:
                    pass
            live = still
        for cond, fn in deferred:
            pl.when(cond)(fn)

    outs = pl.pallas_call(
        kern,
        grid=(bsz // rows, t_len // tb),
        in_specs=[s for p in progs for s in p.in_specs],
        out_specs=[s for p in progs for s in p.out_specs],
        out_shape=[s for p in progs for s in p.out_shape],
        scratch_shapes=([pltpu.VMEM((rows,) + shape, dt) for p in progs for shape, dt in p.scratch]
                        + [pltpu.VMEM((rows,) + tuple(shape), dt) for shape, dt in shared]),
        compiler_params=_params(("parallel", "arbitrary")),
        name="rec",
    )(*[a for p in progs for a in p.args])
    return [outs[sum(n_out[:k]):sum(n_out[:k + 1])] for k in range(len(progs))]


def _prep_weights(w_in, gla_w_a2, mlstm_gate_bias, heads):
    (ah, adk, adv), (bh, bdk, bdv), (ch, cdk, cdv), rank = heads
    sizes = (ah * adk, ah * adk, ah * adv, ah * adv,
             bh * bdk, bh * bdk, bh * bdv, bh * bdv, 2 * bh,
             ch * cdk, ch * cdk, ch * cdv, ch * cdv, rank)
    offs = [0]
    for s in sizes:
        offs.append(offs[-1] + s)
    col = lambda i: w_in[:, :, offs[i]:offs[i + 1]]
    (a_q, a_f, a_i, a_g, b_q, b_k, b_v, b_o, b_if, c_q, c_k, c_v, c_g, c_lr) = [col(i) for i in range(14)]
    pad = jnp.zeros(w_in.shape[:2] + (LANE - 2 * bh - rank,), w_in.dtype)
    groups = ([a_q, a_i, a_g], [a_f], [b_q, b_k, b_v, b_o], [c_q, c_k, c_v, c_g], [b_if, c_lr, pad])
    widths = tuple(sum(g.shape[-1] for g in grp) for grp in groups)
    w_cat = jnp.concatenate([g for grp in groups for g in grp], axis=-1).astype(BF16)
    w_mg = w_in[:, :, offs[14]:].astype(BF16)
    depth = w_in.shape[0]
    w2_pad = jnp.zeros((depth, LANE, gla_w_a2.shape[-1]), F32)
    w2_pad = w2_pad.at[:, 2 * bh:2 * bh + rank, :].set(gla_w_a2).astype(BF16)
    bias_row = jnp.zeros((depth, 1, LANE), F32).at[:, 0, :2 * bh].set(mlstm_gate_bias)
    return w_cat, widths, w_mg, w2_pad, bias_row


def _run_trunk(x, states, wts, heads):
    (ffn1_norm, ffn1_up, ffn1_down, mix_norm, w_cat, widths, w_mg, hgrn_lb_logits, hgrn_norm,
     bias_row, mlstm_norm, w2_pad, gla_b_a, gla_norm, w_branch, w_out,
     ffn2_norm, ffn2_up, ffn2_down, final_norm) = wts
    (ah, adk, adv), (bh, bdk, bdv), (ch, cdk, cdv), _ = heads
    bsz, t_len, d = x.shape
    depth = ffn1_norm.shape[0]
    x = x.reshape(bsz * t_len, d)
    new_states = ([], [], [], [], [])
    for l in range(depth):
        x = _ffn(x, ffn1_norm, ffn1_up, ffn1_down, l)
        if states is None:
            p_a = p_f = p_b = p_c = p_s = st_a = st_b = st_c = None
            extra, shared = _inproj_program(x, mix_norm, w_cat, l, widths, bsz, t_len)
        else:
            p_a, p_f, p_b, p_c, p_s = _inproj(x, mix_norm, w_cat, l, widths)
            extra, shared = None, ()
            s_hgrn, c_ml, n_ml, m_ml, s_gla = states
            st_a = s_hgrn[l]
            m_pad = jnp.zeros((bsz, 1, LANE), F32).at[:, 0, :bh].set(m_ml[l])
            st_b = (c_ml[l], n_ml[l], m_pad)
            st_c = s_gla[l].reshape(bsz, ch // 2, 2 * cdk, cdv)
        progs = [_glr("hgrn", l, bsz, t_len, (p_a, p_f, hgrn_lb_logits, hgrn_norm), st_a),
                 _mlstm(l, bsz, t_len, p_b, p_s, bias_row, mlstm_norm, st_b),
                 _glr("gla", l, bsz, t_len, (p_c, p_s, w2_pad, gla_b_a, gla_norm), st_c)]
        (o_a, s_a), (o_b, c_new, n_new, m_new), (o_c, s_c) = _run_rec(
            progs + ([extra] if extra else []), bsz, t_len, shared)[:3]
        o_a, o_b, o_c = (o.reshape(bsz * t_len, -1) for o in (o_a, o_b, o_c))
        x = _merge(x, mix_norm, o_a, o_b, o_c, w_mg, w_branch, w_out, l)
        x = _ffn(x, ffn2_norm, ffn2_up, ffn2_down, l,
                 final_w=final_norm if l == depth - 1 else None)
        for acc, s in zip(new_states, (s_a, c_new, n_new, m_new[:, 0, :bh],
                                       s_c.reshape(bsz, ch, cdk, cdv))):
            acc.append(s)
    return x.reshape(bsz, t_len, d), tuple(jnp.stack(acc) for acc in new_states)


def kernel(x_prompt, x_sample, state_hgrn, state_mlstm_c, state_mlstm_n, state_mlstm_m, state_gla, ffn1_norm, ffn1_w_up, ffn1_w_down, mix_norm, w_in, hgrn_lb_logits, hgrn_norm, mlstm_gate_bias, mlstm_norm, gla_w_a2, gla_b_a, gla_norm, w_branch, w_out, ffn2_norm, ffn2_w_up, ffn2_w_down, final_norm):
    heads = (state_hgrn.shape[2:], state_mlstm_c.shape[2:], state_gla.shape[2:], gla_w_a2.shape[1])
    assert heads[0][1:] == (LANE, LANE) and heads[1][1:] == (LANE, LANE)
    assert heads[2][1:] == (LANE // 2, LANE) and heads[2][0] % 2 == 0
    w_cat, widths, w_mg, w2_pad, bias_row = _prep_weights(w_in, gla_w_a2, mlstm_gate_bias, heads)
    row = lambda a: a[:, None, :]
    wts = (row(ffn1_norm), ffn1_w_up.astype(BF16), ffn1_w_down.astype(BF16), row(mix_norm),
           w_cat, widths, w_mg, hgrn_lb_logits, row(hgrn_norm), bias_row, row(mlstm_norm),
           w2_pad, row(gla_b_a), row(gla_norm), w_branch.astype(BF16), w_out.astype(BF16),
           row(ffn2_norm), ffn2_w_up.astype(BF16), ffn2_w_down.astype(BF16), final_norm[None, :])
    y_p, st_p = _run_trunk(x_prompt, None, wts, heads)
    sample_states = (state_hgrn, state_mlstm_c, state_mlstm_n, state_mlstm_m, state_gla)
    y_s, st_s = _run_trunk(x_sample, sample_states, wts, heads)
    return (y_p, y_s) + st_p + st_s
```

```python
import functools
from typing import Any, NamedTuple

import jax
import jax.numpy as jnp
from jax import lax
from jax.experimental import pallas as pl
from jax.experimental.pallas import tpu as pltpu

F32 = jnp.float32
BF16 = jnp.bfloat16

EPS = 1e-6
CHUNK = 64
MLSTM_CHUNK = 256
GLA_GATE_TEMP = 16.0
N_BRANCH = 3
LANE = 128
MXU_COLS = 256
TOKEN_TILE = 1024
COL_STEP = 512
TIME_BLOCK = 256
ROWS_SHORT, ROWS_LONG = 8, 2
INPROJ_DTYPES = (BF16, F32, BF16, BF16, F32)
V7X_VMEM_BYTES = 64 * 1024 * 1024
VMEM_LIMIT = V7X_VMEM_BYTES - 8 * 1024 * 1024
SAFE_DECAY_SPAN = 60.0
LOG2E = 1.4426950408889634

_NT = (((1,), (1,)), ((), ()))
_TN = (((0,), (0,)), ((), ()))


def _dot(a, b):
    return jnp.dot(a, b, preferred_element_type=F32)


def _dot_nt(a, b):
    return lax.dot_general(a, b, _NT, preferred_element_type=F32)


def _dot_tn(a, b):
    return lax.dot_general(a, b, _TN, preferred_element_type=F32)


def _rms(x, w):
    return x * lax.rsqrt(jnp.mean(x * x, axis=-1, keepdims=True) + EPS) * w


def _sigmoid(x):
    return 1.0 / (1.0 + jnp.exp2(x * (-LOG2E)))


def _log_sigmoid(x):
    return jnp.minimum(x, 0.0) - jnp.log(1.0 + jnp.exp(-jnp.abs(x)))


def _split3(x):
    hi = x.astype(BF16)
    r = x - hi.astype(F32)
    mid = r.astype(BF16)
    lo = (r - mid.astype(F32)).astype(BF16)
    return hi, mid, lo


def _cumsum_rows(tri, g):
    hi, mid, lo = _split3(g)
    return (_dot(tri, hi) + _dot(tri, mid)) + _dot(tri, lo)


def _rows_to_lanes(sel, x):
    hi, mid, lo = _split3(x)
    return (_dot_nt(sel, hi) + _dot_nt(sel, mid)) + _dot_nt(sel, lo)


def _pick_tile(n, cap):
    t = min(n, cap)
    while n % t or (t % 8 and t != n):
        t -= 1
    return t


def _row_parts(tm):
    n = 2 if tm == TOKEN_TILE else 1
    return [slice(r * (tm // n), (r + 1) * (tm // n)) for r in range(n)]


def _params(sem):
    return pltpu.CompilerParams(dimension_semantics=sem, vmem_limit_bytes=VMEM_LIMIT)


def _ffn_kernel(x_ref, nw_ref, wup_ref, wd_ref, *rest, d_ff, tf, tn, final):
    if final:
        fw_ref, o_ref, act_scr = rest
    else:
        o_ref, act_scr = rest
    d = x_ref.shape[1]
    for rows in _row_parts(x_ref.shape[0]):
        h = _rms(x_ref[rows, :], nw_ref[...]).astype(BF16)
        for c in range(0, d_ff, tf):
            g = _dot(h, wup_ref[:, c:c + tf])
            u = _dot(h, wup_ref[:, d_ff + c:d_ff + c + tf])
            act_scr[rows, c:c + tf] = (g * _sigmoid(g) * u).astype(BF16)
        for c in range(0, d, tn):
            o_ref[rows, c:c + tn] = (x_ref[rows, c:c + tn]
                                     + 0.5 * _dot(act_scr[rows, :], wd_ref[:, c:c + tn]))
        if final:
            o_ref[rows, :] = _rms(o_ref[rows, :], fw_ref[...])


def _ffn(x, norm_w, w_up, w_down, layer, final_w=None):
    n, d = x.shape
    d_ff = w_down.shape[1]
    tm = _pick_tile(n, TOKEN_TILE)
    tf = tn = MXU_COLS
    assert d_ff % tf == 0 and d % tn == 0
    final = final_w is not None
    resident = pl.Buffered(1)
    in_specs = [
        pl.BlockSpec((tm, d), lambda i: (i, 0)),
        pl.BlockSpec((None, 1, d), lambda i: (layer, 0, 0)),
        pl.BlockSpec((None, d, 2 * d_ff), lambda i: (layer, 0, 0), pipeline_mode=resident),
        pl.BlockSpec((None, d_ff, d), lambda i: (layer, 0, 0), pipeline_mode=resident),
    ]
    args = [x, norm_w, w_up, w_down]
    if final:
        in_specs.append(pl.BlockSpec((1, d), lambda i: (0, 0)))
        args.append(final_w)
    return pl.pallas_call(
        functools.partial(_ffn_kernel, d_ff=d_ff, tf=tf, tn=tn, final=final),
        grid=(n // tm,),
        in_specs=in_specs,
        out_specs=pl.BlockSpec((tm, d), lambda i: (i, 0)),
        out_shape=jax.ShapeDtypeStruct((n, d), F32),
        scratch_shapes=[pltpu.VMEM((tm, d_ff), BF16)],
        compiler_params=_params(("parallel",)),
        name="ffn",
    )(*args)


def _inproj_kernel(x_ref, nw_ref, w_ref, oa_ref, of_ref, ob_ref, oc_ref, os_ref, *, widths):
    for rows in _row_parts(x_ref.shape[0]):
        h = _rms(x_ref[rows, :], nw_ref[...]).astype(BF16)
        c0 = 0
        for ref, width in zip((oa_ref, of_ref, ob_ref, oc_ref, os_ref), widths):
            step = min(width, COL_STEP)
            for c in range(0, width, step):
                ref[rows, c:c + step] = _dot(h, w_ref[:, c0 + c:c0 + c + step]).astype(ref.dtype)
            c0 += width


def _inproj(x, norm_w, w_cat, layer, widths):
    n, d = x.shape
    tm = _pick_tile(n, TOKEN_TILE)
    dtypes = INPROJ_DTYPES
    return pl.pallas_call(
        functools.partial(_inproj_kernel, widths=widths),
        grid=(n // tm,),
        in_specs=[
            pl.BlockSpec((tm, d), lambda i: (i, 0)),
            pl.BlockSpec((None, 1, d), lambda i: (layer, 0, 0)),
            pl.BlockSpec((None, d, sum(widths)), lambda i: (layer, 0, 0),
                         pipeline_mode=pl.Buffered(1)),
        ],
        out_specs=[pl.BlockSpec((tm, w), lambda i: (i, 0)) for w in widths],
        out_shape=[jax.ShapeDtypeStruct((n, w), dt) for w, dt in zip(widths, dtypes)],
        compiler_params=_params(("parallel",)),
        name="inproj",
    )(x, norm_w, w_cat)


def _inproj_stage(x_ref, nw_ref, w_ref, *, widths, deferred, shared):
    del deferred
    h = _rms(x_ref[...], nw_ref[...]).astype(BF16)
    starts = [sum(widths[:i]) for i in range(len(widths))]
    for stage in ((0, 1, 3, 4), (2,)):
        for i in stage:
            step = min(widths[i], COL_STEP)
            for c in range(0, widths[i], step):
                cols = slice(starts[i] + c, starts[i] + c + step)
                shared[i][:, c:c + step] = _dot(h, w_ref[:, cols]).astype(shared[i].dtype)
        yield


def _inproj_program(x, norm_w, w_cat, layer, widths, bsz, t_len):
    d = x.shape[-1]
    tb = _pick_tile(t_len, TIME_BLOCK)
    rows = _rows_per_step(bsz, t_len == tb)
    prog = _RecProgram(
        body=functools.partial(_inproj_stage, widths=widths),
        args=[x.reshape(bsz, t_len, d), norm_w, w_cat],
        in_specs=[pl.BlockSpec((rows, tb, d), lambda b, t: (b, t, 0)),
                  pl.BlockSpec((None, 1, d), lambda b, t: (layer, 0, 0)),
                  pl.BlockSpec((None, d, sum(widths)), lambda b, t: (layer, 0, 0),
                               pipeline_mode=pl.Buffered(1))],
        in_rows=[True, False, False], out_specs=[], out_shape=[], scratch=[])
    return prog, [((tb, w), dt) for w, dt in zip(widths, INPROJ_DTYPES)]


def _merge_kernel(x_ref, nw_ref, oa_ref, ob_ref, oc_ref, wm_ref, wb_ref, wo_ref, o_ref):
    d = x_ref.shape[1]
    for rows in _row_parts(x_ref.shape[0]):
        x = x_ref[rows, :]
        h = _rms(x, nw_ref[...]).astype(BF16)
        merged = None
        for nb, br_ref in enumerate((oa_ref, ob_ref, oc_ref)):
            gate = _sigmoid(_dot(h, wm_ref[:, nb * d:(nb + 1) * d]))
            term = gate * _dot(br_ref[rows, :], wb_ref[nb])
            merged = term if merged is None else merged + term
        o_ref[rows, :] = x + _dot(merged.astype(BF16), wo_ref[...])


def _merge(x, norm_w, o_a, o_b, o_c, w_mg, w_branch, w_out, layer):
    n, d = x.shape
    bw = o_a.shape[1]
    tm = _pick_tile(n, TOKEN_TILE)
    br_spec = pl.BlockSpec((tm, bw), lambda i: (i, 0))
    resident = pl.Buffered(1)
    return pl.pallas_call(
        _merge_kernel,
        grid=(n // tm,),
        in_specs=[
            pl.BlockSpec((tm, d), lambda i: (i, 0)),
            pl.BlockSpec((None, 1, d), lambda i: (layer, 0, 0)),
            br_spec, br_spec, br_spec,
            pl.BlockSpec((None, d, N_BRANCH * d), lambda i: (layer, 0, 0), pipeline_mode=resident),
            pl.BlockSpec((None, N_BRANCH, bw, d), lambda i: (layer, 0, 0, 0),
                         pipeline_mode=resident),
            pl.BlockSpec((None, d, d), lambda i: (layer, 0, 0), pipeline_mode=resident),
        ],
        out_specs=pl.BlockSpec((tm, d), lambda i: (i, 0)),
        out_shape=jax.ShapeDtypeStruct((n, d), F32),
        compiler_params=_params(("parallel",)),
        name="merge",
    )(x, norm_w, o_a, o_b, o_c, w_mg, w_branch, w_out)


def _chunk_consts(length, n_chunks=1):
    tb = length * n_chunks
    rb = lax.broadcasted_iota(jnp.int32, (tb, tb), 0)
    cb = lax.broadcasted_iota(jnp.int32, (tb, tb), 1)
    causal = rb >= cb
    for i in range(1, n_chunks):
        causal = causal & ((rb < i * length) | (cb >= i * length))
    return causal, jnp.where(causal, 1.0, 0.0).astype(BF16)


def _head_norm(o, w):
    return o * lax.rsqrt(jnp.mean(o * o, axis=-1, keepdims=True) + EPS) * w


def _glr_intra_exact(q, k, g, vs, masks, tmp_ref, tri):
    length = q.shape[0]
    b = _cumsum_rows(tri, g)
    tmp_ref[0, 0:length, :] = k
    tmp_ref[1, 0:length, :] = b
    for h, v in enumerate(vs):
        tmp_ref[2 + h, 0:length, :] = v
    t_idx = lax.broadcasted_iota(jnp.int32, (length, 1), 0)

    def body(s, accs):
        k_s = tmp_ref[0, pl.ds(s, 1), :]
        b_s = tmp_ref[1, pl.ds(s, 1), :]
        a = q * k_s * jnp.exp2(jnp.minimum(b - b_s, 0.0))
        new = []
        for h, m in enumerate(masks):
            ah = a if m is None else a * m
            col = jnp.where(t_idx >= s, jnp.sum(ah, axis=1, keepdims=True), 0.0)
            new.append(accs[h] + col * tmp_ref[2 + h, pl.ds(s, 1), :])
        return tuple(new)

    init = tuple(jnp.zeros((length, v.shape[1]), F32) for v in vs)
    return lax.fori_loop(0, length, body, init)


def _lane_masks(n_sub):
    if n_sub == 1:
        return [None]
    lane = lax.broadcasted_iota(jnp.int32, (1, LANE), 1)
    w = LANE // n_sub
    return [jnp.where((lane >= h * w) & (lane < (h + 1) * w), 1.0, 0.0) for h in range(n_sub)]


def _when(cond, always):
    return (lambda f: f()) if always else pl.when(cond)


class _RecProgram(NamedTuple):
    body: Any
    args: list
    in_specs: list
    in_rows: list
    out_specs: list
    out_shape: list
    scratch: list


def _rows_per_step(bsz, single_block):
    g = min(bsz, ROWS_SHORT if single_block else ROWS_LONG)
    while bsz % g:
        g -= 1
    return g


def _cols(ref, blk, width):
    return ref.at[:, pl.ds(blk * width, width)]


def _glr_kernel(*refs, kind, layer, length, n_chunks, n_groups, n_sub, has_state, single_block,
                deferred, shared=None):
    bw, kw = n_groups * n_sub * LANE, n_groups * LANE
    if kind == "hgrn" and shared is not None:
        lbl_ref, nw_ref = refs[:2]
        refs = refs[2:]
        q_ref, v_ref, og_ref = (_cols(shared[0], i, bw) for i in range(3))
        f_ref = shared[1]
    elif kind == "hgrn":
        q_ref, v_ref, og_ref, f_ref, lbl_ref, nw_ref = refs[:6]
        refs = refs[6:]
    elif shared is not None:
        w2_ref, ba_ref, nw_ref = refs[:3]
        refs = refs[3:]
        q_ref, k_ref = _cols(shared[3], 0, kw), _cols(shared[3], 1, kw)
        v_ref, og_ref = _cols(shared[3], 1, bw), _cols(shared[3], 2, bw)
        s_ref = shared[4]
    else:
        q_ref, k_ref, v_ref, og_ref, s_ref, w2_ref, ba_ref, nw_ref = refs[:8]
        refs = refs[8:]
    if has_state:
        s0_ref, refs = refs[0], refs[1:]
    (o_ref, sout_ref, st_scr, qt_scr, qin_scr, kt_scr, kst_scr, dec_scr, p_scr, inter_scr,
     tmp_scr) = refs
    t = pl.program_id(1)
    n_heads = n_groups * n_sub
    dv = LANE
    tb = length * n_chunks

    xw = n_chunks * LANE

    def xcols(idx, c):
        return slice(idx * xw + c * LANE, idx * xw + (c + 1) * LANE)

    @_when(t == 0, single_block)
    def _():
        for gi in range(n_groups):
            if has_state:
                st_scr[gi] = s0_ref[gi].T
            else:
                st_scr[gi] = jnp.zeros((dv, LANE), F32)
        if n_chunks > 1:
            kst_scr[...] = jnp.zeros_like(kst_scr)

    yield
    causal, tri_blk = _chunk_consts(length, n_chunks)
    masks = _lane_masks(n_sub)

    if kind == "hgrn":
        logits = lbl_ref[...]
        e = jnp.exp(logits - jnp.max(logits, axis=0, keepdims=True))
        p = e / jnp.sum(e, axis=0, keepdims=True)
        lb = jnp.zeros((1, p.shape[1]), F32)
        for j in range(1, layer + 1):
            lb = lb + p[j:j + 1, :]

    def load_group(rows, gi, with_v=True):
        lanes = slice(gi * LANE, (gi + 1) * LANE)
        if kind == "hgrn":
            aq = q_ref[rows, lanes].astype(F32)
            q = aq * _sigmoid(aq)
            x = f_ref[rows, lanes]
            lbh = lb[:, lanes]
            forget = lbh + (1.0 - lbh) * _sigmoid(x)
            g = jnp.log(forget) * LOG2E
            k = 1.0 - forget
        else:
            q = q_ref[rows, lanes].astype(F32) * (float(LANE // n_sub) ** -0.5)
            k = k_ref[rows, lanes].astype(F32)
            pre = _dot(s_ref[rows, :].astype(BF16), w2_ref[:, lanes]) + ba_ref[:, lanes]
            g = _log_sigmoid(pre) * (LOG2E / GLA_GATE_TEMP)
        vs = [v_ref[rows, (gi * n_sub + h) * dv:(gi * n_sub + h + 1) * dv].astype(F32)
              for h in range(n_sub)] if with_v else None
        return q, k, g, vs

    def finish(rows, head, o):
        lanes = slice(head * dv, (head + 1) * dv)
        og = og_ref[rows, lanes].astype(F32)
        if kind == "hgrn":
            y = _head_norm(o * _sigmoid(og), nw_ref[:, lanes])
        else:
            y = _head_norm(o, nw_ref[:, lanes]) * (og * _sigmoid(og))
        o_ref[rows, lanes] = y.astype(o_ref.dtype)

    spans = []

    gates, b_of = [], []
    for g0 in range(0, n_groups, 2):
        pair = [load_group(slice(0, tb), gi, with_v=False) for gi in range(g0, min(g0 + 2, n_groups))]
        b_pair = _cumsum_rows(tri_blk, jnp.concatenate([g for _, _, g, _ in pair], axis=1))
        gates += pair
        b_of += [b_pair[:, j * LANE:(j + 1) * LANE] for j in range(len(pair))]
        yield

    def phase1(gi):
        glanes = slice(gi * LANE, (gi + 1) * LANE)
        q, k, _, _ = gates[gi]
        b = b_of[gi]
        for c in range(n_chunks):
            sl = slice(c * length, (c + 1) * length)
            bc, qc, kc = b[sl], q[sl], k[sl]
            b_mid = bc[length // 2:length // 2 + 1, :]
            b_last = bc[length - 1:length, :]
            spans.append(jnp.maximum(bc[0:1, :] - b_mid, b_mid - b_last))
            qt = qc * jnp.exp2(bc - b_mid)
            q_in = qc * jnp.exp2(bc)
            kt_scr[sl, glanes] = (kc * jnp.exp2(b_mid - bc)).astype(BF16)
            kst_scr[sl, xcols(gi, c)] = (kc * jnp.exp2(b_last - bc)).astype(BF16)
            dec_scr[c:c + 1, glanes] = jnp.exp2(b_last)
            for h, m in enumerate(masks):
                head = gi * n_sub + h
                qt_scr[sl, head * LANE:(head + 1) * LANE] = (qt if m is None else qt * m).astype(BF16)
                qin_scr[sl, head * LANE:(head + 1) * LANE] = (q_in if m is None else q_in * m).astype(BF16)

    starts = {}

    def phase2(gi):
        glanes = slice(gi * LANE, (gi + 1) * LANE)
        kt = kt_scr[:, glanes]
        kst_x = kst_scr[:, gi * xw:(gi + 1) * xw]
        upd = None
        for h, m in enumerate(masks):
            head = gi * n_sub + h
            hl = slice(head * LANE, (head + 1) * LANE)
            p_scr[head] = jnp.where(causal, _dot_nt(qt_scr[:, hl], kt), 0.0).astype(BF16)
            u = _dot_tn(v_ref[:, hl], kst_x)
            if m is not None:
                u = u * jnp.concatenate([m] * n_chunks, axis=1)
            upd = u if upd is None else upd + u
        st = st_scr[gi]
        starts[gi] = []
        for c in range(n_chunks):
            starts[gi].append(st.astype(BF16))
            st = st * dec_scr[c:c + 1, glanes] + upd[:, c * LANE:(c + 1) * LANE]
        st_scr[gi] = st

    def phase3(gi):
        for h in range(n_sub):
            head = gi * n_sub + h
            hl = slice(head * LANE, (head + 1) * LANE)
            o_intra = _dot(p_scr[head], v_ref[:, hl])
            parts = [_dot_nt(qin_scr[c * length:(c + 1) * length, hl], starts[gi][c])
                     for c in range(n_chunks)]
            o_inter = parts[0] if n_chunks == 1 else jnp.concatenate(parts, axis=0)
            inter_scr[:, hl] = o_inter
            finish(slice(0, tb), head, o_intra + o_inter)

    for gi in range(n_groups):
        phase1(gi)
        phase2(gi)
        yield
    for gi in range(n_groups):
        phase3(gi)
        yield
    span_vec = functools.reduce(jnp.maximum, spans)

    def redo_intra():
        tri = tri_blk[0:length, 0:length]

        def chunk(c, carry):
            rows = pl.ds(pl.multiple_of(c * length, length), length)
            for gi in range(n_groups):
                q, k, g, vs = load_group(rows, gi)
                intra = _glr_intra_exact(q, k, g, vs, masks, tmp_scr, tri)
                for h in range(n_sub):
                    head = gi * n_sub + h
                    finish(rows, head, intra[h] + inter_scr[rows, head * dv:(head + 1) * dv])
            return carry

        lax.fori_loop(0, n_chunks, chunk, 0)

    deferred.append((jnp.max(span_vec) > SAFE_DECAY_SPAN * LOG2E, redo_intra))
    yield

    @_when(t == pl.num_programs(1) - 1, single_block)
    def _():
        for gi in range(n_groups):
            sout_ref[gi] = st_scr[gi].T


def _time_block(kind, t_len):
    tb = _pick_tile(t_len, TIME_BLOCK)
    length = min(CHUNK, t_len) if kind == "hgrn" else tb
    assert tb % length == 0
    return length, tb


def _glr(kind, layer, bsz, t_len, ins, s0):
    length, tb = _time_block(kind, t_len)
    n_groups, n_sub = (4, 1) if kind == "hgrn" else (2, 2)
    n_heads = n_groups * n_sub
    bw = n_heads * LANE
    has_state = s0 is not None
    single_block = t_len == tb
    rows = _rows_per_step(bsz, single_block)

    def tok(width, blk):
        return pl.BlockSpec((rows, tb, width), lambda b, t: (b, t, blk))

    def full(shape):
        return pl.BlockSpec(shape, lambda b, t: (0,) * len(shape))

    if kind == "hgrn":
        a, f, lbl, nw = ins
        args = [lbl, nw]
        in_specs = [full(lbl.shape), pl.BlockSpec((None, 1, bw), lambda b, t: (layer, 0, 0))]
        row_flags = [False] * 2
        if a is not None:
            a = a.reshape(bsz, t_len, -1)
            f = f.reshape(bsz, t_len, -1)
            args = [a, a, a, f] + args
            in_specs = [tok(bw, 0), tok(bw, 1), tok(bw, 2), tok(bw, 0)] + in_specs
            row_flags = [True] * 4 + row_flags
    else:
        c, s, w2, ba, nw = ins
        kw = n_groups * LANE
        args = [w2, ba, nw]
        in_specs = [pl.BlockSpec((None, LANE, kw), lambda b, t: (layer, 0, 0)),
                    pl.BlockSpec((None, 1, kw), lambda b, t: (layer, 0, 0)),
                    pl.BlockSpec((None, 1, bw), lambda b, t: (layer, 0, 0))]
        row_flags = [False] * 3
        if c is not None:
            c = c.reshape(bsz, t_len, -1)
            s = s.reshape(bsz, t_len, -1)
            args = [c, c, c, c, s] + args
            in_specs = [tok(kw, 0), tok(kw, 1), tok(bw, 1), tok(bw, 2), tok(LANE, 0)] + in_specs
            row_flags = [True] * 5 + row_flags
    st_spec = pl.BlockSpec((rows, n_groups, LANE, LANE), lambda b, t: (b, 0, 0, 0))
    if has_state:
        args.append(s0)
        in_specs.append(pl.BlockSpec((None, rows, n_groups, LANE, LANE),
                                     lambda b, t: (layer, b, 0, 0, 0)))
        row_flags.append(True)
    scratch = [((n_groups, LANE, LANE), F32),
               ((tb, bw), BF16),
               ((tb, bw), BF16),
               ((tb, n_groups * LANE), BF16),
               ((tb, n_groups * LANE * (tb // length)), BF16),
               ((max(tb // length, 8), n_groups * LANE), F32),
               ((n_heads, tb, tb), BF16),
               ((tb, bw), F32),
               ((2 + n_sub, length, LANE), F32)]
    body = functools.partial(_glr_kernel, kind=kind, layer=layer, length=length,
                             n_chunks=tb // length, n_groups=n_groups, n_sub=n_sub,
                             has_state=has_state, single_block=single_block)
    return _RecProgram(
        body=body, args=args, in_specs=in_specs, in_rows=row_flags,
        out_specs=[pl.BlockSpec((rows, tb, bw), lambda b, t: (b, t, 0)), st_spec],
        out_shape=[jax.ShapeDtypeStruct((bsz, t_len, bw), BF16),
                   jax.ShapeDtypeStruct((bsz, n_groups, LANE, LANE), F32)],
        scratch=scratch)


def _mlstm_kernel(*refs, length, n_heads, has_state, single_block, deferred, shared=None):
    if shared is not None:
        bias_ref, nw_ref = refs[:2]
        refs = refs[2:]
        q_ref, k_ref, v_ref, og_ref = (_cols(shared[2], i, n_heads * LANE) for i in range(4))
        s_ref = shared[4]
    else:
        q_ref, k_ref, v_ref, og_ref, s_ref, bias_ref, nw_ref = refs[:7]
        refs = refs[7:]
    if has_state:
        c0_ref, n0_ref, m0_ref = refs[:3]
        refs = refs[3:]
    o_ref, cout_ref, nout_ref, mout_ref, cn_scr, m_scr, s_scr = refs
    t = pl.program_id(1)
    dk = dv = LANE
    scale = float(dk) ** -0.5
    lane = lax.broadcasted_iota(jnp.int32, (1, LANE), 1)

    del deferred

    @_when(t == 0, single_block)
    def _():
        for h in range(n_heads):
            if has_state:
                n_cols = jnp.broadcast_to(n0_ref[h:h + 1, :], (dv, dk)).T
                cn_scr[h] = jnp.concatenate([c0_ref[h], n_cols], axis=1)
                m_scr[h:h + 1, :] = jnp.broadcast_to(m0_ref[:, h:h + 1], (1, LANE)) * LOG2E
            else:
                cn_scr[h] = jnp.zeros((dk, 2 * dv), F32)
                m_scr[h:h + 1, :] = jnp.zeros((1, LANE), F32)

    yield
    causal, tri = _chunk_consts(length)
    sel = jnp.where(lax.broadcasted_iota(jnp.int32, (8, LANE), 0)
                    == lax.broadcasted_iota(jnp.int32, (8, LANE), 1), 1.0, 0.0).astype(BF16)
    reps = -(-length // LANE)

    x = s_ref[...] + bias_ref[...]
    y = jnp.where(lane < n_heads, x, _cumsum_rows(tri, _log_sigmoid(x))) * LOG2E
    y_t = _rows_to_lanes(sel, y)
    ones = jnp.ones((length, dv), BF16)
    m_row = jnp.zeros((1, LANE), F32)
    yield

    def gate_cols(h):
        i_c = jnp.broadcast_to(y[:, h:h + 1], (length, LANE))
        b_c = jnp.broadcast_to(y[:, n_heads + h:n_heads + h + 1], (length, LANE))
        return i_c, b_c, b_c[length - 1:length, :]

    def intra_logits(h, b_c):
        z_r = y_t[h:h + 1, :] - y_t[n_heads + h:n_heads + h + 1, :]
        b_cw = jnp.concatenate([b_c] * reps, axis=1)[:, :length]
        return jnp.where(causal, b_cw + z_r, -jnp.inf)

    staged = []
    for h in range(n_heads):
        lanes = slice(h * dk, (h + 1) * dk)
        q = q_ref[:, lanes]
        kf = k_ref[:, lanes].astype(F32) * scale
        v_ext = jnp.concatenate([v_ref[:, lanes], ones], axis=1)
        i_c, b_c, b_last = gate_cols(h)
        s_scr[h] = _dot_nt(q, kf.astype(BF16))
        e_intra = b_last - b_c + i_c
        e_max = jnp.max(e_intra, axis=0, keepdims=True)
        kw = jnp.exp2(e_intra - e_max) * kf
        u_ext = _dot_tn(kw.astype(BF16), v_ext)
        m_old = m_scr[h:h + 1, :]
        cn_old = cn_scr[h]
        qc_ext = _dot(q, cn_old.astype(BF16))
        e_inter = b_last + m_old
        m_new = jnp.maximum(e_inter, e_max)
        w_c = jnp.exp2(e_inter - m_new)
        f_s = jnp.exp2(e_max - m_new)
        cn_scr[h] = (jnp.concatenate([w_c, w_c], axis=1) * cn_old
                     + jnp.concatenate([f_s, f_s], axis=1) * u_ext)
        m_scr[h:h + 1, :] = m_new
        m_row = jnp.where(lane == h, m_new, m_row)
        staged.append((v_ext, b_c + m_old, qc_ext))
        yield

    for h in range(n_heads):
        lanes = slice(h * dk, (h + 1) * dk)
        v_ext, d_inter, qc_ext = staged[h]
        d_intra = intra_logits(h, gate_cols(h)[1])
        a_t = jnp.max(d_intra, axis=1, keepdims=True)
        p = jnp.exp2(d_intra - a_t) * s_scr[h]
        num_ext = _dot(p.astype(BF16), v_ext)
        m_t = jnp.maximum(d_inter, a_t)
        f_intra = jnp.exp2(a_t - m_t)
        w_inter = jnp.exp2(d_inter - m_t)
        tot = (jnp.concatenate([f_intra, f_intra], axis=1) * num_ext
               + jnp.concatenate([w_inter, w_inter], axis=1) * qc_ext)
        hid = tot[:, :dv] / jnp.maximum(jnp.abs(tot[:, dv:]), jnp.exp2(-m_t))
        og = og_ref[:, lanes].astype(F32)
        y_out = _sigmoid(og) * _head_norm(hid, nw_ref[:, lanes])
        o_ref[:, lanes] = y_out.astype(o_ref.dtype)
        yield

    @_when(t == pl.num_programs(1) - 1, single_block)
    def _():
        for h in range(n_heads):
            cn = cn_scr[h]
            cout_ref[h] = cn[:, :dv]
            nout_ref[h:h + 1, :] = cn[:, dv:].T[0:1, :]
        mout_ref[...] = m_row * (1.0 / LOG2E)


def _mlstm(layer, bsz, t_len, p_b, p_s, bias_row, norm_w, states):
    length = tb = _pick_tile(t_len, MLSTM_CHUNK)
    bw = norm_w.shape[-1]
    n_heads = bw // LANE
    has_state = states is not None
    single_block = t_len == tb
    rows = _rows_per_step(bsz, single_block)

    def tok(width, blk):
        return pl.BlockSpec((rows, tb, width), lambda b, t: (b, t, blk))

    c_spec = pl.BlockSpec((rows, n_heads, LANE, LANE), lambda b, t: (b, 0, 0, 0))
    n_spec = pl.BlockSpec((rows, n_heads, LANE), lambda b, t: (b, 0, 0))
    m_spec = pl.BlockSpec((rows, 1, LANE), lambda b, t: (b, 0, 0))
    args = [bias_row, norm_w]
    in_specs = [pl.BlockSpec((None, 1, LANE), lambda b, t: (layer, 0, 0)),
                pl.BlockSpec((None, 1, bw), lambda b, t: (layer, 0, 0))]
    row_flags = [False] * 2
    if p_b is not None:
        p_b = p_b.reshape(bsz, t_len, -1)
        p_s = p_s.reshape(bsz, t_len, -1)
        args = [p_b, p_b, p_b, p_b, p_s] + args
        in_specs = [tok(bw, 0), tok(bw, 1), tok(bw, 2), tok(bw, 3), tok(LANE, 0)] + in_specs
        row_flags = [True] * 5 + row_flags
    if has_state:
        args += list(states)
        in_specs += [pl.BlockSpec((None, rows, n_heads, LANE, LANE), lambda b, t: (layer, b, 0, 0, 0)),
                     pl.BlockSpec((None, rows, n_heads, LANE), lambda b, t: (layer, b, 0, 0)),
                     pl.BlockSpec((None, rows, 1, LANE), lambda b, t: (layer, b, 0, 0))]
        row_flags += [True] * 3
    scratch = [((n_heads, LANE, 2 * LANE), F32), ((n_heads, LANE), F32), ((n_heads, tb, tb), F32)]
    body = functools.partial(_mlstm_kernel, length=length, n_heads=n_heads, has_state=has_state,
                             single_block=single_block)
    return _RecProgram(
        body=body, args=args, in_specs=in_specs, in_rows=row_flags,
        out_specs=[pl.BlockSpec((rows, tb, bw), lambda b, t: (b, t, 0)), c_spec, n_spec, m_spec],
        out_shape=[jax.ShapeDtypeStruct((bsz, t_len, bw), BF16),
                   jax.ShapeDtypeStruct((bsz, n_heads, LANE, LANE), F32),
                   jax.ShapeDtypeStruct((bsz, n_heads, LANE), F32),
                   jax.ShapeDtypeStruct((bsz, 1, LANE), F32)],
        scratch=scratch)


def _run_rec(progs, bsz, t_len, shared=()):
    tb = _pick_tile(t_len, TIME_BLOCK)
    rows = _rows_per_step(bsz, t_len == tb)
    n_in = [len(p.args) for p in progs]
    n_out = [len(p.out_shape) for p in progs]
    n_scr = [len(p.scratch) for p in progs]

    def kern(*refs):
        ins, outs, scrs = (refs[:sum(n_in)], refs[sum(n_in):sum(n_in) + sum(n_out)],
                           refs[sum(n_in) + sum(n_out):])
        shared_refs = scrs[sum(n_scr):]
        deferred, live = [], []
        for k, p in enumerate(progs):
            mine = (list(ins[sum(n_in[:k]):sum(n_in[:k + 1])])
                    + list(outs[sum(n_out[:k]):sum(n_out[:k + 1])])
                    + list(scrs[sum(n_scr[:k]):sum(n_scr[:k + 1])]))
            flags = list(p.in_rows) + [True] * (n_out[k] + n_scr[k])
            for g in range(rows):
                extra = {"shared": [r.at[g] for r in shared_refs]} if shared else {}
                live.append(p.body(*[r.at[g] if f else r for r, f in zip(mine, flags)],
                                   deferred=deferred, **extra))
        while live:
            still = []
            for gen in live:
                try:
                    next(gen)
                    still.append(gen)
                except StopIteration:
                    pass
            live = still
        for cond, fn in deferred:
            pl.when(cond)(fn)

    outs = pl.pallas_call(
        kern,
        grid=(bsz // rows, t_len // tb),
        in_specs=[s for p in progs for s in p.in_specs],
        out_specs=[s for p in progs for s in p.out_specs],
        out_shape=[s for p in progs for s in p.out_shape],
        scratch_shapes=([pltpu.VMEM((rows,) + shape, dt) for p in progs for shape, dt in p.scratch]
                        + [pltpu.VMEM((rows,) + tuple(shape), dt) for shape, dt in shared]),
        compiler_params=_params(("parallel", "arbitrary")),
        name="rec",
    )(*[a for p in progs for a in p.args])
    return [outs[sum(n_out[:k]):sum(n_out[:k + 1])] for k in range(len(progs))]


def _prep_weights(w_in, gla_w_a2, mlstm_gate_bias, heads):
    (ah, adk, adv), (bh, bdk, bdv), (ch, cdk, cdv), rank = heads
    sizes = (ah * adk, ah * adk, ah * adv, ah * adv,
             bh * bdk, bh * bdk, bh * bdv, bh * bdv, 2 * bh,
             ch * cdk, ch * cdk, ch * cdv, ch * cdv, rank)
    offs = [0]
    for s in sizes:
        offs.append(offs[-1] + s)
    col = lambda i: w_in[:, :, offs[i]:offs[i + 1]]
    (a_q, a_f, a_i, a_g, b_q, b_k, b_v, b_o, b_if, c_q, c_k, c_v, c_g, c_lr) = [col(i) for i in range(14)]
    pad = jnp.zeros(w_in.shape[:2] + (LANE - 2 * bh - rank,), w_in.dtype)
    groups = ([a_q, a_i, a_g], [a_f], [b_q, b_k, b_v, b_o], [c_q, c_k, c_v, c_g], [b_if, c_lr, pad])
    widths = tuple(sum(g.shape[-1] for g in grp) for grp in groups)
    w_cat = jnp.concatenate([g for grp in groups for g in grp], axis=-1).astype(BF16)
    w_mg = w_in[:, :, offs[14]:].astype(BF16)
    depth = w_in.shape[0]
    w2_pad = jnp.zeros((depth, LANE, gla_w_a2.shape[-1]), F32)
    w2_pad = w2_pad.at[:, 2 * bh:2 * bh + rank, :].set(gla_w_a2).astype(BF16)
    bias_row = jnp.zeros((depth, 1, LANE), F32).at[:, 0, :2 * bh].set(mlstm_gate_bias)
    return w_cat, widths, w_mg, w2_pad, bias_row


def _run_trunk(x, states, wts, heads):
    (ffn1_norm, ffn1_up, ffn1_down, mix_norm, w_cat, widths, w_mg, hgrn_lb_logits, hgrn_norm,
     bias_row, mlstm_norm, w2_pad, gla_b_a, gla_norm, w_branch, w_out,
     ffn2_norm, ffn2_up, ffn2_down, final_norm) = wts
    (ah, adk, adv), (bh, bdk, bdv), (ch, cdk, cdv), _ = heads
    bsz, t_len, d = x.shape
    depth = ffn1_norm.shape[0]
    x = x.reshape(bsz * t_len, d)
    new_states = ([], [], [], [], [])
    for l in range(depth):
        x = _ffn(x, ffn1_norm, ffn1_up, ffn1_down, l)
        if states is None:
            p_a = p_f = p_b = p_c = p_s = st_a = st_b = st_c = None
            extra, shared = _inproj_program(x, mix_norm, w_cat, l, widths, bsz, t_len)
        else:
            p_a, p_f, p_b, p_c, p_s = _inproj(x, mix_norm, w_cat, l, widths)
            extra, shared = None, ()
            s_hgrn, c_ml, n_ml, m_ml, s_gla = states
            st_a = s_hgrn
            m_pad = jnp.zeros((depth, bsz, 1, LANE), F32).at[:, :, 0, :bh].set(m_ml)
            st_b = (c_ml, n_ml, m_pad)
            st_c = s_gla.reshape(depth, bsz, ch // 2, 2 * cdk, cdv)
        progs = [_glr("hgrn", l, bsz, t_len, (p_a, p_f, hgrn_lb_logits, hgrn_norm), st_a),
                 _mlstm(l, bsz, t_len, p_b, p_s, bias_row, mlstm_norm, st_b),
                 _glr("gla", l, bsz, t_len, (p_c, p_s, w2_pad, gla_b_a, gla_norm), st_c)]
        (o_a, s_a), (o_b, c_new, n_new, m_new), (o_c, s_c) = _run_rec(
            progs + ([extra] if extra else []), bsz, t_len, shared)[:3]
        o_a, o_b, o_c = (o.reshape(bsz * t_len, -1) for o in (o_a, o_b, o_c))
        x = _merge(x, mix_norm, o_a, o_b, o_c, w_mg, w_branch, w_out, l)
        x = _ffn(x, ffn2_norm, ffn2_up, ffn2_down, l,
                 final_w=final_norm if l == depth - 1 else None)
        for acc, s in zip(new_states, (s_a, c_new, n_new, m_new[:, 0, :bh],
                                       s_c.reshape(bsz, ch, cdk, cdv))):
            acc.append(s)
    return x.reshape(bsz, t_len, d), tuple(jnp.stack(acc) for acc in new_states)


def kernel(x_prompt, x_sample, state_hgrn, state_mlstm_c, state_mlstm_n, state_mlstm_m, state_gla, ffn1_norm, ffn1_w_up, ffn1_w_down, mix_norm, w_in, hgrn_lb_logits, hgrn_norm, mlstm_gate_bias, mlstm_norm, gla_w_a2, gla_b_a, gla_norm, w_branch, w_out, ffn2_norm, ffn2_w_up, ffn2_w_down, final_norm):
    heads = (state_hgrn.shape[2:], state_mlstm_c.shape[2:], state_gla.shape[2:], gla_w_a2.shape[1])
    assert heads[0][1:] == (LANE, LANE) and heads[1][1:] == (LANE, LANE)
    assert heads[2][1:] == (LANE // 2, LANE) and heads[2][0] % 2 == 0
    w_cat, widths, w_mg, w2_pad, bias_row = _prep_weights(w_in, gla_w_a2, mlstm_gate_bias, heads)
    row = lambda a: a[:, None, :]
    wts = (row(ffn1_norm), ffn1_w_up.astype(BF16), ffn1_w_down.astype(BF16), row(mix_norm),
           w_cat, widths, w_mg, hgrn_lb_logits, row(hgrn_norm), bias_row, row(mlstm_norm),
           w2_pad, row(gla_b_a), row(gla_norm), w_branch.astype(BF16), w_out.astype(BF16),
           row(ffn2_norm), ffn2_w_up.astype(BF16), ffn2_w_down.astype(BF16), final_norm[None, :])
    y_p, st_p = _run_trunk(x_prompt, None, wts, heads)
    sample_states = (state_hgrn, state_mlstm_c, state_mlstm_n, state_mlstm_m, state_gla)
    y_s, st_s = _run_trunk(x_sample, sample_states, wts, heads)
    return (y_p, y_s) + st_p + st_s
```
